```python
import math
import jax, jax.numpy as jnp
from jax import lax
import numpy as np

D_MODEL = 1024
BATCH = 4
SEQ = 4096
DEPTH = 1

GRID_W = 64
CTX_LEN = 256
D_A = 2 * D_MODEL
A_HEADS = 4
A_HEAD_DIM = D_A // A_HEADS
QKV_BLOCK = 4
A_NB = D_A // QKV_BLOCK
A_CONV = 3
CHUNK = 128
D_B = D_MODEL
B_CONV = 3
FILTER_BANDS = 16
FILTER_EMB = 1 + 2 * FILTER_BANDS
FILTER_HIDDEN = 64
DECAY_TARGET = 1e-2
FAST_DECAY_PCT = 0.3
SLOW_DECAY_PCT = 1.5
MAX_DECAY = math.log(DECAY_TARGET) / FAST_DECAY_PCT
MIN_DECAY = math.log(DECAY_TARGET) / SLOW_DECAY_PCT
FILTER_SHIFT = 0.05
D_FF = 2816
EPS = 1e-6
N_MOD = 9
IN_WIDTH = 2 * D_A + 3 * D_B + 2 * D_MODEL
IN_SPLITS = (D_A, 2 * D_A, 2 * D_A + 3 * D_B, 2 * D_A + 3 * D_B + D_MODEL)

kernel_name = "hybrid_mlstm_hyena_dit_layer"

F32 = jnp.float32


def rmsnorm(x, g):
    xf = x.astype(F32)
    y = xf * lax.rsqrt(jnp.mean(xf * xf, -1, keepdims=True) + EPS)
    return (y * g.astype(F32)).astype(x.dtype)


def modulate(h, shift, scale):
    return h * (1.0 + scale[:, None, :]) + shift[:, None, :]


def swiglu(h, w_up, w_down):
    g, u = jnp.split(h @ w_up, 2, axis=-1)
    return (jax.nn.silu(g) * u) @ w_down


def short_conv(x, w, b):
    K = w.shape[0]
    L = x.shape[1]
    xp = jnp.pad(x, ((0, 0), (K // 2, K // 2), (0, 0)))
    y = xp[:, 0:L] * w[0]
    for j in range(1, K):
        y = y + xp[:, j:j + L] * w[j]
    return y + b


def sincos_2d(L, dtype):
    rows = L // GRID_W
    t = jnp.arange(rows * GRID_W)
    r = (t // GRID_W).astype(F32)
    c = (t % GRID_W).astype(F32)
    nf = D_MODEL // 4
    omega = 1.0 / (10000.0 ** (jnp.arange(nf, dtype=F32) / nf))

    def emb(p):
        a = p[:, None] * omega[None, :]
        return jnp.concatenate([jnp.sin(a), jnp.cos(a)], -1)

    return jnp.concatenate([emb(r), emb(c)], -1).astype(dtype)


def blockdiag(x, w):
    B, L, _ = x.shape
    xb = x.reshape(B, L, A_NB, QKV_BLOCK)
    return jnp.einsum('blni,nij->blnj', xb, w).reshape(B, L, A_NB * QKV_BLOCK)


def flip_t(a):
    return jnp.flip(a, axis=2)


def zero_state(B):
    return (jnp.zeros((B, A_HEADS, A_HEAD_DIM, A_HEAD_DIM), F32),
            jnp.zeros((B, A_HEADS, A_HEAD_DIM), F32),
            jnp.zeros((B, A_HEADS), F32))


def mlstm_features(xm, lp):
    B, L, _ = xm.shape
    xc = jax.nn.silu(short_conv(xm, lp['a_conv_w'], lp['a_conv_b']))
    q = blockdiag(xc, lp['a_wq'])
    k = blockdiag(xc, lp['a_wk'])
    v = blockdiag(xm, lp['a_wv'])
    g = jnp.concatenate([q, k, v], -1) @ lp['a_w_gate'] + lp['a_b_gate']
    g = jnp.moveaxis(g.astype(F32), -1, 1)
    ig_f, fg_f, ig_b, fg_b = jnp.split(g, 4, axis=1)

    def heads(a):
        return jnp.moveaxis(a.reshape(B, L, A_HEADS, A_HEAD_DIM), 2, 1).astype(F32)

    q = heads(q) * (A_HEAD_DIM ** -0.5)
    return (xc, q, heads(k), heads(v),
            (ig_f, jax.nn.log_sigmoid(fg_f)), (ig_b, jax.nn.log_sigmoid(fg_b)))


def _state_update(k, v, ig, b, state):
    C, n, m = state
    b_last = b[..., -1]
    log_w = b_last[..., None] - b + ig
    m_new = jnp.maximum(b_last + m, jnp.max(log_w, -1))
    decay = jnp.exp(b_last + m - m_new)
    kw = k * jnp.exp(log_w - m_new[..., None])[..., None]
    C_new = decay[..., None, None] * C + jnp.einsum('bhsk,bhsv->bhkv', kw, v)
    n_new = decay[..., None] * n + jnp.sum(kw, axis=2)
    return (C_new, n_new, m_new)


def _chunk_step(state, inp):
    q, k, v, ig, lf = inp
    C, n, m = state
    Lc = q.shape[2]
    b = jnp.cumsum(lf, -1)
    lower = jnp.tril(jnp.ones((Lc, Lc), dtype=bool))
    log_d = jnp.where(lower, b[..., :, None] - b[..., None, :] + ig[..., None, :], -jnp.inf)
    m_inter = b + m[..., None]
    m_t = jnp.maximum(m_inter, jnp.max(log_d, -1))
    d = jnp.exp(log_d - m_t[..., None])
    inter = jnp.exp(m_inter - m_t)
    s = jnp.einsum('bhtk,bhsk->bhts', q, k) * d
    num = jnp.einsum('bhts,bhsv->bhtv', s, v) + inter[..., None] * jnp.einsum('bhtk,bhkv->bhtv', q, C)
    den = jnp.sum(s, -1) + inter * jnp.einsum('bhtk,bhk->bht', q, n)
    h = num / jnp.maximum(jnp.abs(den), jnp.exp(-m_t))[..., None]
    return _state_update(k, v, ig, b, state), h


def mlstm_scan(q, k, v, ig, lf, state):
    B, H, L, dh = q.shape
    nc = L // CHUNK

    def to_chunks(a):
        a = a.reshape(B, H, nc, CHUNK, *a.shape[3:])
        return jnp.moveaxis(a, 2, 0)

    state, h = lax.scan(_chunk_step, state, (to_chunks(q), to_chunks(k), to_chunks(v), to_chunks(ig), to_chunks(lf)))
    h = jnp.moveaxis(h, 0, 2).reshape(B, H, L, dh)
    return h, state


def mlstm_out(h, xc, z, lp):
    B, H, L, dh = h.shape
    hn = h * lax.rsqrt(jnp.mean(h * h, -1, keepdims=True) + EPS)
    hn = jnp.moveaxis(hn, 1, 2).reshape(B, L, H * dh).astype(xc.dtype)
    return jax.nn.sigmoid(z) * (hn * lp['a_norm_g'] + lp['a_skip'] * xc)


def mlstm_mixer(xm, z, lp, init_f, init_b):
    xc, q, k, v, (ig_f, lf_f), (ig_b, lf_b) = mlstm_features(xm, lp)
    h_f, st_f = mlstm_scan(q, k, v, ig_f, lf_f, init_f)
    h_b, st_b = mlstm_scan(flip_t(q), flip_t(k), flip_t(v), flip_t(ig_b), flip_t(lf_b), init_b)
    return mlstm_out(h_f + flip_t(h_b), xc, z, lp), st_f, st_b


def mlstm_context_states(xm, lp):
    _, q, k, v, (ig_f, lf_f), (ig_b, lf_b) = mlstm_features(xm, lp)
    init = zero_state(k.shape[0])
    st_f = _state_update(k, v, ig_f, jnp.cumsum(lf_f, -1), init)
    st_b = _state_update(flip_t(k), flip_t(v), flip_t(ig_b), jnp.cumsum(flip_t(lf_b), -1), init)
    return st_f, st_b


def hyena_filters(L, lp):
    t = jnp.linspace(0.0, 1.0, L, dtype=F32)[:, None]
    w = 2.0 * math.pi * jnp.arange(L, dtype=F32)[:, None] / L
    f = jnp.linspace(1e-4, FILTER_BANDS - 1, FILTER_BANDS, dtype=F32)[None, :]
    z = jnp.concatenate([t, jnp.cos(f * w), -jnp.sin(f * w)], -1)
    freq = lp['b_filt_freq'].astype(F32)
    a = jnp.sin(freq * (z @ lp['b_filt_w1'].astype(F32) + lp['b_filt_b1'].astype(F32)))
    a = jnp.sin(freq * (a @ lp['b_filt_w2'].astype(F32) + lp['b_filt_b2'].astype(F32)))
    a = jnp.sin(freq * (a @ lp['b_filt_w3'].astype(F32) + lp['b_filt_b3'].astype(F32)))
    hf = a @ lp['b_filt_w4'].astype(F32)
    deltas = jnp.abs(jnp.linspace(MIN_DECAY, MAX_DECAY, D_B, dtype=F32))
    window = jnp.exp(-t * deltas[None, :]) + FILTER_SHIFT
    h_fwd, h_bwd = jnp.split(hf, 2, axis=-1)
    return h_fwd * window, h_bwd * window


def bidir_long_conv(u, h_fwd, h_bwd, d_skip):
    B, L, C = u.shape
    k = jnp.concatenate([h_fwd, jnp.zeros((1, C), F32), jnp.flip(h_bwd[1:], axis=0)], 0)
    uf = jnp.fft.rfft(u.astype(F32), n=2 * L, axis=1)
    kf = jnp.fft.rfft(k, n=2 * L, axis=0)
    y = jnp.fft.irfft(uf * kf[None], n=2 * L, axis=1)[:, :L]
    return (y + u.astype(F32) * d_skip.astype(F32)).astype(u.dtype)


def hyena_mixer(hy, lp):
    L = hy.shape[1]
    hy = short_conv(hy, lp['b_conv_w'], lp['b_conv_b'])
    x0, x1, v = jnp.split(hy, 3, axis=-1)
    h_fwd, h_bwd = hyena_filters(L, lp)
    return x0 * bidir_long_conv(x1 * v, h_fwd, h_bwd, lp['b_skip'])


def parallel_mixers(h, lp, init_f, init_b):
    xm, z, hy, ga, gb = jnp.split(h @ lp['w_in'], IN_SPLITS, axis=-1)
    ya, st_f, st_b = mlstm_mixer(xm, z, lp, init_f, init_b)
    yb = hyena_mixer(hy, lp)
    mix = jax.nn.sigmoid(ga) * (ya @ lp['w_pa']) + jax.nn.sigmoid(gb) * (yb @ lp['w_pb'])
    return mix @ lp['w_out'], st_f, st_b


def layer(x, ctx, c_silu, c_ctx_silu, lp, last):
    m = jnp.split(c_silu @ lp['w_ada'] + lp['b_ada'], N_MOD, axis=-1)
    mc = jnp.split(c_ctx_silu @ lp['w_ada'] + lp['b_ada'], N_MOD, axis=-1)
    g = lp['norm_g']

    def pre(t, i, mods):
        return modulate(rmsnorm(t, g[i]), mods[3 * i], mods[3 * i + 1])

    x = x + 0.5 * m[2][:, None, :] * swiglu(pre(x, 0, m), lp['ffn1_up'], lp['ffn1_down'])
    ctx = ctx + 0.5 * mc[2][:, None, :] * swiglu(pre(ctx, 0, mc), lp['ffn1_up'], lp['ffn1_down'])
    h = pre(x, 1, m)
    hc = pre(ctx, 1, mc)
    if last:
        st_f, st_b = mlstm_context_states(hc @ lp['w_in'][:, :D_A], lp)
    else:
        init = zero_state(ctx.shape[0])
        ctx_mix, st_f, st_b = parallel_mixers(hc, lp, init, init)
        ctx = ctx + mc[5][:, None, :] * ctx_mix
        ctx = ctx + 0.5 * mc[8][:, None, :] * swiglu(pre(ctx, 2, mc), lp['ffn2_up'], lp['ffn2_down'])
    x_mix, _, _ = parallel_mixers(h, lp, st_f, st_b)
    x = x + m[5][:, None, :] * x_mix
    x = x + 0.5 * m[8][:, None, :] * swiglu(pre(x, 2, m), lp['ffn2_up'], lp['ffn2_down'])
    return x, ctx


def setup_inputs(seed: int = 0) -> dict:
    key = jax.random.key(seed)
    ks = iter(jax.random.split(key, 48))

    def nrm(shape, scale):
        return scale * jax.random.normal(next(ks), shape, F32)

    Dp = DEPTH
    f_lin = jnp.linspace(3.0, 6.0, A_HEADS, dtype=F32)
    a_b_gate = jnp.concatenate([
        nrm((Dp, A_HEADS), 0.1),
        f_lin + nrm((Dp, A_HEADS), 0.1),
        nrm((Dp, A_HEADS), 0.1),
        f_lin + nrm((Dp, A_HEADS), 0.1)], axis=-1)
    return {
        'x': nrm((BATCH, SEQ, D_MODEL), 1.0),
        'c': nrm((BATCH, D_MODEL), 1.0),
        'ctx': nrm((BATCH, CTX_LEN, D_MODEL), 1.0),
        'c_ctx': nrm((D_MODEL,), 1.0),
        'w_ada': nrm((Dp, D_MODEL, N_MOD * D_MODEL), 0.5 * D_MODEL ** -0.5),
        'b_ada': nrm((Dp, N_MOD * D_MODEL), 0.02),
        'norm_g': 1.0 + nrm((Dp, 3, D_MODEL), 0.05),
        'ffn1_up': nrm((Dp, D_MODEL, 2 * D_FF), D_MODEL ** -0.5),
        'ffn1_down': nrm((Dp, D_FF, D_MODEL), D_FF ** -0.5),
        'ffn2_up': nrm((Dp, D_MODEL, 2 * D_FF), D_MODEL ** -0.5),
        'ffn2_down': nrm((Dp, D_FF, D_MODEL), D_FF ** -0.5),
        'w_in': nrm((Dp, D_MODEL, IN_WIDTH), D_MODEL ** -0.5),
        'a_conv_w': nrm((Dp, A_CONV, D_A), 0.5),
        'a_conv_b': nrm((Dp, D_A), 0.02),
        'a_wq': nrm((Dp, A_NB, QKV_BLOCK, QKV_BLOCK), QKV_BLOCK ** -0.5),
        'a_wk': nrm((Dp, A_NB, QKV_BLOCK, QKV_BLOCK), QKV_BLOCK ** -0.5),
        'a_wv': nrm((Dp, A_NB, QKV_BLOCK, QKV_BLOCK), QKV_BLOCK ** -0.5),
        'a_w_gate': nrm((Dp, 3 * D_A, 4 * A_HEADS), (3 * D_A) ** -0.5),
        'a_b_gate': a_b_gate,
        'a_norm_g': 1.0 + nrm((Dp, D_A), 0.05),
        'a_skip': 1.0 + nrm((Dp, D_A), 0.05),
        'b_conv_w': nrm((Dp, B_CONV, 3 * D_B), 0.5),
        'b_conv_b': nrm((Dp, 3 * D_B), 0.02),
        'b_filt_w1': nrm((Dp, FILTER_EMB, FILTER_HIDDEN), FILTER_EMB ** -0.5),
        'b_filt_b1': nrm((Dp, FILTER_HIDDEN), 0.1),
        'b_filt_w2': nrm((Dp, FILTER_HIDDEN, FILTER_HIDDEN), FILTER_HIDDEN ** -0.5),
        'b_filt_b2': nrm((Dp, FILTER_HIDDEN), 0.1),
        'b_filt_w3': nrm((Dp, FILTER_HIDDEN, FILTER_HIDDEN), FILTER_HIDDEN ** -0.5),
        'b_filt_b3': nrm((Dp, FILTER_HIDDEN), 0.1),
        'b_filt_w4': nrm((Dp, FILTER_HIDDEN, 2 * D_B), 0.1 * FILTER_HIDDEN ** -0.5),
        'b_filt_freq': 1.0 + nrm((Dp, FILTER_HIDDEN), 0.05),
        'b_skip': nrm((Dp, D_B), 0.5),
        'w_pa': nrm((Dp, D_A, D_MODEL), D_A ** -0.5),
        'w_pb': nrm((Dp, D_B, D_MODEL), D_B ** -0.5),
        'w_out': nrm((Dp, D_MODEL, D_MODEL), D_MODEL ** -0.5),
        'final_g': 1.0 + nrm((D_MODEL,), 0.05),
    }


def reference(x, c, ctx, c_ctx, w_ada, b_ada, norm_g, ffn1_up, ffn1_down, ffn2_up, ffn2_down,
              w_in, a_conv_w, a_conv_b, a_wq, a_wk, a_wv, a_w_gate, a_b_gate, a_norm_g, a_skip,
              b_conv_w, b_conv_b, b_filt_w1, b_filt_b1, b_filt_w2, b_filt_b2, b_filt_w3, b_filt_b3,
              b_filt_w4, b_filt_freq, b_skip, w_pa, w_pb, w_out, final_g):
    L = x.shape[1]
    x = x + sincos_2d(L, x.dtype)[None]
    c_silu = jax.nn.silu(c)
    c_ctx_silu = jax.nn.silu(c_ctx)[None]
    for l in range(DEPTH):
        lp = dict(w_ada=w_ada[l], b_ada=b_ada[l], norm_g=norm_g[l],
                  ffn1_up=ffn1_up[l], ffn1_down=ffn1_down[l], ffn2_up=ffn2_up[l], ffn2_down=ffn2_down[l],
                  w_in=w_in[l], a_conv_w=a_conv_w[l], a_conv_b=a_conv_b[l],
                  a_wq=a_wq[l], a_wk=a_wk[l], a_wv=a_wv[l], a_w_gate=a_w_gate[l], a_b_gate=a_b_gate[l],
                  a_norm_g=a_norm_g[l], a_skip=a_skip[l],
                  b_conv_w=b_conv_w[l], b_conv_b=b_conv_b[l],
                  b_filt_w1=b_filt_w1[l], b_filt_b1=b_filt_b1[l], b_filt_w2=b_filt_w2[l], b_filt_b2=b_filt_b2[l],
                  b_filt_w3=b_filt_w3[l], b_filt_b3=b_filt_b3[l], b_filt_w4=b_filt_w4[l],
                  b_filt_freq=b_filt_freq[l], b_skip=b_skip[l],
                  w_pa=w_pa[l], w_pb=w_pb[l], w_out=w_out[l])
        x, ctx = layer(x, ctx, c_silu, c_ctx_silu, lp, l == DEPTH - 1)
    return rmsnorm(x, final_g)
```

```python
import functools
import math

import numpy as np
import jax
import jax.numpy as jnp
from jax import lax
from jax.experimental import pallas as pl
from jax.experimental.pallas import tpu as pltpu

F32 = jnp.float32
BF16 = jnp.bfloat16

D_MODEL = 1024
D_A = 2048
A_HEADS = 4
DH = D_A // A_HEADS
QKV_BLOCK = 4
D_B = 1024
D_FF = 2816
EPS = 1e-6
N_MOD = 9
GRID_W = 64
CHUNK = 128
FILTER_BANDS = 16
FILTER_HIDDEN = 64
DECAY_TARGET = 1e-2
MAX_DECAY = math.log(DECAY_TARGET) / 0.3
MIN_DECAY = math.log(DECAY_TARGET) / 1.5
FILTER_SHIFT = 0.05

N1 = 256
N2 = 32
K1P = 136
BD = 256

VMEM_LIMIT = 56 * 1024 * 1024
HIGHEST = lax.Precision.HIGHEST
NEG = -1e30


def _cp(sem):
    return pltpu.CompilerParams(dimension_semantics=sem, vmem_limit_bytes=VMEM_LIMIT)


def _const_spec(shape):
    nd = len(shape)
    return pl.BlockSpec(shape, lambda *_: (0,) * nd, pipeline_mode=pl.Buffered(1))


def _silu(x):
    return x * jax.nn.sigmoid(x)


def _rms(x, g):
    return x * lax.rsqrt(jnp.mean(x * x, axis=-1, keepdims=True) + EPS) * g


def _mods_kernel(c_ref, w_ref, b_ref, o_ref):
    cs = _silu(c_ref[...])
    o_ref[...] = jnp.dot(cs, w_ref[...], preferred_element_type=F32, precision=HIGHEST) + b_ref[...]


def _mods(c8, w_ada, b_ada):
    n = w_ada.shape[1]
    tn = 1024
    return pl.pallas_call(
        _mods_kernel,
        out_shape=jax.ShapeDtypeStruct((8, n), F32),
        grid=(n // tn,),
        in_specs=[pl.BlockSpec((8, D_MODEL), lambda j: (0, 0)),
                  pl.BlockSpec((D_MODEL, tn), lambda j: (0, j)),
                  pl.BlockSpec((1, tn), lambda j: (0, j))],
        out_specs=pl.BlockSpec((8, tn), lambda j: (0, j)),
        compiler_params=_cp(("parallel",)),
        name="mods",
    )(c8, w_ada, b_ada.reshape(1, n))


def _pos_kernel(om_ref, o_ref):
    g = pl.program_id(0)
    om = om_ref[...]
    nf = om.shape[1]
    r = g.astype(F32)
    c = lax.broadcasted_iota(jnp.int32, (GRID_W, nf), 0).astype(F32)
    ar = jnp.broadcast_to(r * om, (GRID_W, nf))
    ac = c * om
    o_ref[:, 0 * nf:1 * nf] = jnp.sin(ar)
    o_ref[:, 1 * nf:2 * nf] = jnp.cos(ar)
    o_ref[:, 2 * nf:3 * nf] = jnp.sin(ac)
    o_ref[:, 3 * nf:4 * nf] = jnp.cos(ac)


def _pos_table(L):
    nf = D_MODEL // 4
    omega = 1.0 / (10000.0 ** (jnp.arange(nf, dtype=F32) / nf))
    return pl.pallas_call(
        _pos_kernel,
        out_shape=jax.ShapeDtypeStruct((L, D_MODEL), F32),
        grid=(L // GRID_W,),
        in_specs=[pl.BlockSpec((1, nf), lambda g: (0, 0))],
        out_specs=pl.BlockSpec((GRID_W, D_MODEL), lambda g: (g, 0)),
        compiler_params=_cp(("parallel",)),
        name="pos",
    )(omega.reshape(1, nf))


FF_CHUNK = D_FF // 2


def _ffn_kernel(*refs, sub, has_pos, final):
    it = iter(refs)
    x_ref = next(it)
    pos_ref = next(it) if has_pos else None
    mod_ref = next(it)
    g_ref = next(it)
    up_ref = next(it)
    dn_ref = next(it)
    fg_ref = next(it) if final else None
    o_ref = next(it)

    x = x_ref[0]
    if has_pos:
        x = x + pos_ref[...]
    shift = mod_ref[0, 3 * sub:3 * sub + 1, :]
    scale = mod_ref[0, 3 * sub + 1:3 * sub + 2, :]
    gate = mod_ref[0, 3 * sub + 2:3 * sub + 3, :]
    h = _rms(x, g_ref[sub:sub + 1, :]) * (1.0 + scale) + shift
    hb = h.astype(BF16)
    acc = jnp.zeros(x.shape, F32)
    for j in range(D_FF // FF_CHUNK):
        lo = j * FF_CHUNK
        gg = jnp.dot(hb, up_ref[:, lo:lo + FF_CHUNK], preferred_element_type=F32)
        uu = jnp.dot(hb, up_ref[:, D_FF + lo:D_FF + lo + FF_CHUNK], preferred_element_type=F32)
        a = (_silu(gg) * uu).astype(BF16)
        acc = acc + jnp.dot(a, dn_ref[lo:lo + FF_CHUNK, :], preferred_element_type=F32)
    y = x + 0.5 * gate * acc
    if final:
        y = _rms(y, fg_ref[...])
    o_ref[0] = y


def _ffn(x, mods, mod_row, norm_g, up_b, dn_b, *, sub, tm, pos=None, final_g=None):
    B, L, D = x.shape
    args = [x]
    specs = [pl.BlockSpec((1, tm, D), lambda b, i: (b, i, 0))]
    if pos is not None:
        args.append(pos)
        specs.append(pl.BlockSpec((tm, D), lambda b, i: (i, 0)))
    args += [mods, norm_g, up_b, dn_b]
    specs += [pl.BlockSpec((1, N_MOD, D), lambda b, i: (mod_row(b), 0, 0)),
              _const_spec((3, D)), _const_spec((D, 2 * D_FF)), _const_spec((D_FF, D))]
    if final_g is not None:
        args.append(final_g.reshape(1, D))
        specs.append(_const_spec((1, D)))
    return pl.pallas_call(
        functools.partial(_ffn_kernel, sub=sub, has_pos=pos is not None, final=final_g is not None),
        out_shape=jax.ShapeDtypeStruct((B, L, D), F32),
        grid=(B, L // tm),
        in_specs=specs,
        out_specs=pl.BlockSpec((1, tm, D), lambda b, i: (b, i, 0)),
        compiler_params=_cp(("parallel", "parallel")),
        name=f"ffn{sub}",
    )(*args)


def _proj_kernel(x_ref, mod_ref, g_ref, w_ref, *o_refs, widths):
    x = x_ref[0]
    shift = mod_ref[0, 3:4, :]
    scale = mod_ref[0, 4:5, :]
    hb = (_rms(x, g_ref[1:2, :]) * (1.0 + scale) + shift).astype(BF16)
    lo = 0
    for o_ref, w in zip(o_refs, widths):
        o_ref[0] = jnp.dot(hb, w_ref[:, lo:lo + w], preferred_element_type=F32).astype(o_ref.dtype)
        lo += w


def _proj(x, mods, mod_row, norm_g, w_b, widths, *, tm):
    B, L, D = x.shape
    n = sum(widths)
    return pl.pallas_call(
        functools.partial(_proj_kernel, widths=widths),
        out_shape=[jax.ShapeDtypeStruct((B, L, w), BF16) for w in widths],
        grid=(B, L // tm),
        in_specs=[pl.BlockSpec((1, tm, D), lambda b, i: (b, i, 0)),
                  pl.BlockSpec((1, N_MOD, D), lambda b, i: (mod_row(b), 0, 0)),
                  _const_spec((3, D)), _const_spec((D, n))],
        out_specs=[pl.BlockSpec((1, tm, w), lambda b, i: (b, i, 0)) for w in widths],
        compiler_params=_cp(("parallel", "parallel")),
        name="proj",
    )(x, mods, norm_g, w_b)


HALO = 16


def _conv3(x, prev_row, next_row, w_ref, b_ref):
    T = x.shape[0]
    rows = lax.broadcasted_iota(jnp.int32, x.shape, 0)
    xp = jnp.where(rows == 0, prev_row, pltpu.roll(x, 1, 0))
    xn = jnp.where(rows == T - 1, next_row, pltpu.roll(x, T - 1, 0))
    return xp * w_ref[0:1, :] + x * w_ref[1:2, :] + xn * w_ref[2:3, :] + b_ref[...]


def _halo_rows(prev_ref, next_ref):
    i = pl.program_id(1)
    n = pl.num_programs(1)
    p = prev_ref[0, HALO - 1:HALO, :].astype(F32)
    q = next_ref[0, 0:1, :].astype(F32)
    p = jnp.where(i == 0, 0.0, p)
    q = jnp.where(i == n - 1, 0.0, q)
    return p, q


def _halo_specs(tm, L, C):
    r = tm // HALO
    nb = L // HALO
    return [pl.BlockSpec((1, HALO, C), lambda b, i: (b, jnp.maximum(i * r - 1, 0), 0)),
            pl.BlockSpec((1, tm, C), lambda b, i: (b, i, 0)),
            pl.BlockSpec((1, HALO, C), lambda b, i: (b, jnp.minimum((i + 1) * r, nb - 1), 0))]


def _feat_kernel(prev_ref, xm_ref, next_ref, cw_ref, cb_ref, bq_ref, bkt_ref, bv_ref, gqk_ref, gv_ref, gb_ref,
                 *o_refs, with_q):
    if with_q:
        q_ref, xc_ref, kt_ref, v_ref, g_ref = o_refs
    else:
        kt_ref, v_ref, g_ref = o_refs
    xm = xm_ref[0].astype(F32)
    p, n = _halo_rows(prev_ref, next_ref)
    xc = _silu(_conv3(xm, p, n, cw_ref, cb_ref))
    xcb = xc.astype(BF16)
    xmb = xm_ref[0]
    if with_q:
        xc_ref[0] = xcb
    g_ref[0] = (jnp.dot(xcb, gqk_ref[...], preferred_element_type=F32)
                + jnp.dot(xmb, gv_ref[...], preferred_element_type=F32) + gb_ref[...])
    for j in range(D_A // BD):
        sl = slice(j * BD, (j + 1) * BD)
        if with_q:
            q = jnp.dot(xcb[:, sl], bq_ref[j], preferred_element_type=F32)
            q_ref[0, :, sl] = (q * (DH ** -0.5)).astype(BF16)
        kt_ref[0, sl, :] = lax.dot_general(bkt_ref[j], xcb[:, sl], (((1,), (1,)), ((), ())),
                                          preferred_element_type=F32).astype(BF16)
        v_ref[0, :, sl] = jnp.dot(xmb[:, sl], bv_ref[j], preferred_element_type=F32).astype(BF16)


def _feat(xm, cw, cb, bq, bkt, bv, gqk, gv, gb, *, tm, with_q=True):
    B, L, _ = xm.shape
    nb = D_A // BD
    tok = jax.ShapeDtypeStruct((B, L, D_A), BF16)
    tok_spec = pl.BlockSpec((1, tm, D_A), lambda b, i: (b, i, 0))
    shapes = [jax.ShapeDtypeStruct((B, D_A, L), BF16), tok, jax.ShapeDtypeStruct((B, L, 16), F32)]
    specs = [pl.BlockSpec((1, D_A, tm), lambda b, i: (b, 0, i)), tok_spec,
             pl.BlockSpec((1, tm, 16), lambda b, i: (b, i, 0))]
    if with_q:
        shapes = [tok, tok] + shapes
        specs = [tok_spec, tok_spec] + specs
    return pl.pallas_call(
        functools.partial(_feat_kernel, with_q=with_q),
        out_shape=shapes,
        grid=(B, L // tm),
        in_specs=_halo_specs(tm, L, D_A) + [
            _const_spec((3, D_A)), _const_spec((1, D_A)),
            _const_spec((nb, BD, BD)), _const_spec((nb, BD, BD)), _const_spec((nb, BD, BD)),
            _const_spec((D_A, 16)), _const_spec((D_A, 16)), _const_spec((1, 16))],
        out_specs=specs,
        compiler_params=_cp(("parallel", "parallel")),
        name="feat" if with_q else "feat_ctx",
    )(xm, xm, xm, cw, cb.reshape(1, D_A), bq, bkt, bv, gqk, gv, gb.reshape(1, 16))


def _gatew_kernel(bq_ref, bk_ref, bv_ref, wg_ref, gqk_ref, gv_ref):
    j = pl.program_id(0)
    wq = wg_ref[0]
    wk = wg_ref[1]
    wv = wg_ref[2]
    gqk_ref[...] = (jnp.dot(bq_ref[0], wq, preferred_element_type=F32, precision=HIGHEST)
                    + jnp.dot(bk_ref[0], wk, preferred_element_type=F32, precision=HIGHEST))
    gv_ref[...] = jnp.dot(bv_ref[0], wv, preferred_element_type=F32, precision=HIGHEST)


def _gatew(bq, bk, bv, w_gate):
    nb = D_A // BD
    wg = w_gate.reshape(3, D_A, 16)
    blk = pl.BlockSpec((1, BD, BD), lambda j: (j, 0, 0))
    return pl.pallas_call(
        _gatew_kernel,
        out_shape=[jax.ShapeDtypeStruct((D_A, 16), F32)] * 2,
        grid=(nb,),
        in_specs=[blk, blk, blk, pl.BlockSpec((3, BD, 16), lambda j: (0, j, 0))],
        out_specs=[pl.BlockSpec((BD, 16), lambda j: (j, 0))] * 2,
        compiler_params=_cp(("parallel",)),
        name="gatew",
    )(bq, bk, bv, wg)


def _blockdiag_tiles(w):
    nb = D_A // BD
    per = BD // QKV_BLOCK
    wr = w.reshape(nb, per, QKV_BLOCK, QKV_BLOCK)
    eye = jnp.eye(per, dtype=w.dtype)
    t = wr[:, :, :, None, :] * eye[None, :, None, :, None]
    return t.reshape(nb, BD, BD)


def _gate_vectors(gt_ref, gl_ref, sign):
    T = gt_ref.shape[0]
    ig_row = gl_ref[0:1, :]
    lf_row = jax.nn.log_sigmoid(gl_ref[1:2, :])
    lf_col = jax.nn.log_sigmoid(gt_ref[:, 1:2])
    r = lax.broadcasted_iota(jnp.int32, (T, T), 0)
    c = lax.broadcasted_iota(jnp.int32, (T, T), 1)
    mask = sign * (r - c) >= 0
    mask_t = sign * (c - r) >= 0
    b_col = jnp.sum(jnp.where(mask, lf_row, 0.0), axis=1, keepdims=True)
    b_row = jnp.sum(jnp.where(mask_t, lf_col, 0.0), axis=0, keepdims=True)
    b_last = jnp.sum(lf_row, axis=1, keepdims=True)
    return ig_row, b_col, b_row, b_last, mask


def _state_step(kt, v, ig_row, b_row, b_last, m_prev):
    log_w = b_last - b_row + ig_row
    m_new = jnp.maximum(b_last + m_prev, jnp.max(log_w, axis=1, keepdims=True))
    decay = jnp.exp(b_last + m_prev - m_new)
    w = jnp.exp(log_w - m_new)
    kw = kt.astype(F32) * w
    dC = jnp.dot(kw.astype(BF16), v, preferred_element_type=F32)
    dn = jnp.sum(kw, axis=1, keepdims=True)
    return decay, m_new, dC, dn


def _ctxstate_kernel(kt_ref, v_ref, gt_ref, gl_ref, c_ref, n_ref, m_ref):
    d = pl.program_id(0)
    sign = 1 - 2 * d
    ig_row, _, b_row, b_last, _ = _gate_vectors(gt_ref.at[0, 0, 0], gl_ref.at[0, 0, 0], sign)
    m0 = jnp.zeros((1, 1), F32)
    _, m_new, dC, dn = _state_step(kt_ref[0], v_ref[0], ig_row, b_row, b_last, m0)
    c_ref[0, 0, 0] = dC
    n_ref[0, 0, 0] = jnp.broadcast_to(dn, (DH, 128))
    m_ref[0, 0, 0] = jnp.broadcast_to(m_new, (8, 128))


def _ctxstate(kt, v, gt, gl):
    B, _, Lc = kt.shape
    H = A_HEADS
    return pl.pallas_call(
        _ctxstate_kernel,
        out_shape=[jax.ShapeDtypeStruct((2, B, H, DH, DH), F32),
                   jax.ShapeDtypeStruct((2, B, H, DH, 128), F32),
                   jax.ShapeDtypeStruct((2, B, H, 8, 128), F32)],
        grid=(2, B, H),
        in_specs=[pl.BlockSpec((1, DH, Lc), lambda d, b, h: (b, h, 0)),
                  pl.BlockSpec((1, Lc, DH), lambda d, b, h: (b, 0, h)),
                  pl.BlockSpec((1, 1, 1, Lc, 2), lambda d, b, h: (d, h, b, 0, 0)),
                  pl.BlockSpec((1, 1, 1, 2, Lc), lambda d, b, h: (d, h, b, 0, 0))],
        out_specs=[pl.BlockSpec((1, 1, 1, DH, DH), lambda d, b, h: (d, b, h, 0, 0)),
                   pl.BlockSpec((1, 1, 1, DH, 128), lambda d, b, h: (d, b, h, 0, 0)),
                   pl.BlockSpec((1, 1, 1, 8, 128), lambda d, b, h: (d, b, h, 0, 0))],
        compiler_params=_cp(("parallel", "parallel", "parallel")),
        name="ctxstate",
    )(kt, v, gt, gl)


def _scan_kernel(q_ref, kt_ref, v_ref, gt_ref, gl_ref, c0_ref, n0_ref, m0_ref, h_ref,
                 c_s, cb_s, n_s, nb_s, m_s):
    d = pl.program_id(0)
    j = pl.program_id(3)
    sign = 1 - 2 * d

    @pl.when(j == 0)
    def _():
        c0 = c0_ref[0, 0, 0]
        c_s[...] = c0
        cb_s[...] = c0.astype(BF16)
        n0 = n0_ref[0, 0, 0]
        n_s[...] = n0
        nb_s[...] = n0.astype(BF16)
        m_s[...] = m0_ref[0, 0, 0]

    q = q_ref[0]
    kt = kt_ref[0]
    v = v_ref[0]
    ig_row, b_col, b_row, b_last, mask = _gate_vectors(gt_ref.at[0, 0, 0], gl_ref.at[0, 0, 0], sign)
    m_prev = m_s[0:1, 0:1]

    log_d = jnp.where(mask, b_col - b_row + ig_row, NEG)
    m_inter = b_col + m_prev
    m_t = jnp.maximum(m_inter, jnp.max(log_d, axis=1, keepdims=True))
    dmat = jnp.exp(log_d - m_t)
    inter = jnp.exp(m_inter - m_t)
    s = jnp.dot(q, kt, preferred_element_type=F32) * dmat
    num = (jnp.dot(s.astype(BF16), v, preferred_element_type=F32)
           + inter * jnp.dot(q, cb_s[...], preferred_element_type=F32))
    qn = jnp.dot(q, nb_s[...], preferred_element_type=F32)[:, 0:1]
    den = jnp.sum(s, axis=1, keepdims=True) + inter * qn
    h = num / jnp.maximum(jnp.abs(den), jnp.exp(-m_t))
    h_ref[0, 0] = h.astype(h_ref.dtype)

    decay, m_new, dC, dn = _state_step(kt, v, ig_row, b_row, b_last, m_prev)
    c_new = decay * c_s[...] + dC
    c_s[...] = c_new
    cb_s[...] = c_new.astype(BF16)
    n_new = decay * n_s[...] + dn
    n_s[...] = n_new
    nb_s[...] = n_new.astype(BF16)
    m_s[...] = jnp.broadcast_to(m_new, (8, 128))


def _scan(q, kt, v, gt, gl, c0, n0, m0):
    B, L, _ = q.shape
    H = A_HEADS
    nc = L // CHUNK

    def cj(d, j):
        return j + d * (nc - 1 - 2 * j)

    return pl.pallas_call(
        _scan_kernel,
        out_shape=jax.ShapeDtypeStruct((2, B, L, D_A), BF16),
        grid=(2, B, H, nc),
        in_specs=[pl.BlockSpec((1, CHUNK, DH), lambda d, b, h, j: (b, cj(d, j), h)),
                  pl.BlockSpec((1, DH, CHUNK), lambda d, b, h, j: (b, h, cj(d, j))),
                  pl.BlockSpec((1, CHUNK, DH), lambda d, b, h, j: (b, cj(d, j), h)),
                  pl.BlockSpec((1, 1, 1, CHUNK, 2), lambda d, b, h, j: (d, h, b, cj(d, j), 0)),
                  pl.BlockSpec((1, 1, 1, 2, CHUNK), lambda d, b, h, j: (d, h, b, 0, cj(d, j))),
                  pl.BlockSpec((1, 1, 1, DH, DH), lambda d, b, h, j: (d, b, h, 0, 0)),
                  pl.BlockSpec((1, 1, 1, DH, 128), lambda d, b, h, j: (d, b, h, 0, 0)),
                  pl.BlockSpec((1, 1, 1, 8, 128), lambda d, b, h, j: (d, b, h, 0, 0))],
        out_specs=pl.BlockSpec((1, 1, CHUNK, DH), lambda d, b, h, j: (d, b, cj(d, j), h)),
        scratch_shapes=[pltpu.VMEM((DH, DH), F32), pltpu.VMEM((DH, DH), BF16),
                        pltpu.VMEM((DH, 128), F32), pltpu.VMEM((DH, 128), BF16),
                        pltpu.VMEM((8, 128), F32)],
        compiler_params=_cp(("parallel", "parallel", "parallel", "arbitrary")),
        name="scan",
    )(q, kt, v, gt, gl, c0, n0, m0)


def _gate_layouts(g):
    B, L, _ = g.shape
    g5 = g.reshape(B, L, 2, 2, A_HEADS)
    gt = jnp.transpose(g5, (2, 4, 0, 1, 3))
    gl = jnp.transpose(g5, (2, 4, 0, 3, 1))
    return gt, gl


def _hyconv_kernel(prev_ref, hy_ref, next_ref, cw_ref, cb_ref, u_ref, x0_ref):
    hy = hy_ref[0].astype(F32)
    p, n = _halo_rows(prev_ref, next_ref)
    y = _conv3(hy, p, n, cw_ref, cb_ref)
    x0_ref[0] = y[:, 0:D_B].astype(BF16)
    u_ref[0] = (y[:, D_B:2 * D_B] * y[:, 2 * D_B:3 * D_B]).astype(BF16)


def _hyconv(hy, cw, cb, *, tm):
    B, L, C = hy.shape
    o = jax.ShapeDtypeStruct((B, L, D_B), BF16)
    return pl.pallas_call(
        _hyconv_kernel,
        out_shape=[o, o],
        grid=(B, L // tm),
        in_specs=_halo_specs(tm, L, C) + [_const_spec((3, C)), _const_spec((1, C))],
        out_specs=[pl.BlockSpec((1, tm, D_B), lambda b, i: (b, i, 0))] * 2,
        compiler_params=_cp(("parallel", "parallel")),
        name="hyconv",
    )(hy, hy, hy, cw, cb.reshape(1, C))


def _filt_kernel(w1_ref, b1_ref, w2_ref, b2_ref, w3_ref, b3_ref, w4_ref, fr_ref, o_ref, *, L):
    i = pl.program_id(0)
    T = o_ref.shape[1]
    pos = (lax.broadcasted_iota(jnp.int32, (T, 128), 0) + i * T).astype(F32)
    lane = lax.broadcasted_iota(jnp.int32, (T, 128), 1)
    t = pos / (L - 1.0)
    w = (2.0 * math.pi) * pos / L
    band = jnp.where(lane <= FILTER_BANDS, lane - 1, lane - 1 - FILTER_BANDS).astype(F32)
    f = 1e-4 + band * ((FILTER_BANDS - 1 - 1e-4) / (FILTER_BANDS - 1))
    z = jnp.where(lane == 0, t,
                  jnp.where(lane <= FILTER_BANDS, jnp.cos(f * w),
                            jnp.where(lane <= 2 * FILTER_BANDS, -jnp.sin(f * w), 0.0)))
    fr = fr_ref[...]
    a = jnp.sin(fr * (jnp.dot(z, w1_ref[...], preferred_element_type=F32, precision=HIGHEST) + b1_ref[...]))
    a = jnp.sin(fr * (jnp.dot(a, w2_ref[...], preferred_element_type=F32, precision=HIGHEST) + b2_ref[...]))
    a = jnp.sin(fr * (jnp.dot(a, w3_ref[...], preferred_element_type=F32, precision=HIGHEST) + b3_ref[...]))
    hf = jnp.dot(a, w4_ref[...], preferred_element_type=F32, precision=HIGHEST)
    ch = lax.broadcasted_iota(jnp.int32, (1, D_B), 1).astype(F32)
    deltas = jnp.abs(MIN_DECAY + ch * ((MAX_DECAY - MIN_DECAY) / (D_B - 1)))
    window = jnp.exp(-t[:, 0:1] * deltas) + FILTER_SHIFT
    o_ref[0] = (hf[:, 0:D_B] * window).astype(o_ref.dtype)
    o_ref[1] = jnp.where(pos[:, 0:1] == 0.0, 0.0, hf[:, D_B:2 * D_B] * window).astype(o_ref.dtype)


def _filters(L, w1, b1, w2, b2, w3, b3, w4, freq, *, tm):
    emb = w1.shape[0]
    w1p = jnp.zeros((128, FILTER_HIDDEN), F32).at[0:emb].set(w1)
    row = lambda a: a.reshape(1, -1)
    return pl.pallas_call(
        functools.partial(_filt_kernel, L=L),
        out_shape=jax.ShapeDtypeStruct((2, L, D_B), BF16),
        grid=(L // tm,),
        in_specs=[_const_spec((128, FILTER_HIDDEN)), _const_spec((1, FILTER_HIDDEN)),
                  _const_spec((FILTER_HIDDEN, FILTER_HIDDEN)), _const_spec((1, FILTER_HIDDEN)),
                  _const_spec((FILTER_HIDDEN, FILTER_HIDDEN)), _const_spec((1, FILTER_HIDDEN)),
                  _const_spec((FILTER_HIDDEN, 2 * D_B)), _const_spec((1, FILTER_HIDDEN))],
        out_specs=pl.BlockSpec((2, tm, D_B), lambda i: (0, i, 0)),
        compiler_params=_cp(("parallel",)),
        name="filt",
    )(w1p, row(b1), w2, row(b2), w3, row(b3), w4, row(freq))


def _dft_tables():
    N = N1 * N2
    half = N1 // 2
    k1 = np.arange(K1P)[:, None].astype(np.float64)
    n1 = np.arange(half)[None, :].astype(np.float64)
    valid = (k1 <= half).astype(np.float64)
    ang = 2.0 * np.pi * k1 * n1 / N1
    f1 = np.concatenate([np.cos(ang) * valid, -np.sin(ang) * valid], axis=0)
    f1[K1P + half] = 0.0
    f1[K1P] = 0.0
    ck = np.where((k1 == 0) | (k1 == half), 1.0, 2.0) * valid
    g3r = (np.cos(ang) * ck / N).T
    g3i = (-np.sin(ang) * ck / N).T
    n2 = np.arange(N2)[None, :].astype(np.float64)
    tw = 2.0 * np.pi * k1 * n2 / N
    twr = np.repeat((np.cos(tw) * valid)[:, :, None], 128, axis=2)
    twi = np.repeat((-np.sin(tw) * valid)[:, :, None], 128, axis=2)
    k2 = np.arange(N2)[:, None].astype(np.float64)
    a2 = 2.0 * np.pi * k2 * n2 / N2
    cr, ci = np.cos(a2), -np.sin(a2)
    s2f = np.block([[cr, -ci], [ci, cr]])
    s2b = np.block([[cr, ci], [-ci, cr]])
    f = lambda a: jnp.asarray(a, dtype=F32)
    return f(f1), f(g3r), f(g3i), f(twr), f(twi), f(s2f), f(s2b)


def _dft1_kernel(u_ref, f_ref, zr_ref, zi_ref):
    z = jnp.dot(f_ref[...], u_ref[0], preferred_element_type=F32)
    zr_ref[0] = z[0:K1P]
    zi_ref[0] = z[K1P:2 * K1P]


def _dft1(u, f1b, *, tn):
    S, L, C = u.shape
    half = N1 // 2
    W = N2 * C
    uv = u.reshape(S, half, W)
    o = jax.ShapeDtypeStruct((S, K1P, W), F32)
    return pl.pallas_call(
        _dft1_kernel,
        out_shape=[o, o],
        grid=(S, W // tn),
        in_specs=[pl.BlockSpec((1, half, tn), lambda s, j: (s, 0, j)), _const_spec((2 * K1P, half))],
        out_specs=[pl.BlockSpec((1, K1P, tn), lambda s, j: (s, 0, j))] * 2,
        compiler_params=_cp(("parallel", "parallel")),
        name="dft1",
    )(uv, f1b)


K1T = 8


def _fwd_stage2(zr, zi, tr, ti, s2f):
    xr = zr * tr - zi * ti
    xi = zr * ti + zi * tr
    x = jnp.dot(s2f, jnp.concatenate([xr, xi], axis=0).astype(BF16), preferred_element_type=F32)
    return x[0:N2], x[N2:2 * N2]


def _fspec_kernel(zr_ref, zi_ref, twr_ref, twi_ref, s2f_ref, kr_ref, ki_ref):
    s2f = s2f_ref[...]
    for kk in range(K1T):
        tr = twr_ref[kk][:, 0:1]
        ti = twi_ref[kk][:, 0:1]
        ar, ai = _fwd_stage2(zr_ref[0, kk], zi_ref[0, kk], tr, ti, s2f)
        br, bi = _fwd_stage2(zr_ref[1, kk], zi_ref[1, kk], tr, ti, s2f)
        kr_ref[kk] = ar + br
        ki_ref[kk] = ai - bi


def _fspec(zr, zi, twr, twi, s2fb):
    C = zr.shape[-1]
    o = jax.ShapeDtypeStruct((K1P, N2, C), F32)
    zspec = pl.BlockSpec((2, K1T, N2, C), lambda t: (0, t, 0, 0))
    tspec = pl.BlockSpec((K1T, N2, 128), lambda t: (t, 0, 0))
    return pl.pallas_call(
        _fspec_kernel,
        out_shape=[o, o],
        grid=(K1P // K1T,),
        in_specs=[zspec, zspec, tspec, tspec, _const_spec((2 * N2, 2 * N2))],
        out_specs=[pl.BlockSpec((K1T, N2, C), lambda t: (t, 0, 0))] * 2,
        compiler_params=_cp(("parallel",)),
        name="fspec",
    )(zr, zi, twr, twi, s2fb)


def _spec_kernel(zr_ref, zi_ref, kr_ref, ki_ref, twr_ref, twi_ref, s2f_ref, s2b_ref, vr_ref, vi_ref):
    s2f = s2f_ref[...]
    s2b = s2b_ref[...]
    for kk in range(K1T):
        tr = twr_ref[kk][:, 0:1]
        ti = twi_ref[kk][:, 0:1]
        xr, xi = _fwd_stage2(zr_ref[0, kk], zi_ref[0, kk], tr, ti, s2f)
        kr = kr_ref[kk]
        ki = ki_ref[kk]
        yr = xr * kr - xi * ki
        yi = xr * ki + xi * kr
        v = jnp.dot(s2b, jnp.concatenate([yr, yi], axis=0).astype(BF16), preferred_element_type=F32)
        vr = v[0:N2]
        vi = v[N2:2 * N2]
        vr_ref[0, kk] = vr * tr + vi * ti
        vi_ref[0, kk] = vi * tr - vr * ti


def _spec(zr, zi, kr, ki, twr, twi, s2fb, s2bb):
    B, _, _, C = zr.shape
    o = jax.ShapeDtypeStruct((B, K1P, N2, C), F32)
    zspec = pl.BlockSpec((1, K1T, N2, C), lambda t, b: (b, t, 0, 0))
    kspec = pl.BlockSpec((K1T, N2, C), lambda t, b: (t, 0, 0))
    tspec = pl.BlockSpec((K1T, N2, 128), lambda t, b: (t, 0, 0))
    return pl.pallas_call(
        _spec_kernel,
        out_shape=[o, o],
        grid=(K1P // K1T, B),
        in_specs=[zspec, zspec, kspec, kspec, tspec, tspec,
                  _const_spec((2 * N2, 2 * N2)), _const_spec((2 * N2, 2 * N2))],
        out_specs=[zspec, zspec],
        compiler_params=_cp(("parallel", "parallel")),
        name="spec",
    )(zr, zi, kr, ki, twr, twi, s2fb, s2bb)


def _dft3_kernel(vr_ref, vi_ref, gr_ref, gi_ref, u_ref, x0_ref, ds_ref, o_ref):
    y = (jnp.dot(gr_ref[...], vr_ref[0].astype(BF16), preferred_element_type=F32)
         + jnp.dot(gi_ref[...], vi_ref[0].astype(BF16), preferred_element_type=F32))
    u = u_ref[0].astype(F32)
    o_ref[0] = (x0_ref[0].astype(F32) * (y + u * ds_ref[...])).astype(o_ref.dtype)


def _dft3(vr, vi, g3rb, g3ib, u, x0, dskip, *, tn):
    B, L, C = u.shape
    half = N1 // 2
    W = N2 * C
    ds = jnp.tile(dskip.reshape(1, C), (1, tn // C))
    vspec = pl.BlockSpec((1, K1P, tn), lambda b, j: (b, 0, j))
    tspec = pl.BlockSpec((1, half, tn), lambda b, j: (b, 0, j))
    out = pl.pallas_call(
        _dft3_kernel,
        out_shape=jax.ShapeDtypeStruct((B, half, W), BF16),
        grid=(B, W // tn),
        in_specs=[vspec, vspec, _const_spec((half, K1P)), _const_spec((half, K1P)),
                  tspec, tspec, _const_spec((1, tn))],
        out_specs=tspec,
        compiler_params=_cp(("parallel", "parallel")),
        name="dft3",
    )(vr.reshape(B, K1P, W), vi.reshape(B, K1P, W), g3rb, g3ib, u.reshape(B, half, W), x0.reshape(B, half, W), ds)
    return out.reshape(B, L, C)


def _merge_kernel(hf_ref, hb_ref, xc_ref, z_ref, yb_ref, gab_ref, x_ref, mod_ref, ng_ref, sk_ref,
                  wpa_ref, wpb_ref, wo_ref, o_ref):
    h = hf_ref[0, 0].astype(F32) + hb_ref[0, 0].astype(F32)
    xc = xc_ref[0].astype(F32)
    parts = []
    for k in range(A_HEADS):
        sl = slice(k * DH, (k + 1) * DH)
        hh = h[:, sl]
        hn = hh * lax.rsqrt(jnp.mean(hh * hh, axis=-1, keepdims=True) + EPS)
        parts.append(hn * ng_ref[:, sl] + sk_ref[:, sl] * xc[:, sl])
    ya = jax.nn.sigmoid(z_ref[0].astype(F32)) * jnp.concatenate(parts, axis=1)
    gab = gab_ref[0].astype(F32)
    mix = (jax.nn.sigmoid(gab[:, 0:D_MODEL]) * jnp.dot(ya.astype(BF16), wpa_ref[...], preferred_element_type=F32)
           + jax.nn.sigmoid(gab[:, D_MODEL:]) * jnp.dot(yb_ref[0], wpb_ref[...], preferred_element_type=F32))
    out = jnp.dot(mix.astype(BF16), wo_ref[...], preferred_element_type=F32)
    o_ref[0] = x_ref[0] + mod_ref[0, 5:6, :] * out


def _merge(hfb, xc, z, yb, gab, x, mods, a_norm_g, a_skip, wpa, wpb, wo, *, tm):
    B, L, D = x.shape
    tok = lambda w: pl.BlockSpec((1, tm, w), lambda b, i: (b, i, 0))
    return pl.pallas_call(
        _merge_kernel,
        out_shape=jax.ShapeDtypeStruct((B, L, D), F32),
        grid=(B, L // tm),
        in_specs=[pl.BlockSpec((1, 1, tm, D_A), lambda b, i: (0, b, i, 0)),
                  pl.BlockSpec((1, 1, tm, D_A), lambda b, i: (1, b, i, 0)),
                  tok(D_A), tok(D_A), tok(D_B), tok(2 * D_MODEL), tok(D),
                  pl.BlockSpec((1, N_MOD, D), lambda b, i: (b, 0, 0)),
                  _const_spec((1, D_A)), _const_spec((1, D_A)),
                  _const_spec((D_A, D)), _const_spec((D_B, D)), _const_spec((D, D))],
        out_specs=tok(D),
        compiler_params=_cp(("parallel", "parallel")),
        name="merge",
    )(hfb, hfb, xc, z, yb, gab, x, mods, a_norm_g.reshape(1, D_A), a_skip.reshape(1, D_A), wpa, wpb, wo)


def kernel(x, c, ctx, c_ctx, w_ada, b_ada, norm_g, ffn1_up, ffn1_down, ffn2_up, ffn2_down, w_in, a_conv_w, a_conv_b, a_wq, a_wk, a_wv, a_w_gate, a_b_gate, a_norm_g, a_skip, b_conv_w, b_conv_b, b_filt_w1, b_filt_b1, b_filt_w2, b_filt_b2, b_filt_w3, b_filt_b3, b_filt_w4, b_filt_freq, b_skip, w_pa, w_pb, w_out, final_g):
    B, L, D = x.shape
    Lc = ctx.shape[1]
    assert w_ada.shape[0] == 1, "single-layer stack"
    assert 2 * L == N1 * N2 and D == D_MODEL
    TM = 512

    c8 = jnp.zeros((8, D), F32).at[0:B].set(c).at[B].set(c_ctx)
    mods = _mods(c8, w_ada[0], b_ada[0]).reshape(8, N_MOD, D)
    ng = norm_g[0]
    row_b = lambda b: b
    row_ctx = lambda b: B

    up1 = ffn1_up[0].astype(BF16)
    dn1 = ffn1_down[0].astype(BF16)
    up2 = ffn2_up[0].astype(BF16)
    dn2 = ffn2_down[0].astype(BF16)
    w_in_b = w_in[0].astype(BF16)

    pos = _pos_table(L)
    x1 = _ffn(x, mods, row_b, ng, up1, dn1, sub=0, tm=TM, pos=pos)
    ctx1 = _ffn(ctx, mods, row_ctx, ng, up1, dn1, sub=0, tm=Lc)

    widths = (D_A, D_A, 3 * D_B, 2 * D_MODEL)
    xm, z, hy, gab = _proj(x1, mods, row_b, ng, w_in_b, widths, tm=256)
    (xm_c,) = _proj(ctx1, mods, row_ctx, ng, w_in_b[:, 0:D_A], (D_A,), tm=Lc)

    bq = _blockdiag_tiles(a_wq[0])
    bk = _blockdiag_tiles(a_wk[0])
    bv = _blockdiag_tiles(a_wv[0])
    gqk, gv = _gatew(bq, bk, bv, a_w_gate[0])
    fw = (a_conv_w[0], a_conv_b[0], bq.astype(BF16), jnp.swapaxes(bk, 1, 2).astype(BF16), bv.astype(BF16),
          gqk.astype(BF16), gv.astype(BF16), a_b_gate[0])
    q, xc, kt, v, g = _feat(xm, *fw, tm=TM)
    kt_c, v_c, g_c = _feat(xm_c, *fw, tm=Lc, with_q=False)

    c0, n0, m0 = _ctxstate(kt_c, v_c, *_gate_layouts(g_c))
    hfb = _scan(q, kt, v, *_gate_layouts(g), c0, n0, m0)

    u, x0c = _hyconv(hy, b_conv_w[0], b_conv_b[0], tm=TM)
    f1, g3r, g3i, twr, twi, s2f, s2b = _dft_tables()
    f1b, g3rb, g3ib, s2fb, s2bb = (a.astype(BF16) for a in (f1, g3r, g3i, s2f, s2b))
    hfilt = _filters(L, b_filt_w1[0], b_filt_b1[0], b_filt_w2[0], b_filt_b2[0], b_filt_w3[0], b_filt_b3[0],
                     b_filt_w4[0], b_filt_freq[0], tm=TM)
    fzr, fzi = _dft1(hfilt, f1b, tn=4096)
    kr, ki = _fspec(fzr.reshape(2, K1P, N2, D_B), fzi.reshape(2, K1P, N2, D_B), twr, twi, s2fb)
    zr, zi = _dft1(u, f1b, tn=4096)
    vr, vi = _spec(zr.reshape(B, K1P, N2, D_B), zi.reshape(B, K1P, N2, D_B), kr, ki, twr, twi, s2fb, s2bb)
    yb = _dft3(vr, vi, g3rb, g3ib, u, x0c, b_skip[0], tn=4096)

    x2 = _merge(hfb, xc, z, yb, gab, x1, mods, a_norm_g[0], a_skip[0],
                w_pa[0].astype(BF16), w_pb[0].astype(BF16), w_out[0].astype(BF16), tm=TM)

    return _ffn(x2, mods, row_b, ng, up2, dn2, sub=2, tm=TM, final_g=final_g)
```

```python
import functools
import math

import numpy as np
import jax
import jax.numpy as jnp
from jax import lax
from jax.experimental import pallas as pl
from jax.experimental.pallas import tpu as pltpu

F32 = jnp.float32
BF16 = jnp.bfloat16

D_MODEL = 1024
D_A = 2048
A_HEADS = 4
DH = D_A // A_HEADS
QKV_BLOCK = 4
D_B = 1024
D_FF = 2816
EPS = 1e-6
N_MOD = 9
GRID_W = 64
CHUNK = 256
FILTER_BANDS = 16
FILTER_HIDDEN = 64
DECAY_TARGET = 1e-2
MAX_DECAY = math.log(DECAY_TARGET) / 0.3
MIN_DECAY = math.log(DECAY_TARGET) / 1.5
FILTER_SHIFT = 0.05

N1 = 256
N2 = 32
K1P = 136
BD = 256

VMEM_LIMIT = 56 * 1024 * 1024
HIGHEST = lax.Precision.HIGHEST
NEG = -1e30


def _cp(sem):
    return pltpu.CompilerParams(dimension_semantics=sem, vmem_limit_bytes=VMEM_LIMIT)


def _const_spec(shape):
    nd = len(shape)
    return pl.BlockSpec(shape, lambda *_: (0,) * nd, pipeline_mode=pl.Buffered(1))


def _silu(x):
    return x * jax.nn.sigmoid(x)


def _rms(x, g):
    return x * lax.rsqrt(jnp.mean(x * x, axis=-1, keepdims=True) + EPS) * g


def _mods_kernel(c_ref, w_ref, b_ref, o_ref):
    cs = _silu(c_ref[...])
    o_ref[...] = jnp.dot(cs, w_ref[...], preferred_element_type=F32, precision=HIGHEST) + b_ref[...]


def _mods(c8, w_ada, b_ada):
    n = w_ada.shape[1]
    tn = 1024
    return pl.pallas_call(
        _mods_kernel,
        out_shape=jax.ShapeDtypeStruct((8, n), F32),
        grid=(n // tn,),
        in_specs=[pl.BlockSpec((8, D_MODEL), lambda j: (0, 0)),
                  pl.BlockSpec((D_MODEL, tn), lambda j: (0, j)),
                  pl.BlockSpec((1, tn), lambda j: (0, j))],
        out_specs=pl.BlockSpec((8, tn), lambda j: (0, j)),
        compiler_params=_cp(("parallel",)),
        name="mods",
    )(c8, w_ada, b_ada.reshape(1, n))


def _pos_kernel(om_ref, o_ref):
    g = pl.program_id(0)
    om = om_ref[...]
    nf = om.shape[1]
    r = g.astype(F32)
    c = lax.broadcasted_iota(jnp.int32, (GRID_W, nf), 0).astype(F32)
    ar = jnp.broadcast_to(r * om, (GRID_W, nf))
    ac = c * om
    o_ref[:, 0 * nf:1 * nf] = jnp.sin(ar)
    o_ref[:, 1 * nf:2 * nf] = jnp.cos(ar)
    o_ref[:, 2 * nf:3 * nf] = jnp.sin(ac)
    o_ref[:, 3 * nf:4 * nf] = jnp.cos(ac)


def _pos_table(L):
    nf = D_MODEL // 4
    omega = 1.0 / (10000.0 ** (jnp.arange(nf, dtype=F32) / nf))
    return pl.pallas_call(
        _pos_kernel,
        out_shape=jax.ShapeDtypeStruct((L, D_MODEL), F32),
        grid=(L // GRID_W,),
        in_specs=[pl.BlockSpec((1, nf), lambda g: (0, 0))],
        out_specs=pl.BlockSpec((GRID_W, D_MODEL), lambda g: (g, 0)),
        compiler_params=_cp(("parallel",)),
        name="pos",
    )(omega.reshape(1, nf))


FF_CHUNK = D_FF // 2


def _ffn_kernel(*refs, sub, has_pos, final):
    it = iter(refs)
    x_ref = next(it)
    pos_ref = next(it) if has_pos else None
    mod_ref = next(it)
    g_ref = next(it)
    up_ref = next(it)
    dn_ref = next(it)
    fg_ref = next(it) if final else None
    o_ref = next(it)

    x = x_ref[0]
    if has_pos:
        x = x + pos_ref[...]
    shift = mod_ref[0, 3 * sub:3 * sub + 1, :]
    scale = mod_ref[0, 3 * sub + 1:3 * sub + 2, :]
    gate = mod_ref[0, 3 * sub + 2:3 * sub + 3, :]
    h = _rms(x, g_ref[sub:sub + 1, :]) * (1.0 + scale) + shift
    hb = h.astype(BF16)
    acc = jnp.zeros(x.shape, F32)
    for j in range(D_FF // FF_CHUNK):
        lo = j * FF_CHUNK
        gg = jnp.dot(hb, up_ref[:, lo:lo + FF_CHUNK], preferred_element_type=F32)
        uu = jnp.dot(hb, up_ref[:, D_FF + lo:D_FF + lo + FF_CHUNK], preferred_element_type=F32)
        a = (_silu(gg) * uu).astype(BF16)
        acc = acc + jnp.dot(a, dn_ref[lo:lo + FF_CHUNK, :], preferred_element_type=F32)
    y = x + 0.5 * gate * acc
    if final:
        y = _rms(y, fg_ref[...])
    o_ref[0] = y


def _ffn(x, mods, mod_row, norm_g, up_b, dn_b, *, sub, tm, pos=None, final_g=None):
    B, L, D = x.shape
    args = [x]
    specs = [pl.BlockSpec((1, tm, D), lambda b, i: (b, i, 0))]
    if pos is not None:
        args.append(pos)
        specs.append(pl.BlockSpec((tm, D), lambda b, i: (i, 0)))
    args += [mods, norm_g, up_b, dn_b]
    specs += [pl.BlockSpec((1, N_MOD, D), lambda b, i: (mod_row(b), 0, 0)),
              _const_spec((3, D)), _const_spec((D, 2 * D_FF)), _const_spec((D_FF, D))]
    if final_g is not None:
        args.append(final_g.reshape(1, D))
        specs.append(_const_spec((1, D)))
    return pl.pallas_call(
        functools.partial(_ffn_kernel, sub=sub, has_pos=pos is not None, final=final_g is not None),
        out_shape=jax.ShapeDtypeStruct((B, L, D), F32),
        grid=(B, L // tm),
        in_specs=specs,
        out_specs=pl.BlockSpec((1, tm, D), lambda b, i: (b, i, 0)),
        compiler_params=_cp(("parallel", "parallel")),
        name=f"ffn{sub}",
    )(*args)


def _proj_kernel(x_ref, mod_ref, g_ref, w_ref, *o_refs, widths):
    x = x_ref[0]
    shift = mod_ref[0, 3:4, :]
    scale = mod_ref[0, 4:5, :]
    hb = (_rms(x, g_ref[1:2, :]) * (1.0 + scale) + shift).astype(BF16)
    lo = 0
    for o_ref, w in zip(o_refs, widths):
        o_ref[0] = jnp.dot(hb, w_ref[:, lo:lo + w], preferred_element_type=F32).astype(o_ref.dtype)
        lo += w


def _proj(x, mods, mod_row, norm_g, w_b, widths, *, tm):
    B, L, D = x.shape
    n = sum(widths)
    return pl.pallas_call(
        functools.partial(_proj_kernel, widths=widths),
        out_shape=[jax.ShapeDtypeStruct((B, L, w), BF16) for w in widths],
        grid=(B, L // tm),
        in_specs=[pl.BlockSpec((1, tm, D), lambda b, i: (b, i, 0)),
                  pl.BlockSpec((1, N_MOD, D), lambda b, i: (mod_row(b), 0, 0)),
                  _const_spec((3, D)), _const_spec((D, n))],
        out_specs=[pl.BlockSpec((1, tm, w), lambda b, i: (b, i, 0)) for w in widths],
        compiler_params=_cp(("parallel", "parallel")),
        name="proj",
    )(x, mods, norm_g, w_b)


HALO = 16


def _conv3(x, prev_row, next_row, w_ref, b_ref):
    T = x.shape[0]
    rows = lax.broadcasted_iota(jnp.int32, x.shape, 0)
    xp = jnp.where(rows == 0, prev_row, pltpu.roll(x, 1, 0))
    xn = jnp.where(rows == T - 1, next_row, pltpu.roll(x, T - 1, 0))
    return xp * w_ref[0:1, :] + x * w_ref[1:2, :] + xn * w_ref[2:3, :] + b_ref[...]


def _halo_rows(prev_ref, next_ref):
    i = pl.program_id(1)
    n = pl.num_programs(1)
    p = prev_ref[0, HALO - 1:HALO, :].astype(F32)
    q = next_ref[0, 0:1, :].astype(F32)
    p = jnp.where(i == 0, 0.0, p)
    q = jnp.where(i == n - 1, 0.0, q)
    return p, q


def _halo_specs(tm, L, C):
    r = tm // HALO
    nb = L // HALO
    return [pl.BlockSpec((1, HALO, C), lambda b, i: (b, jnp.maximum(i * r - 1, 0), 0)),
            pl.BlockSpec((1, tm, C), lambda b, i: (b, i, 0)),
            pl.BlockSpec((1, HALO, C), lambda b, i: (b, jnp.minimum((i + 1) * r, nb - 1), 0))]


def _feat_kernel(prev_ref, xm_ref, next_ref, cw_ref, cb_ref, bq_ref, bkt_ref, bv_ref, gqk_ref, gv_ref, gb_ref,
                 *o_refs, with_q):
    if with_q:
        q_ref, xc_ref, kt_ref, v_ref, g_ref, gl_ref = o_refs
    else:
        kt_ref, v_ref, g_ref, gl_ref = o_refs
    xm = xm_ref[0].astype(F32)
    p, n = _halo_rows(prev_ref, next_ref)
    xc = _silu(_conv3(xm, p, n, cw_ref, cb_ref))
    xcb = xc.astype(BF16)
    xmb = xm_ref[0]
    if with_q:
        xc_ref[0] = xcb
    g = (jnp.dot(xcb, gqk_ref[...], preferred_element_type=F32)
         + jnp.dot(xmb, gv_ref[...], preferred_element_type=F32) + gb_ref[...])
    g_ref[0] = g
    gl_ref[0] = g.T
    for j in range(D_A // BD):
        sl = slice(j * BD, (j + 1) * BD)
        if with_q:
            q = jnp.dot(xcb[:, sl], bq_ref[j], preferred_element_type=F32)
            q_ref[0, :, sl] = (q * (DH ** -0.5)).astype(BF16)
        kt_ref[0, sl, :] = lax.dot_general(bkt_ref[j], xcb[:, sl], (((1,), (1,)), ((), ())),
                                          preferred_element_type=F32).astype(BF16)
        v_ref[0, :, sl] = jnp.dot(xmb[:, sl], bv_ref[j], preferred_element_type=F32).astype(BF16)


def _feat(xm, cw, cb, bq, bkt, bv, gqk, gv, gb, *, tm, with_q=True):
    B, L, _ = xm.shape
    nb = D_A // BD
    tok = jax.ShapeDtypeStruct((B, L, D_A), BF16)
    tok_spec = pl.BlockSpec((1, tm, D_A), lambda b, i: (b, i, 0))
    shapes = [jax.ShapeDtypeStruct((B, D_A, L), BF16), tok, jax.ShapeDtypeStruct((B, L, 16), F32),
              jax.ShapeDtypeStruct((B, 16, L), F32)]
    specs = [pl.BlockSpec((1, D_A, tm), lambda b, i: (b, 0, i)), tok_spec,
             pl.BlockSpec((1, tm, 16), lambda b, i: (b, i, 0)), pl.BlockSpec((1, 16, tm), lambda b, i: (b, 0, i))]
    if with_q:
        shapes = [tok, tok] + shapes
        specs = [tok_spec, tok_spec] + specs
    return pl.pallas_call(
        functools.partial(_feat_kernel, with_q=with_q),
        out_shape=shapes,
        grid=(B, L // tm),
        in_specs=_halo_specs(tm, L, D_A) + [
            _const_spec((3, D_A)), _const_spec((1, D_A)),
            _const_spec((nb, BD, BD)), _const_spec((nb, BD, BD)), _const_spec((nb, BD, BD)),
            _const_spec((D_A, 16)), _const_spec((D_A, 16)), _const_spec((1, 16))],
        out_specs=specs,
        compiler_params=_cp(("parallel", "parallel")),
        name="feat" if with_q else "feat_ctx",
    )(xm, xm, xm, cw, cb.reshape(1, D_A), bq, bkt, bv, gqk, gv, gb.reshape(1, 16))


def _gatew_kernel(bq_ref, bk_ref, bv_ref, wg_ref, gqk_ref, gv_ref):
    j = pl.program_id(0)
    wq = wg_ref[0]
    wk = wg_ref[1]
    wv = wg_ref[2]
    gqk_ref[...] = (jnp.dot(bq_ref[0], wq, preferred_element_type=F32, precision=HIGHEST)
                    + jnp.dot(bk_ref[0], wk, preferred_element_type=F32, precision=HIGHEST))
    gv_ref[...] = jnp.dot(bv_ref[0], wv, preferred_element_type=F32, precision=HIGHEST)


def _gatew(bq, bk, bv, w_gate):
    nb = D_A // BD
    wg = w_gate.reshape(3, D_A, 16)
    blk = pl.BlockSpec((1, BD, BD), lambda j: (j, 0, 0))
    return pl.pallas_call(
        _gatew_kernel,
        out_shape=[jax.ShapeDtypeStruct((D_A, 16), F32)] * 2,
        grid=(nb,),
        in_specs=[blk, blk, blk, pl.BlockSpec((3, BD, 16), lambda j: (0, j, 0))],
        out_specs=[pl.BlockSpec((BD, 16), lambda j: (j, 0))] * 2,
        compiler_params=_cp(("parallel",)),
        name="gatew",
    )(bq, bk, bv, wg)


def _blockdiag_tiles(w):
    nb = D_A // BD
    per = BD // QKV_BLOCK
    wr = w.reshape(nb, per, QKV_BLOCK, QKV_BLOCK)
    eye = jnp.eye(per, dtype=w.dtype)
    t = wr[:, :, :, None, :] * eye[None, :, None, :, None]
    return t.reshape(nb, BD, BD)


def _gate_vectors(gt, gl, idx, sign):
    T = gt.shape[0]
    sub = lax.broadcasted_iota(jnp.int32, gl.shape, 0)
    lane = lax.broadcasted_iota(jnp.int32, gt.shape, 1)
    ig_row = jnp.sum(jnp.where(sub == idx, gl, 0.0), axis=0, keepdims=True)
    fg_row = jnp.sum(jnp.where(sub == idx + A_HEADS, gl, 0.0), axis=0, keepdims=True)
    fg_col = jnp.sum(jnp.where(lane == idx + A_HEADS, gt, 0.0), axis=1, keepdims=True)
    lf_row = jax.nn.log_sigmoid(fg_row)
    lf_col = jax.nn.log_sigmoid(fg_col)
    r = lax.broadcasted_iota(jnp.int32, (T, T), 0)
    c = lax.broadcasted_iota(jnp.int32, (T, T), 1)
    mask = sign * (r - c) >= 0
    mask_t = sign * (c - r) >= 0
    b_col = jnp.sum(jnp.where(mask, lf_row, 0.0), axis=1, keepdims=True)
    b_row = jnp.sum(jnp.where(mask_t, lf_col, 0.0), axis=0, keepdims=True)
    b_last = jnp.sum(lf_row, axis=1, keepdims=True)
    return ig_row, b_col, b_row, b_last, mask


def _state_step(kt, v, ig_row, b_row, b_last, m_prev):
    log_w = b_last - b_row + ig_row
    m_new = jnp.maximum(b_last + m_prev, jnp.max(log_w, axis=1, keepdims=True))
    decay = jnp.exp(b_last + m_prev - m_new)
    w = jnp.exp(log_w - m_new)
    kw = kt.astype(F32) * w
    dC = jnp.dot(kw.astype(BF16), v, preferred_element_type=F32)
    dn = jnp.sum(kw, axis=1, keepdims=True)
    return decay, m_new, dC, dn


def _ctxstate_kernel(kt_ref, v_ref, gt_ref, gl_ref, c_ref, n_ref, m_ref):
    d = pl.program_id(0)
    h = pl.program_id(2)
    sign = 1 - 2 * d
    ig_row, _, b_row, b_last, _ = _gate_vectors(gt_ref[0], gl_ref[0], 8 * d + h, sign)
    m0 = jnp.zeros((1, 1), F32)
    _, m_new, dC, dn = _state_step(kt_ref[0], v_ref[0], ig_row, b_row, b_last, m0)
    c_ref[0, 0, 0] = dC
    n_ref[0, 0, 0] = jnp.broadcast_to(dn, (DH, 128))
    m_ref[0, 0, 0] = jnp.broadcast_to(m_new, (8, 128))


def _ctxstate(kt, v, gt, gl):
    B, _, Lc = kt.shape
    H = A_HEADS
    return pl.pallas_call(
        _ctxstate_kernel,
        out_shape=[jax.ShapeDtypeStruct((2, B, H, DH, DH), F32),
                   jax.ShapeDtypeStruct((2, B, H, DH, 128), F32),
                   jax.ShapeDtypeStruct((2, B, H, 8, 128), F32)],
        grid=(2, B, H),
        in_specs=[pl.BlockSpec((1, DH, Lc), lambda d, b, h: (b, h, 0)),
                  pl.BlockSpec((1, Lc, DH), lambda d, b, h: (b, 0, h)),
                  pl.BlockSpec((1, Lc, 16), lambda d, b, h: (b, 0, 0)),
                  pl.BlockSpec((1, 16, Lc), lambda d, b, h: (b, 0, 0))],
        out_specs=[pl.BlockSpec((1, 1, 1, DH, DH), lambda d, b, h: (d, b, h, 0, 0)),
                   pl.BlockSpec((1, 1, 1, DH, 128), lambda d, b, h: (d, b, h, 0, 0)),
                   pl.BlockSpec((1, 1, 1, 8, 128), lambda d, b, h: (d, b, h, 0, 0))],
        compiler_params=_cp(("parallel", "parallel", "parallel")),
        name="ctxstate",
    )(kt, v, gt, gl)


def _scan_kernel(qf_ref, ktf_ref, vf_ref, gtf_ref, glf_ref, qb_ref, ktb_ref, vb_ref, gtb_ref, glb_ref,
                 c0_ref, n0_ref, m0_ref, hf_ref, hb_ref,
                 cf_s, cbf_s, nf_s, nbf_s, mf_s, cb_s, cbb_s, nb_s, nbb_s, mb_s):
    head = pl.program_id(1)
    j = pl.program_id(2)
    dirs = ((0, qf_ref, ktf_ref, vf_ref, gtf_ref, glf_ref, hf_ref, cf_s, cbf_s, nf_s, nbf_s, mf_s),
            (1, qb_ref, ktb_ref, vb_ref, gtb_ref, glb_ref, hb_ref, cb_s, cbb_s, nb_s, nbb_s, mb_s))

    @pl.when(j == 0)
    def _():
        for d, _, _, _, _, _, _, c_s, cq_s, n_s, nq_s, m_s in dirs:
            c0 = c0_ref[d, 0, 0]
            c_s[...] = c0
            cq_s[...] = c0.astype(BF16)
            n0 = n0_ref[d, 0, 0]
            n_s[...] = n0
            nq_s[...] = n0.astype(BF16)
            m_s[...] = m0_ref[d, 0, 0]

    for d, q_ref, kt_ref, v_ref, gt_ref, gl_ref, h_ref, c_s, cq_s, n_s, nq_s, m_s in dirs:
        sign = 1 - 2 * d
        q = q_ref[0]
        kt = kt_ref[0]
        v = v_ref[0]
        ig_row, b_col, b_row, b_last, mask = _gate_vectors(gt_ref[0], gl_ref[0], 8 * d + head, sign)
        m_prev = m_s[0:1, 0:1]

        log_d = jnp.where(mask, b_col - b_row + ig_row, NEG)
        m_inter = b_col + m_prev
        m_t = jnp.maximum(m_inter, jnp.max(log_d, axis=1, keepdims=True))
        dmat = jnp.exp(log_d - m_t)
        inter = jnp.exp(m_inter - m_t)
        s = jnp.dot(q, kt, preferred_element_type=F32) * dmat
        num = (jnp.dot(s.astype(BF16), v, preferred_element_type=F32)
               + inter * jnp.dot(q, cq_s[...], preferred_element_type=F32))
        qn = jnp.dot(q, nq_s[...], preferred_element_type=F32)[:, 0:1]
        den = jnp.sum(s, axis=1, keepdims=True) + inter * qn
        hh = num / jnp.maximum(jnp.abs(den), jnp.exp(-m_t))
        h_ref[0] = hh.astype(h_ref.dtype)

        decay, m_new, dC, dn = _state_step(kt, v, ig_row, b_row, b_last, m_prev)
        c_new = decay * c_s[...] + dC
        c_s[...] = c_new
        cq_s[...] = c_new.astype(BF16)
        n_new = decay * n_s[...] + dn
        n_s[...] = n_new
        nq_s[...] = n_new.astype(BF16)
        m_s[...] = jnp.broadcast_to(m_new, (8, 128))


def _scan(q, kt, v, gt, gl, c0, n0, m0):
    B, L, _ = q.shape
    H = A_HEADS
    nc = L // CHUNK
    fwd = lambda j: j
    bwd = lambda j: nc - 1 - j

    def specs(cj):
        return [pl.BlockSpec((1, CHUNK, DH), lambda b, h, j: (b, cj(j), h)),
                pl.BlockSpec((1, DH, CHUNK), lambda b, h, j: (b, h, cj(j))),
                pl.BlockSpec((1, CHUNK, DH), lambda b, h, j: (b, cj(j), h)),
                pl.BlockSpec((1, CHUNK, 16), lambda b, h, j: (b, cj(j), 0)),
                pl.BlockSpec((1, 16, CHUNK), lambda b, h, j: (b, 0, cj(j)))]

    state = [pltpu.VMEM((DH, DH), F32), pltpu.VMEM((DH, DH), BF16),
             pltpu.VMEM((DH, 128), F32), pltpu.VMEM((DH, 128), BF16), pltpu.VMEM((8, 128), F32)]
    out = jax.ShapeDtypeStruct((B, L, D_A), BF16)
    return pl.pallas_call(
        _scan_kernel,
        out_shape=[out, out],
        grid=(B, H, nc),
        in_specs=specs(fwd) + specs(bwd) + [
            pl.BlockSpec((2, 1, 1, DH, DH), lambda b, h, j: (0, b, h, 0, 0)),
            pl.BlockSpec((2, 1, 1, DH, 128), lambda b, h, j: (0, b, h, 0, 0)),
            pl.BlockSpec((2, 1, 1, 8, 128), lambda b, h, j: (0, b, h, 0, 0))],
        out_specs=[pl.BlockSpec((1, CHUNK, DH), lambda b, h, j: (b, fwd(j), h)),
                   pl.BlockSpec((1, CHUNK, DH), lambda b, h, j: (b, bwd(j), h))],
        scratch_shapes=state + state,
        compiler_params=_cp(("parallel", "parallel", "arbitrary")),
        name="scan",
    )(q, kt, v, gt, gl, q, kt, v, gt, gl, c0, n0, m0)


def _hyconv_kernel(prev_ref, hy_ref, next_ref, cw_ref, cb_ref, u_ref, x0_ref):
    hy = hy_ref[0].astype(F32)
    p, n = _halo_rows(prev_ref, next_ref)
    y = _conv3(hy, p, n, cw_ref, cb_ref)
    x0_ref[0] = y[:, 0:D_B].astype(BF16)
    u_ref[0] = (y[:, D_B:2 * D_B] * y[:, 2 * D_B:3 * D_B]).astype(BF16)


def _hyconv(hy, cw, cb, *, tm):
    B, L, C = hy.shape
    o = jax.ShapeDtypeStruct((B, L, D_B), BF16)
    return pl.pallas_call(
        _hyconv_kernel,
        out_shape=[o, o],
        grid=(B, L // tm),
        in_specs=_halo_specs(tm, L, C) + [_const_spec((3, C)), _const_spec((1, C))],
        out_specs=[pl.BlockSpec((1, tm, D_B), lambda b, i: (b, i, 0))] * 2,
        compiler_params=_cp(("parallel", "parallel")),
        name="hyconv",
    )(hy, hy, hy, cw, cb.reshape(1, C))


def _filt_kernel(w1_ref, b1_ref, w2_ref, b2_ref, w3_ref, b3_ref, w4_ref, fr_ref, o_ref, *, L):
    i = pl.program_id(0)
    T = o_ref.shape[1]
    pos = (lax.broadcasted_iota(jnp.int32, (T, 128), 0) + i * T).astype(F32)
    lane = lax.broadcasted_iota(jnp.int32, (T, 128), 1)
    t = pos / (L - 1.0)
    w = (2.0 * math.pi) * pos / L
    band = jnp.where(lane <= FILTER_BANDS, lane - 1, lane - 1 - FILTER_BANDS).astype(F32)
    f = 1e-4 + band * ((FILTER_BANDS - 1 - 1e-4) / (FILTER_BANDS - 1))
    z = jnp.where(lane == 0, t,
                  jnp.where(lane <= FILTER_BANDS, jnp.cos(f * w),
                            jnp.where(lane <= 2 * FILTER_BANDS, -jnp.sin(f * w), 0.0)))
    fr = fr_ref[...]
    a = jnp.sin(fr * (jnp.dot(z, w1_ref[...], preferred_element_type=F32, precision=HIGHEST) + b1_ref[...]))
    a = jnp.sin(fr * (jnp.dot(a, w2_ref[...], preferred_element_type=F32, precision=HIGHEST) + b2_ref[...]))
    a = jnp.sin(fr * (jnp.dot(a, w3_ref[...], preferred_element_type=F32, precision=HIGHEST) + b3_ref[...]))
    hf = jnp.dot(a, w4_ref[...], preferred_element_type=F32, precision=HIGHEST)
    ch = lax.broadcasted_iota(jnp.int32, (1, D_B), 1).astype(F32)
    deltas = jnp.abs(MIN_DECAY + ch * ((MAX_DECAY - MIN_DECAY) / (D_B - 1)))
    window = jnp.exp(-t[:, 0:1] * deltas) + FILTER_SHIFT
    o_ref[0] = (hf[:, 0:D_B] * window).astype(o_ref.dtype)
    o_ref[1] = jnp.where(pos[:, 0:1] == 0.0, 0.0, hf[:, D_B:2 * D_B] * window).astype(o_ref.dtype)


def _filters(L, w1, b1, w2, b2, w3, b3, w4, freq, *, tm):
    emb = w1.shape[0]
    w1p = jnp.zeros((128, FILTER_HIDDEN), F32).at[0:emb].set(w1)
    row = lambda a: a.reshape(1, -1)
    return pl.pallas_call(
        functools.partial(_filt_kernel, L=L),
        out_shape=jax.ShapeDtypeStruct((2, L, D_B), BF16),
        grid=(L // tm,),
        in_specs=[_const_spec((128, FILTER_HIDDEN)), _const_spec((1, FILTER_HIDDEN)),
                  _const_spec((FILTER_HIDDEN, FILTER_HIDDEN)), _const_spec((1, FILTER_HIDDEN)),
                  _const_spec((FILTER_HIDDEN, FILTER_HIDDEN)), _const_spec((1, FILTER_HIDDEN)),
                  _const_spec((FILTER_HIDDEN, 2 * D_B)), _const_spec((1, FILTER_HIDDEN))],
        out_specs=pl.BlockSpec((2, tm, D_B), lambda i: (0, i, 0)),
        compiler_params=_cp(("parallel",)),
        name="filt",
    )(w1p, row(b1), w2, row(b2), w3, row(b3), w4, row(freq))


def _dft_tables():
    N = N1 * N2
    half = N1 // 2
    k1 = np.arange(K1P)[:, None].astype(np.float64)
    n1 = np.arange(half)[None, :].astype(np.float64)
    valid = (k1 <= half).astype(np.float64)
    ang = 2.0 * np.pi * k1 * n1 / N1
    f1 = np.concatenate([np.cos(ang) * valid, -np.sin(ang) * valid], axis=0)
    f1[K1P + half] = 0.0
    f1[K1P] = 0.0
    ck = np.where((k1 == 0) | (k1 == half), 1.0, 2.0) * valid
    g3r = (np.cos(ang) * ck / N).T
    g3i = (-np.sin(ang) * ck / N).T
    n2 = np.arange(N2)[None, :].astype(np.float64)
    tw = 2.0 * np.pi * k1 * n2 / N
    twr = np.repeat((np.cos(tw) * valid)[:, :, None], 128, axis=2)
    twi = np.repeat((-np.sin(tw) * valid)[:, :, None], 128, axis=2)
    k2 = np.arange(N2)[:, None].astype(np.float64)
    a2 = 2.0 * np.pi * k2 * n2 / N2
    cr, ci = np.cos(a2), -np.sin(a2)
    s2f = np.block([[cr, -ci], [ci, cr]])
    s2b = np.block([[cr, ci], [-ci, cr]])
    f = lambda a: jnp.asarray(a, dtype=F32)
    return f(f1), f(g3r), f(g3i), f(twr), f(twi), f(s2f), f(s2b)


def _dft1_kernel(u_ref, f_ref, zr_ref, zi_ref):
    z = jnp.dot(f_ref[...], u_ref[0], preferred_element_type=F32)
    zr_ref[0] = z[0:K1P]
    zi_ref[0] = z[K1P:2 * K1P]


def _dft1(u, f1b, *, tn):
    S, L, C = u.shape
    half = N1 // 2
    W = N2 * C
    uv = u.reshape(S, half, W)
    o = jax.ShapeDtypeStruct((S, K1P, W), F32)
    return pl.pallas_call(
        _dft1_kernel,
        out_shape=[o, o],
        grid=(S, W // tn),
        in_specs=[pl.BlockSpec((1, half, tn), lambda s, j: (s, 0, j)), _const_spec((2 * K1P, half))],
        out_specs=[pl.BlockSpec((1, K1P, tn), lambda s, j: (s, 0, j))] * 2,
        compiler_params=_cp(("parallel", "parallel")),
        name="dft1",
    )(uv, f1b)


K1T = 8


def _fwd_stage2(zr, zi, tr, ti, s2f):
    xr = zr * tr - zi * ti
    xi = zr * ti + zi * tr
    x = jnp.dot(s2f, jnp.concatenate([xr, xi], axis=0).astype(BF16), preferred_element_type=F32)
    return x[0:N2], x[N2:2 * N2]


def _fspec_kernel(zr_ref, zi_ref, twr_ref, twi_ref, s2f_ref, kr_ref, ki_ref):
    s2f = s2f_ref[...]
    for kk in range(K1T):
        tr = twr_ref[kk][:, 0:1]
        ti = twi_ref[kk][:, 0:1]
        ar, ai = _fwd_stage2(zr_ref[0, kk], zi_ref[0, kk], tr, ti, s2f)
        br, bi = _fwd_stage2(zr_ref[1, kk], zi_ref[1, kk], tr, ti, s2f)
        kr_ref[kk] = ar + br
        ki_ref[kk] = ai - bi


def _fspec(zr, zi, twr, twi, s2fb):
    C = zr.shape[-1]
    o = jax.ShapeDtypeStruct((K1P, N2, C), F32)
    zspec = pl.BlockSpec((2, K1T, N2, C), lambda t: (0, t, 0, 0))
    tspec = pl.BlockSpec((K1T, N2, 128), lambda t: (t, 0, 0))
    return pl.pallas_call(
        _fspec_kernel,
        out_shape=[o, o],
        grid=(K1P // K1T,),
        in_specs=[zspec, zspec, tspec, tspec, _const_spec((2 * N2, 2 * N2))],
        out_specs=[pl.BlockSpec((K1T, N2, C), lambda t: (t, 0, 0))] * 2,
        compiler_params=_cp(("parallel",)),
        name="fspec",
    )(zr, zi, twr, twi, s2fb)


def _spec_kernel(zr_ref, zi_ref, kr_ref, ki_ref, twr_ref, twi_ref, s2f_ref, s2b_ref, vr_ref, vi_ref):
    s2f = s2f_ref[...]
    s2b = s2b_ref[...]
    for kk in range(K1T):
        tr = twr_ref[kk][:, 0:1]
        ti = twi_ref[kk][:, 0:1]
        xr, xi = _fwd_stage2(zr_ref[0, kk], zi_ref[0, kk], tr, ti, s2f)
        kr = kr_ref[kk]
        ki = ki_ref[kk]
        yr = xr * kr - xi * ki
        yi = xr * ki + xi * kr
        v = jnp.dot(s2b, jnp.concatenate([yr, yi], axis=0).astype(BF16), preferred_element_type=F32)
        vr = v[0:N2]
        vi = v[N2:2 * N2]
        vr_ref[0, kk] = vr * tr + vi * ti
        vi_ref[0, kk] = vi * tr - vr * ti


def _spec(zr, zi, kr, ki, twr, twi, s2fb, s2bb):
    B, _, _, C = zr.shape
    o = jax.ShapeDtypeStruct((B, K1P, N2, C), F32)
    zspec = pl.BlockSpec((1, K1T, N2, C), lambda t, b: (b, t, 0, 0))
    kspec = pl.BlockSpec((K1T, N2, C), lambda t, b: (t, 0, 0))
    tspec = pl.BlockSpec((K1T, N2, 128), lambda t, b: (t, 0, 0))
    return pl.pallas_call(
        _spec_kernel,
        out_shape=[o, o],
        grid=(K1P // K1T, B),
        in_specs=[zspec, zspec, kspec, kspec, tspec, tspec,
                  _const_spec((2 * N2, 2 * N2)), _const_spec((2 * N2, 2 * N2))],
        out_specs=[zspec, zspec],
        compiler_params=_cp(("parallel", "parallel")),
        name="spec",
    )(zr, zi, kr, ki, twr, twi, s2fb, s2bb)


def _dft3_kernel(vr_ref, vi_ref, gr_ref, gi_ref, u_ref, x0_ref, ds_ref, o_ref):
    y = (jnp.dot(gr_ref[...], vr_ref[0].astype(BF16), preferred_element_type=F32)
         + jnp.dot(gi_ref[...], vi_ref[0].astype(BF16), preferred_element_type=F32))
    u = u_ref[0].astype(F32)
    o_ref[0] = (x0_ref[0].astype(F32) * (y + u * ds_ref[...])).astype(o_ref.dtype)


def _dft3(vr, vi, g3rb, g3ib, u, x0, dskip, *, tn):
    B, L, C = u.shape
    half = N1 // 2
    W = N2 * C
    ds = jnp.tile(dskip.reshape(1, C), (1, tn // C))
    vspec = pl.BlockSpec((1, K1P, tn), lambda b, j: (b, 0, j))
    tspec = pl.BlockSpec((1, half, tn), lambda b, j: (b, 0, j))
    out = pl.pallas_call(
        _dft3_kernel,
        out_shape=jax.ShapeDtypeStruct((B, half, W), BF16),
        grid=(B, W // tn),
        in_specs=[vspec, vspec, _const_spec((half, K1P)), _const_spec((half, K1P)),
                  tspec, tspec, _const_spec((1, tn))],
        out_specs=tspec,
        compiler_params=_cp(("parallel", "parallel")),
        name="dft3",
    )(vr.reshape(B, K1P, W), vi.reshape(B, K1P, W), g3rb, g3ib, u.reshape(B, half, W), x0.reshape(B, half, W), ds)
    return out.reshape(B, L, C)


def _merge_kernel(hf_ref, hb_ref, xc_ref, z_ref, yb_ref, gab_ref, x_ref, mod_ref, ng_ref, sk_ref,
                  wpa_ref, wpb_ref, wo_ref, o_ref):
    h = hf_ref[0].astype(F32) + hb_ref[0].astype(F32)
    xc = xc_ref[0].astype(F32)
    parts = []
    for k in range(A_HEADS):
        sl = slice(k * DH, (k + 1) * DH)
        hh = h[:, sl]
        hn = hh * lax.rsqrt(jnp.mean(hh * hh, axis=-1, keepdims=True) + EPS)
        parts.append(hn * ng_ref[:, sl] + sk_ref[:, sl] * xc[:, sl])
    ya = jax.nn.sigmoid(z_ref[0].astype(F32)) * jnp.concatenate(parts, axis=1)
    gab = gab_ref[0].astype(F32)
    mix = (jax.nn.sigmoid(gab[:, 0:D_MODEL]) * jnp.dot(ya.astype(BF16), wpa_ref[...], preferred_element_type=F32)
           + jax.nn.sigmoid(gab[:, D_MODEL:]) * jnp.dot(yb_ref[0], wpb_ref[...], preferred_element_type=F32))
    out = jnp.dot(mix.astype(BF16), wo_ref[...], preferred_element_type=F32)
    o_ref[0] = x_ref[0] + mod_ref[0, 5:6, :] * out


def _merge(hf, hb, xc, z, yb, gab, x, mods, a_norm_g, a_skip, wpa, wpb, wo, *, tm):
    B, L, D = x.shape
    tok = lambda w: pl.BlockSpec((1, tm, w), lambda b, i: (b, i, 0))
    return pl.pallas_call(
        _merge_kernel,
        out_shape=jax.ShapeDtypeStruct((B, L, D), F32),
        grid=(B, L // tm),
        in_specs=[tok(D_A), tok(D_A),
                  tok(D_A), tok(D_A), tok(D_B), tok(2 * D_MODEL), tok(D),
                  pl.BlockSpec((1, N_MOD, D), lambda b, i: (b, 0, 0)),
                  _const_spec((1, D_A)), _const_spec((1, D_A)),
                  _const_spec((D_A, D)), _const_spec((D_B, D)), _const_spec((D, D))],
        out_specs=tok(D),
        compiler_params=_cp(("parallel", "parallel")),
        name="merge",
    )(hf, hb, xc, z, yb, gab, x, mods, a_norm_g.reshape(1, D_A), a_skip.reshape(1, D_A), wpa, wpb, wo)


def kernel(x, c, ctx, c_ctx, w_ada, b_ada, norm_g, ffn1_up, ffn1_down, ffn2_up, ffn2_down, w_in, a_conv_w, a_conv_b, a_wq, a_wk, a_wv, a_w_gate, a_b_gate, a_norm_g, a_skip, b_conv_w, b_conv_b, b_filt_w1, b_filt_b1, b_filt_w2, b_filt_b2, b_filt_w3, b_filt_b3, b_filt_w4, b_filt_freq, b_skip, w_pa, w_pb, w_out, final_g):
    B, L, D = x.shape
    Lc = ctx.shape[1]
    assert w_ada.shape[0] == 1, "single-layer stack"
    assert 2 * L == N1 * N2 and D == D_MODEL
    TM = 512

    c8 = jnp.zeros((8, D), F32).at[0:B].set(c).at[B].set(c_ctx)
    mods = _mods(c8, w_ada[0], b_ada[0]).reshape(8, N_MOD, D)
    ng = norm_g[0]
    row_b = lambda b: b
    row_ctx = lambda b: B

    up1 = ffn1_up[0].astype(BF16)
    dn1 = ffn1_down[0].astype(BF16)
    up2 = ffn2_up[0].astype(BF16)
    dn2 = ffn2_down[0].astype(BF16)
    w_in_b = w_in[0].astype(BF16)

    pos = _pos_table(L)
    x1 = _ffn(x, mods, row_b, ng, up1, dn1, sub=0, tm=TM, pos=pos)
    ctx1 = _ffn(ctx, mods, row_ctx, ng, up1, dn1, sub=0, tm=Lc)

    widths = (D_A, D_A, 3 * D_B, 2 * D_MODEL)
    xm, z, hy, gab = _proj(x1, mods, row_b, ng, w_in_b, widths, tm=256)
    (xm_c,) = _proj(ctx1, mods, row_ctx, ng, w_in_b, (D_A,), tm=Lc)

    bq = _blockdiag_tiles(a_wq[0])
    bk = _blockdiag_tiles(a_wk[0])
    bv = _blockdiag_tiles(a_wv[0])
    gqk, gv = _gatew(bq, bk, bv, a_w_gate[0])
    fw = (a_conv_w[0], a_conv_b[0], bq.astype(BF16), jnp.swapaxes(bk, 1, 2).astype(BF16), bv.astype(BF16),
          gqk.astype(BF16), gv.astype(BF16), a_b_gate[0])
    q, xc, kt, v, g, gl = _feat(xm, *fw, tm=TM)
    kt_c, v_c, g_c, gl_c = _feat(xm_c, *fw, tm=Lc, with_q=False)

    c0, n0, m0 = _ctxstate(kt_c, v_c, g_c, gl_c)
    hf, hb = _scan(q, kt, v, g, gl, c0, n0, m0)

    u, x0c = _hyconv(hy, b_conv_w[0], b_conv_b[0], tm=TM)
    f1, g3r, g3i, twr, twi, s2f, s2b = _dft_tables()
    f1b, g3rb, g3ib, s2fb, s2bb = (a.astype(BF16) for a in (f1, g3r, g3i, s2f, s2b))
    hfilt = _filters(L, b_filt_w1[0], b_filt_b1[0], b_filt_w2[0], b_filt_b2[0], b_filt_w3[0], b_filt_b3[0],
                     b_filt_w4[0], b_filt_freq[0], tm=TM)
    fzr, fzi = _dft1(hfilt, f1b, tn=4096)
    kr, ki = _fspec(fzr.reshape(2, K1P, N2, D_B), fzi.reshape(2, K1P, N2, D_B), twr, twi, s2fb)
    zr, zi = _dft1(u, f1b, tn=4096)
    vr, vi = _spec(zr.reshape(B, K1P, N2, D_B), zi.reshape(B, K1P, N2, D_B), kr, ki, twr, twi, s2fb, s2bb)
    yb = _dft3(vr, vi, g3rb, g3ib, u, x0c, b_skip[0], tn=4096)

    x2 = _merge(hf, hb, xc, z, yb, gab, x1, mods, a_norm_g[0], a_skip[0],
                w_pa[0].astype(BF16), w_pb[0].astype(BF16), w_out[0].astype(BF16), tm=TM)

    return _ffn(x2, mods, row_b, ng, up2, dn2, sub=2, tm=TM, final_g=final_g)
```

```python
import functools
import math

import numpy as np
import jax
import jax.numpy as jnp
from jax import lax
from jax.experimental import pallas as pl
from jax.experimental.pallas import tpu as pltpu

F32 = jnp.float32
BF16 = jnp.bfloat16

D_MODEL = 1024
D_A = 2048
A_HEADS = 4
DH = D_A // A_HEADS
QKV_BLOCK = 4
D_B = 1024
D_FF = 2816
EPS = 1e-6
N_MOD = 9
GRID_W = 64
CHUNK = 256
FILTER_BANDS = 16
FILTER_HIDDEN = 64
DECAY_TARGET = 1e-2
MAX_DECAY = math.log(DECAY_TARGET) / 0.3
MIN_DECAY = math.log(DECAY_TARGET) / 1.5
FILTER_SHIFT = 0.05

LANES = 128
SUBLANES = 8
BD = 256

N2 = SUBLANES
NFFT = 8192
N1 = NFFT // N2
KPG = BD // (2 * N2)
NG = (N1 // 2) // KPG + 1
F1R = NG * 2 * KPG
ZR = F1R * N2
CT = 256
GROUP_UNROLL = 11

VMEM_LIMIT = 56 * 1024 * 1024
HIGHEST = lax.Precision.HIGHEST
NEG = -1e30


def _cp(sem):
    return pltpu.CompilerParams(dimension_semantics=sem, vmem_limit_bytes=VMEM_LIMIT)


def _const_spec(shape):
    nd = len(shape)
    return pl.BlockSpec(shape, lambda *_: (0,) * nd, pipeline_mode=pl.Buffered(1))


def _silu(x):
    return x * jax.nn.sigmoid(x)


def _rms(x, g):
    return x * lax.rsqrt(jnp.mean(x * x, axis=-1, keepdims=True) + EPS) * g


def _mods_kernel(c_ref, w_ref, b_ref, o_ref):
    cs = _silu(c_ref[...])
    o_ref[...] = jnp.dot(cs, w_ref[...], preferred_element_type=F32, precision=HIGHEST) + b_ref[...]


def _mods(c8, w_ada, b_ada):
    n = w_ada.shape[1]
    tn = 1024
    return pl.pallas_call(
        _mods_kernel,
        out_shape=jax.ShapeDtypeStruct((8, n), F32),
        grid=(n // tn,),
        in_specs=[pl.BlockSpec((8, D_MODEL), lambda j: (0, 0)),
                  pl.BlockSpec((D_MODEL, tn), lambda j: (0, j)),
                  pl.BlockSpec((1, tn), lambda j: (0, j))],
        out_specs=pl.BlockSpec((8, tn), lambda j: (0, j)),
        compiler_params=_cp(("parallel",)),
        name="mods",
    )(c8, w_ada, b_ada.reshape(1, n))


def _postab_kernel(om_ref, o_ref):
    nf = om_ref.shape[1]
    p = lax.broadcasted_iota(jnp.int32, (GRID_W, nf), 0).astype(F32)
    a = p * om_ref[...]
    o_ref[:, 0:nf] = jnp.sin(a)
    o_ref[:, nf:2 * nf] = jnp.cos(a)


def _pos_table():
    nf = D_MODEL // 4
    omega = 1.0 / (10000.0 ** (jnp.arange(nf, dtype=F32) / nf))
    return pl.pallas_call(
        _postab_kernel,
        out_shape=jax.ShapeDtypeStruct((GRID_W, 2 * nf), F32),
        name="postab",
    )(omega.reshape(1, nf))


FF_CHUNK = D_FF // 2


def _ffn_kernel(*refs, sub, has_pos, final):
    it = iter(refs)
    x_ref = next(it)
    pos_ref = next(it) if has_pos else None
    mod_ref = next(it)
    g_ref = next(it)
    up_ref = next(it)
    dn_ref = next(it)
    fg_ref = next(it) if final else None
    o_ref = next(it)

    x = x_ref[0]
    if has_pos:
        tm = x.shape[0]
        per = tm // GRID_W
        half = pos_ref.shape[1]
        i = pl.program_id(1)
        ecol = pos_ref[...]
        rows = []
        for q in range(per):
            erow = jnp.broadcast_to(pos_ref[pl.ds(i * per + q, 1), :], (GRID_W, half))
            rows.append(jnp.concatenate([erow, ecol], axis=1))
        x = x + jnp.concatenate(rows, axis=0)
    shift = mod_ref[0, 3 * sub:3 * sub + 1, :]
    scale = mod_ref[0, 3 * sub + 1:3 * sub + 2, :]
    gate = mod_ref[0, 3 * sub + 2:3 * sub + 3, :]
    h = _rms(x, g_ref[sub:sub + 1, :]) * (1.0 + scale) + shift
    hb = h.astype(BF16)
    acc = jnp.zeros(x.shape, F32)
    for j in range(D_FF // FF_CHUNK):
        lo = j * FF_CHUNK
        gg = jnp.dot(hb, up_ref[:, lo:lo + FF_CHUNK], preferred_element_type=F32)
        uu = jnp.dot(hb, up_ref[:, D_FF + lo:D_FF + lo + FF_CHUNK], preferred_element_type=F32)
        a = (_silu(gg) * uu).astype(BF16)
        acc = acc + jnp.dot(a, dn_ref[lo:lo + FF_CHUNK, :], preferred_element_type=F32)
    y = x + 0.5 * gate * acc
    if final:
        y = _rms(y, fg_ref[...])
    o_ref[0] = y


def _ffn(x, mods, mod_row, norm_g, up_b, dn_b, *, sub, tm, pos=None, final_g=None):
    B, L, D = x.shape
    args = [x]
    specs = [pl.BlockSpec((1, tm, D), lambda b, i: (b, i, 0))]
    if pos is not None:
        args.append(pos)
        specs.append(_const_spec(pos.shape))
    args += [mods, norm_g, up_b, dn_b]
    specs += [pl.BlockSpec((1, N_MOD, D), lambda b, i: (mod_row(b), 0, 0)),
              _const_spec((3, D)), _const_spec((D, 2 * D_FF)), _const_spec((D_FF, D))]
    if final_g is not None:
        args.append(final_g.reshape(1, D))
        specs.append(_const_spec((1, D)))
    return pl.pallas_call(
        functools.partial(_ffn_kernel, sub=sub, has_pos=pos is not None, final=final_g is not None),
        out_shape=jax.ShapeDtypeStruct((B, L, D), F32),
        grid=(B, L // tm),
        in_specs=specs,
        out_specs=pl.BlockSpec((1, tm, D), lambda b, i: (b, i, 0)),
        compiler_params=_cp(("parallel", "parallel")),
        name=f"ffn{sub}",
    )(*args)


def _proj_kernel(x_ref, mod_ref, g_ref, w_ref, *o_refs, widths):
    x = x_ref[0]
    shift = mod_ref[0, 3:4, :]
    scale = mod_ref[0, 4:5, :]
    hb = (_rms(x, g_ref[1:2, :]) * (1.0 + scale) + shift).astype(BF16)
    lo = 0
    for o_ref, w in zip(o_refs, widths):
        o_ref[0] = jnp.dot(hb, w_ref[:, lo:lo + w], preferred_element_type=F32).astype(o_ref.dtype)
        lo += w


def _proj(x, mods, mod_row, norm_g, w_b, widths, *, tm):
    B, L, D = x.shape
    n = sum(widths)
    return pl.pallas_call(
        functools.partial(_proj_kernel, widths=widths),
        out_shape=[jax.ShapeDtypeStruct((B, L, w), BF16) for w in widths],
        grid=(B, L // tm),
        in_specs=[pl.BlockSpec((1, tm, D), lambda b, i: (b, i, 0)),
                  pl.BlockSpec((1, N_MOD, D), lambda b, i: (mod_row(b), 0, 0)),
                  _const_spec((3, D)), _const_spec((D, n))],
        out_specs=[pl.BlockSpec((1, tm, w), lambda b, i: (b, i, 0)) for w in widths],
        compiler_params=_cp(("parallel", "parallel")),
        name="proj",
    )(x, mods, norm_g, w_b)


HALO = 16


def _conv3(x, prev_row, next_row, w_ref, b_ref):
    T = x.shape[0]
    rows = lax.broadcasted_iota(jnp.int32, x.shape, 0)
    xp = jnp.where(rows == 0, prev_row, pltpu.roll(x, 1, 0))
    xn = jnp.where(rows == T - 1, next_row, pltpu.roll(x, T - 1, 0))
    return xp * w_ref[0:1, :] + x * w_ref[1:2, :] + xn * w_ref[2:3, :] + b_ref[...]


def _halo_rows(prev_ref, next_ref):
    i = pl.program_id(1)
    n = pl.num_programs(1)
    p = prev_ref[0, HALO - 1:HALO, :].astype(F32)
    q = next_ref[0, 0:1, :].astype(F32)
    p = jnp.where(i == 0, 0.0, p)
    q = jnp.where(i == n - 1, 0.0, q)
    return p, q


def _halo_specs(tm, L, C):
    r = tm // HALO
    nb = L // HALO
    return [pl.BlockSpec((1, HALO, C), lambda b, i: (b, jnp.maximum(i * r - 1, 0), 0)),
            pl.BlockSpec((1, tm, C), lambda b, i: (b, i, 0)),
            pl.BlockSpec((1, HALO, C), lambda b, i: (b, jnp.minimum((i + 1) * r, nb - 1), 0))]


def _qkvw_kernel(wq_ref, wk_ref, wv_ref, wg_ref, bq_ref, bkt_ref, bv_ref, gqk_ref, gv_ref):
    r = lax.broadcasted_iota(jnp.int32, (BD, BD), 0)
    c = lax.broadcasted_iota(jnp.int32, (BD, BD), 1)
    shift = QKV_BLOCK.bit_length() - 1
    same_block = (r >> shift) == (c >> shift)
    col_in_block = c & (QKV_BLOCK - 1)

    def tile(w_ref):
        w2 = w_ref[0]
        a = jnp.zeros((BD, BD), F32)
        for j in range(QKV_BLOCK):
            a = a + jnp.where(col_in_block == j, w2[:, j:j + 1], 0.0)
        return jnp.where(same_block, a, 0.0)

    tq = tile(wq_ref)
    tk = tile(wk_ref)
    tv = tile(wv_ref)
    bq_ref[0] = tq.astype(BF16)
    bkt_ref[0] = tk.T.astype(BF16)
    bv_ref[0] = tv.astype(BF16)
    gqk_ref[...] = (jnp.dot(tq, wg_ref[0], preferred_element_type=F32, precision=HIGHEST)
                    + jnp.dot(tk, wg_ref[1], preferred_element_type=F32, precision=HIGHEST)).astype(BF16)
    gv_ref[...] = jnp.dot(tv, wg_ref[2], preferred_element_type=F32, precision=HIGHEST).astype(BF16)


def _qkvw(wq, wk, wv, w_gate):
    nb = D_A // BD
    w2 = lambda w: w.reshape(nb, BD, QKV_BLOCK)
    wspec = pl.BlockSpec((1, BD, QKV_BLOCK), lambda j: (j, 0, 0))
    tspec = pl.BlockSpec((1, BD, BD), lambda j: (j, 0, 0))
    gspec = pl.BlockSpec((BD, 16), lambda j: (j, 0))
    tile = jax.ShapeDtypeStruct((nb, BD, BD), BF16)
    gw = jax.ShapeDtypeStruct((D_A, 16), BF16)
    return pl.pallas_call(
        _qkvw_kernel,
        out_shape=[tile, tile, tile, gw, gw],
        grid=(nb,),
        in_specs=[wspec, wspec, wspec, pl.BlockSpec((3, BD, 16), lambda j: (0, j, 0))],
        out_specs=[tspec, tspec, tspec, gspec, gspec],
        compiler_params=_cp(("parallel",)),
        name="qkvw",
    )(w2(wq), w2(wk), w2(wv), w_gate.reshape(3, D_A, 16))


def _feat_kernel(prev_ref, xm_ref, next_ref, cw_ref, cb_ref, bq_ref, bkt_ref, bv_ref, gqk_ref, gv_ref, gb_ref,
                 *o_refs, with_q):
    if with_q:
        q_ref, xc_ref, kt_ref, v_ref, g_ref, gl_ref = o_refs
    else:
        kt_ref, v_ref, g_ref, gl_ref = o_refs
    xm = xm_ref[0].astype(F32)
    p, n = _halo_rows(prev_ref, next_ref)
    xc = _silu(_conv3(xm, p, n, cw_ref, cb_ref))
    xcb = xc.astype(BF16)
    xmb = xm_ref[0]
    if with_q:
        xc_ref[0] = xcb
    g = (jnp.dot(xcb, gqk_ref[...], preferred_element_type=F32)
         + jnp.dot(xmb, gv_ref[...], preferred_element_type=F32) + gb_ref[...])
    g_ref[0] = g
    gl_ref[0] = g.T
    for j in range(D_A // BD):
        sl = slice(j * BD, (j + 1) * BD)
        if with_q:
            q = jnp.dot(xcb[:, sl], bq_ref[j], preferred_element_type=F32)
            q_ref[0, :, sl] = (q * (DH ** -0.5)).astype(BF16)
        kt_ref[0, sl, :] = lax.dot_general(bkt_ref[j], xcb[:, sl], (((1,), (1,)), ((), ())),
                                          preferred_element_type=F32).astype(BF16)
        v_ref[0, :, sl] = jnp.dot(xmb[:, sl], bv_ref[j], preferred_element_type=F32).astype(BF16)


def _feat(xm, cw, cb, bq, bkt, bv, gqk, gv, gb, *, tm, with_q=True):
    B, L, _ = xm.shape
    nb = D_A // BD
    tok = jax.ShapeDtypeStruct((B, L, D_A), BF16)
    tok_spec = pl.BlockSpec((1, tm, D_A), lambda b, i: (b, i, 0))
    shapes = [jax.ShapeDtypeStruct((B, D_A, L), BF16), tok, jax.ShapeDtypeStruct((B, L, 16), F32),
              jax.ShapeDtypeStruct((B, 16, L), F32)]
    specs = [pl.BlockSpec((1, D_A, tm), lambda b, i: (b, 0, i)), tok_spec,
             pl.BlockSpec((1, tm, 16), lambda b, i: (b, i, 0)), pl.BlockSpec((1, 16, tm), lambda b, i: (b, 0, i))]
    if with_q:
        shapes = [tok, tok] + shapes
        specs = [tok_spec, tok_spec] + specs
    return pl.pallas_call(
        functools.partial(_feat_kernel, with_q=with_q),
        out_shape=shapes,
        grid=(B, L // tm),
        in_specs=_halo_specs(tm, L, D_A) + [
            _const_spec((3, D_A)), _const_spec((1, D_A)),
            _const_spec((nb, BD, BD)), _const_spec((nb, BD, BD)), _const_spec((nb, BD, BD)),
            _const_spec((D_A, 16)), _const_spec((D_A, 16)), _const_spec((1, 16))],
        out_specs=specs,
        compiler_params=_cp(("parallel", "parallel")),
        name="feat" if with_q else "feat_ctx",
    )(xm, xm, xm, cw, cb.reshape(1, D_A), bq, bkt, bv, gqk, gv, gb.reshape(1, 16))


def _gate_vectors(gt, gl, idx, sign):
    T = gt.shape[0]
    sub = lax.broadcasted_iota(jnp.int32, gl.shape, 0)
    lane = lax.broadcasted_iota(jnp.int32, gt.shape, 1)
    ig_row = jnp.sum(jnp.where(sub == idx, gl, 0.0), axis=0, keepdims=True)
    fg_row = jnp.sum(jnp.where(sub == idx + A_HEADS, gl, 0.0), axis=0, keepdims=True)
    fg_col = jnp.sum(jnp.where(lane == idx + A_HEADS, gt, 0.0), axis=1, keepdims=True)
    lf_row = jax.nn.log_sigmoid(fg_row)
    lf_col = jax.nn.log_sigmoid(fg_col)
    r = lax.broadcasted_iota(jnp.int32, (T, T), 0)
    c = lax.broadcasted_iota(jnp.int32, (T, T), 1)
    mask = sign * (r - c) >= 0
    mask_t = sign * (c - r) >= 0
    b_col = jnp.sum(jnp.where(mask, lf_row, 0.0), axis=1, keepdims=True)
    b_row = jnp.sum(jnp.where(mask_t, lf_col, 0.0), axis=0, keepdims=True)
    b_last = jnp.sum(lf_row, axis=1, keepdims=True)
    return ig_row, b_col, b_row, b_last, mask


def _state_step(kt, v, ig_row, b_row, b_last, m_prev):
    log_w = b_last - b_row + ig_row
    m_new = jnp.maximum(b_last + m_prev, jnp.max(log_w, axis=1, keepdims=True))
    decay = jnp.exp(b_last + m_prev - m_new)
    w = jnp.exp(log_w - m_new)
    kw = kt.astype(F32) * w
    dC = jnp.dot(kw.astype(BF16), v, preferred_element_type=F32)
    dn = jnp.sum(kw, axis=1, keepdims=True)
    return decay, m_new, dC, dn


def _ctxstate_kernel(kt_ref, v_ref, gt_ref, gl_ref, c_ref, n_ref, m_ref):
    d = pl.program_id(0)
    h = pl.program_id(2)
    sign = 1 - 2 * d
    ig_row, _, b_row, b_last, _ = _gate_vectors(gt_ref[0], gl_ref[0], 8 * d + h, sign)
    m0 = jnp.zeros((1, 1), F32)
    _, m_new, dC, dn = _state_step(kt_ref[0], v_ref[0], ig_row, b_row, b_last, m0)
    c_ref[0, 0, 0] = dC
    n_ref[0, 0, 0] = jnp.broadcast_to(dn, (DH, 128))
    m_ref[0, 0, 0] = jnp.broadcast_to(m_new, (8, 128))


def _ctxstate(kt, v, gt, gl):
    B, _, Lc = kt.shape
    H = A_HEADS
    return pl.pallas_call(
        _ctxstate_kernel,
        out_shape=[jax.ShapeDtypeStruct((2, B, H, DH, DH), F32),
                   jax.ShapeDtypeStruct((2, B, H, DH, 128), F32),
                   jax.ShapeDtypeStruct((2, B, H, 8, 128), F32)],
        grid=(2, B, H),
        in_specs=[pl.BlockSpec((1, DH, Lc), lambda d, b, h: (b, h, 0)),
                  pl.BlockSpec((1, Lc, DH), lambda d, b, h: (b, 0, h)),
                  pl.BlockSpec((1, Lc, 16), lambda d, b, h: (b, 0, 0)),
                  pl.BlockSpec((1, 16, Lc), lambda d, b, h: (b, 0, 0))],
        out_specs=[pl.BlockSpec((1, 1, 1, DH, DH), lambda d, b, h: (d, b, h, 0, 0)),
                   pl.BlockSpec((1, 1, 1, DH, 128), lambda d, b, h: (d, b, h, 0, 0)),
                   pl.BlockSpec((1, 1, 1, 8, 128), lambda d, b, h: (d, b, h, 0, 0))],
        compiler_params=_cp(("parallel", "parallel", "parallel")),
        name="ctxstate",
    )(kt, v, gt, gl)


def _scan_kernel(qf_ref, ktf_ref, vf_ref, gtf_ref, glf_ref, qb_ref, ktb_ref, vb_ref, gtb_ref, glb_ref,
                 c0_ref, n0_ref, m0_ref, hf_ref, hb_ref,
                 cf_s, cbf_s, nf_s, nbf_s, mf_s, cb_s, cbb_s, nb_s, nbb_s, mb_s):
    head = pl.program_id(1)
    j = pl.program_id(2)
    dirs = ((0, qf_ref, ktf_ref, vf_ref, gtf_ref, glf_ref, hf_ref, cf_s, cbf_s, nf_s, nbf_s, mf_s),
            (1, qb_ref, ktb_ref, vb_ref, gtb_ref, glb_ref, hb_ref, cb_s, cbb_s, nb_s, nbb_s, mb_s))

    @pl.when(j == 0)
    def _():
        for d, _, _, _, _, _, _, c_s, cq_s, n_s, nq_s, m_s in dirs:
            c0 = c0_ref[d, 0, 0]
            c_s[...] = c0
            cq_s[...] = c0.astype(BF16)
            n0 = n0_ref[d, 0, 0]
            n_s[...] = n0
            nq_s[...] = n0.astype(BF16)
            m_s[...] = m0_ref[d, 0, 0]

    for d, q_ref, kt_ref, v_ref, gt_ref, gl_ref, h_ref, c_s, cq_s, n_s, nq_s, m_s in dirs:
        sign = 1 - 2 * d
        q = q_ref[0]
        kt = kt_ref[0]
        v = v_ref[0]
        ig_row, b_col, b_row, b_last, mask = _gate_vectors(gt_ref[0], gl_ref[0], 8 * d + head, sign)
        m_prev = m_s[0:1, 0:1]

        log_d = jnp.where(mask, b_col - b_row + ig_row, NEG)
        m_inter = b_col + m_prev
        m_t = jnp.maximum(m_inter, jnp.max(log_d, axis=1, keepdims=True))
        dmat = jnp.exp(log_d - m_t)
        inter = jnp.exp(m_inter - m_t)
        s = jnp.dot(q, kt, preferred_element_type=F32) * dmat
        num = (jnp.dot(s.astype(BF16), v, preferred_element_type=F32)
               + inter * jnp.dot(q, cq_s[...], preferred_element_type=F32))
        qn = jnp.dot(q, nq_s[...], preferred_element_type=F32)[:, 0:1]
        den = jnp.sum(s, axis=1, keepdims=True) + inter * qn
        hh = num / jnp.maximum(jnp.abs(den), jnp.exp(-m_t))
        h_ref[0] = hh.astype(h_ref.dtype)

        decay, m_new, dC, dn = _state_step(kt, v, ig_row, b_row, b_last, m_prev)
        c_new = decay * c_s[...] + dC
        c_s[...] = c_new
        cq_s[...] = c_new.astype(BF16)
        n_new = decay * n_s[...] + dn
        n_s[...] = n_new
        nq_s[...] = n_new.astype(BF16)
        m_s[...] = jnp.broadcast_to(m_new, (8, 128))


def _scan(q, kt, v, gt, gl, c0, n0, m0):
    B, L, _ = q.shape
    H = A_HEADS
    nc = L // CHUNK
    fwd = lambda j: j
    bwd = lambda j: nc - 1 - j

    def specs(cj):
        return [pl.BlockSpec((1, CHUNK, DH), lambda b, h, j: (b, cj(j), h)),
                pl.BlockSpec((1, DH, CHUNK), lambda b, h, j: (b, h, cj(j))),
                pl.BlockSpec((1, CHUNK, DH), lambda b, h, j: (b, cj(j), h)),
                pl.BlockSpec((1, CHUNK, 16), lambda b, h, j: (b, cj(j), 0)),
                pl.BlockSpec((1, 16, CHUNK), lambda b, h, j: (b, 0, cj(j)))]

    state = [pltpu.VMEM((DH, DH), F32), pltpu.VMEM((DH, DH), BF16),
             pltpu.VMEM((DH, 128), F32), pltpu.VMEM((DH, 128), BF16), pltpu.VMEM((8, 128), F32)]
    out = jax.ShapeDtypeStruct((B, L, D_A), BF16)
    return pl.pallas_call(
        _scan_kernel,
        out_shape=[out, out],
        grid=(B, H, nc),
        in_specs=specs(fwd) + specs(bwd) + [
            pl.BlockSpec((2, 1, 1, DH, DH), lambda b, h, j: (0, b, h, 0, 0)),
            pl.BlockSpec((2, 1, 1, DH, 128), lambda b, h, j: (0, b, h, 0, 0)),
            pl.BlockSpec((2, 1, 1, 8, 128), lambda b, h, j: (0, b, h, 0, 0))],
        out_specs=[pl.BlockSpec((1, CHUNK, DH), lambda b, h, j: (b, fwd(j), h)),
                   pl.BlockSpec((1, CHUNK, DH), lambda b, h, j: (b, bwd(j), h))],
        scratch_shapes=state + state,
        compiler_params=_cp(("parallel", "parallel", "arbitrary")),
        name="scan",
    )(q, kt, v, gt, gl, q, kt, v, gt, gl, c0, n0, m0)


def _store_permuted(o_ref, lead, x, s_ref):
    T, C = x.shape
    for c in range(C // LANES):
        s_ref[c] = x[:, c * LANES:(c + 1) * LANES]
    for c in range(C // LANES):
        for n2 in range(N2):
            o_ref[lead + (n2, slice(None), slice(c * LANES, (c + 1) * LANES))] = (
                s_ref[c, pl.ds(n2, T // N2, stride=N2), :].astype(o_ref.dtype))


def _hyconv_kernel(prev_ref, hy_ref, next_ref, cw_ref, cb_ref, up_ref, x0_ref, s_ref):
    hy = hy_ref[0].astype(F32)
    p, n = _halo_rows(prev_ref, next_ref)
    y = _conv3(hy, p, n, cw_ref, cb_ref)
    x0_ref[0] = y[:, 0:D_B].astype(BF16)
    _store_permuted(up_ref, (0,), y[:, D_B:2 * D_B] * y[:, 2 * D_B:3 * D_B], s_ref)


def _hyconv(hy, cw, cb, *, tm):
    B, L, C = hy.shape
    return pl.pallas_call(
        _hyconv_kernel,
        out_shape=[jax.ShapeDtypeStruct((B, N2, L // N2, D_B), BF16), jax.ShapeDtypeStruct((B, L, D_B), BF16)],
        grid=(B, L // tm),
        in_specs=_halo_specs(tm, L, C) + [_const_spec((3, C)), _const_spec((1, C))],
        out_specs=[pl.BlockSpec((1, N2, tm // N2, D_B), lambda b, i: (b, 0, i, 0)),
                   pl.BlockSpec((1, tm, D_B), lambda b, i: (b, i, 0))],
        scratch_shapes=[pltpu.VMEM((D_B // LANES, tm, LANES), F32)],
        compiler_params=_cp(("parallel", "parallel")),
        name="hyconv",
    )(hy, hy, hy, cw, cb.reshape(1, C))


def _filt_kernel(w1_ref, b1_ref, w2_ref, b2_ref, w3_ref, b3_ref, w4_ref, fr_ref, o_ref, s_ref, *, L):
    i = pl.program_id(0)
    T = s_ref.shape[1]
    pos = (lax.broadcasted_iota(jnp.int32, (T, 128), 0) + i * T).astype(F32)
    lane = lax.broadcasted_iota(jnp.int32, (T, 128), 1)
    t = pos / (L - 1.0)
    w = (2.0 * math.pi) * pos / L
    band = jnp.where(lane <= FILTER_BANDS, lane - 1, lane - 1 - FILTER_BANDS).astype(F32)
    f = 1e-4 + band * ((FILTER_BANDS - 1 - 1e-4) / (FILTER_BANDS - 1))
    z = jnp.where(lane == 0, t,
                  jnp.where(lane <= FILTER_BANDS, jnp.cos(f * w),
                            jnp.where(lane <= 2 * FILTER_BANDS, -jnp.sin(f * w), 0.0)))
    fr = fr_ref[...]
    a = jnp.sin(fr * (jnp.dot(z, w1_ref[...], preferred_element_type=F32, precision=HIGHEST) + b1_ref[...]))
    a = jnp.sin(fr * (jnp.dot(a, w2_ref[...], preferred_element_type=F32, precision=HIGHEST) + b2_ref[...]))
    a = jnp.sin(fr * (jnp.dot(a, w3_ref[...], preferred_element_type=F32, precision=HIGHEST) + b3_ref[...]))
    hf = jnp.dot(a, w4_ref[...], preferred_element_type=F32, precision=HIGHEST)
    ch = lax.broadcasted_iota(jnp.int32, (1, D_B), 1).astype(F32)
    deltas = jnp.abs(MIN_DECAY + ch * ((MAX_DECAY - MIN_DECAY) / (D_B - 1)))
    window = jnp.exp(-t[:, 0:1] * deltas) + FILTER_SHIFT
    _store_permuted(o_ref, (0,), hf[:, 0:D_B] * window, s_ref)
    _store_permuted(o_ref, (1,), jnp.where(pos[:, 0:1] == 0.0, 0.0, hf[:, D_B:2 * D_B] * window), s_ref)


def _filters(L, w1, b1, w2, b2, w3, b3, w4, freq, *, tm):
    emb = w1.shape[0]
    w1p = jnp.zeros((128, FILTER_HIDDEN), F32).at[0:emb].set(w1)
    row = lambda a: a.reshape(1, -1)
    return pl.pallas_call(
        functools.partial(_filt_kernel, L=L),
        out_shape=jax.ShapeDtypeStruct((2, N2, L // N2, D_B), BF16),
        grid=(L // tm,),
        in_specs=[_const_spec((128, FILTER_HIDDEN)), _const_spec((1, FILTER_HIDDEN)),
                  _const_spec((FILTER_HIDDEN, FILTER_HIDDEN)), _const_spec((1, FILTER_HIDDEN)),
                  _const_spec((FILTER_HIDDEN, FILTER_HIDDEN)), _const_spec((1, FILTER_HIDDEN)),
                  _const_spec((FILTER_HIDDEN, 2 * D_B)), _const_spec((1, FILTER_HIDDEN))],
        out_specs=pl.BlockSpec((2, N2, tm // N2, D_B), lambda i: (0, 0, i, 0)),
        scratch_shapes=[pltpu.VMEM((D_B // LANES, tm, LANES), F32)],
        compiler_params=_cp(("parallel",)),
        name="filt",
    )(w1p, row(b1), w2, row(b2), w3, row(b3), w4, row(freq))


def _dft_tables():
    half = N1 // 2
    g, ri, kl = np.meshgrid(np.arange(NG), np.arange(2), np.arange(KPG), indexing="ij")
    k1 = (KPG * g + kl).reshape(-1).astype(np.float64)
    is_im = ri.reshape(-1).astype(bool)
    valid = k1 <= half
    n1 = np.arange(half, dtype=np.float64)
    th = 2.0 * np.pi * np.outer(k1, n1) / N1
    f1 = np.where(is_im[:, None], -np.sin(th), np.cos(th)) * valid[:, None]
    ck = np.where((k1 == 0) | (k1 == half), 1.0, 2.0) * valid
    g3 = (np.where(is_im[:, None], -np.sin(th), np.cos(th)) * ck[:, None] / NFFT).T
    mf = np.zeros((NG, BD, BD))
    mb = np.zeros((NG, BD, BD))
    a = np.arange(N2, dtype=np.float64)
    for gi in range(NG):
        for k in range(KPG):
            kk = KPG * gi + k
            if kk > half:
                continue
            phi = 2.0 * np.pi * (np.outer(a, a) / N2 + np.outer(np.ones(N2), a) * kk / NFFT)
            cr, ci = np.cos(phi), -np.sin(phi)
            r0, r1 = k * N2, BD // 2 + k * N2
            mf[gi, r0:r0 + N2, r0:r0 + N2] = cr
            mf[gi, r0:r0 + N2, r1:r1 + N2] = -ci
            mf[gi, r1:r1 + N2, r0:r0 + N2] = ci
            mf[gi, r1:r1 + N2, r1:r1 + N2] = cr
            br, bi = np.cos(phi).T, np.sin(phi).T
            mb[gi, r0:r0 + N2, r0:r0 + N2] = br
            mb[gi, r0:r0 + N2, r1:r1 + N2] = -bi
            mb[gi, r1:r1 + N2, r0:r0 + N2] = bi
            mb[gi, r1:r1 + N2, r1:r1 + N2] = br
    f = lambda t: jnp.asarray(t, dtype=F32)
    return f(f1), f(g3), f(mf), f(mb)


def _stage1(sig, f1_ref, z_s):
    for j in range(N2):
        zj = jnp.dot(f1_ref[...], sig(j), preferred_element_type=F32)
        for c in range(z_s.shape[0]):
            z_s[c, pl.ds(j, F1R, stride=N2), :] = zj[:, c * LANES:(c + 1) * LANES]


def _group_rows(z_s, r0):
    return jnp.concatenate([z_s[c, pl.ds(r0, BD), :] for c in range(z_s.shape[0])], axis=1)


def _fspec_kernel(hp_ref, f1_ref, mf_ref, kf_ref, z_s):
    half = BD // 2
    for s in range(2):
        _stage1(lambda j: hp_ref[s, j], f1_ref, z_s)

        def body(g, carry):
            r0 = pl.multiple_of(g * BD, BD)
            x = jnp.dot(mf_ref[g], _group_rows(z_s, r0).astype(BF16), preferred_element_type=F32)
            if s == 0:
                kf_ref[pl.ds(r0, BD), :] = x
            else:
                kf_ref[pl.ds(r0, half), :] += x[0:half]
                kf_ref[pl.ds(r0 + half, half), :] -= x[half:BD]
            return carry

        lax.fori_loop(0, NG, body, 0, unroll=GROUP_UNROLL)


def _fspec(hp, f1b, mfb):
    C = hp.shape[-1]
    half = N1 // 2
    return pl.pallas_call(
        _fspec_kernel,
        out_shape=jax.ShapeDtypeStruct((ZR, C), F32),
        grid=(C // CT,),
        in_specs=[pl.BlockSpec((2, N2, half, CT), lambda t: (0, 0, 0, t)),
                  _const_spec((F1R, half)), _const_spec((NG, BD, BD))],
        out_specs=pl.BlockSpec((ZR, CT), lambda t: (0, t)),
        scratch_shapes=[pltpu.VMEM((CT // LANES, ZR, LANES), F32)],
        compiler_params=_cp(("parallel",)),
        name="fspec",
    )(hp, f1b, mfb)


def _lconv_kernel(up_ref, x0_ref, kf_ref, f1_ref, mf_ref, mb_ref, g3_ref, ds_ref, o_ref, z_s, y_s):
    half = BD // 2
    nsl = z_s.shape[0]
    _stage1(lambda j: up_ref[0, j], f1_ref, z_s)

    def body(g, carry):
        r0 = pl.multiple_of(g * BD, BD)
        x = jnp.dot(mf_ref[g], _group_rows(z_s, r0).astype(BF16), preferred_element_type=F32)
        kf = kf_ref[pl.ds(r0, BD), :]
        xr, xi = x[0:half], x[half:BD]
        kr, ki = kf[0:half], kf[half:BD]
        y = jnp.concatenate([xr * kr - xi * ki, xr * ki + xi * kr], axis=0)
        v = jnp.dot(mb_ref[g], y.astype(BF16), preferred_element_type=F32)
        for c in range(nsl):
            z_s[c, pl.ds(r0, BD), :] = v[:, c * LANES:(c + 1) * LANES]
        return carry

    lax.fori_loop(0, NG, body, 0, unroll=GROUP_UNROLL)

    for j in range(N2):
        vj = jnp.concatenate([z_s[c, pl.ds(j, F1R, stride=N2), :] for c in range(nsl)], axis=1)
        yj = (jnp.dot(g3_ref[...], vj.astype(BF16), preferred_element_type=F32)
              + up_ref[0, j].astype(F32) * ds_ref[...])
        for c in range(nsl):
            y_s[c, pl.ds(j, N1 // 2, stride=N2), :] = yj[:, c * LANES:(c + 1) * LANES]
    for c in range(nsl):
        sl = slice(c * LANES, (c + 1) * LANES)
        o_ref[0, :, sl] = (x0_ref[0, :, sl].astype(F32) * y_s[c]).astype(o_ref.dtype)


def _lconv(up, x0, kf, f1b, mfb, mbb, g3b, dskip):
    B, L, C = x0.shape
    half = N1 // 2
    return pl.pallas_call(
        _lconv_kernel,
        out_shape=jax.ShapeDtypeStruct((B, L, C), BF16),
        grid=(C // CT, B),
        in_specs=[pl.BlockSpec((1, N2, half, CT), lambda t, b: (b, 0, 0, t)),
                  pl.BlockSpec((1, L, CT), lambda t, b: (b, 0, t)),
                  pl.BlockSpec((ZR, CT), lambda t, b: (0, t), pipeline_mode=pl.Buffered(1)),
                  _const_spec((F1R, half)), _const_spec((NG, BD, BD)), _const_spec((NG, BD, BD)),
                  _const_spec((half, F1R)),
                  pl.BlockSpec((1, CT), lambda t, b: (0, t))],
        out_specs=pl.BlockSpec((1, L, CT), lambda t, b: (b, 0, t)),
        scratch_shapes=[pltpu.VMEM((CT // LANES, ZR, LANES), F32), pltpu.VMEM((CT // LANES, L, LANES), F32)],
        compiler_params=_cp(("parallel", "parallel")),
        name="lconv",
    )(up, x0, kf, f1b, mfb, mbb, g3b, dskip.reshape(1, C))


def _merge_kernel(hf_ref, hb_ref, xc_ref, z_ref, yb_ref, gab_ref, x_ref, mod_ref, ng_ref, sk_ref,
                  wpa_ref, wpb_ref, wo_ref, o_ref):
    h = hf_ref[0].astype(F32) + hb_ref[0].astype(F32)
    xc = xc_ref[0].astype(F32)
    parts = []
    for k in range(A_HEADS):
        sl = slice(k * DH, (k + 1) * DH)
        hh = h[:, sl]
        hn = hh * lax.rsqrt(jnp.mean(hh * hh, axis=-1, keepdims=True) + EPS)
        parts.append(hn * ng_ref[:, sl] + sk_ref[:, sl] * xc[:, sl])
    ya = jax.nn.sigmoid(z_ref[0].astype(F32)) * jnp.concatenate(parts, axis=1)
    gab = gab_ref[0].astype(F32)
    mix = (jax.nn.sigmoid(gab[:, 0:D_MODEL]) * jnp.dot(ya.astype(BF16), wpa_ref[...], preferred_element_type=F32)
           + jax.nn.sigmoid(gab[:, D_MODEL:]) * jnp.dot(yb_ref[0], wpb_ref[...], preferred_element_type=F32))
    out = jnp.dot(mix.astype(BF16), wo_ref[...], preferred_element_type=F32)
    o_ref[0] = x_ref[0] + mod_ref[0, 5:6, :] * out


def _merge(hf, hb, xc, z, yb, gab, x, mods, a_norm_g, a_skip, wpa, wpb, wo, *, tm):
    B, L, D = x.shape
    tok = lambda w: pl.BlockSpec((1, tm, w), lambda b, i: (b, i, 0))
    return pl.pallas_call(
        _merge_kernel,
        out_shape=jax.ShapeDtypeStruct((B, L, D), F32),
        grid=(B, L // tm),
        in_specs=[tok(D_A), tok(D_A),
                  tok(D_A), tok(D_A), tok(D_B), tok(2 * D_MODEL), tok(D),
                  pl.BlockSpec((1, N_MOD, D), lambda b, i: (b, 0, 0)),
                  _const_spec((1, D_A)), _const_spec((1, D_A)),
                  _const_spec((D_A, D)), _const_spec((D_B, D)), _const_spec((D, D))],
        out_specs=tok(D),
        compiler_params=_cp(("parallel", "parallel")),
        name="merge",
    )(hf, hb, xc, z, yb, gab, x, mods, a_norm_g.reshape(1, D_A), a_skip.reshape(1, D_A), wpa, wpb, wo)


def kernel(x, c, ctx, c_ctx, w_ada, b_ada, norm_g, ffn1_up, ffn1_down, ffn2_up, ffn2_down, w_in, a_conv_w, a_conv_b, a_wq, a_wk, a_wv, a_w_gate, a_b_gate, a_norm_g, a_skip, b_conv_w, b_conv_b, b_filt_w1, b_filt_b1, b_filt_w2, b_filt_b2, b_filt_w3, b_filt_b3, b_filt_w4, b_filt_freq, b_skip, w_pa, w_pb, w_out, final_g):
    B, L, D = x.shape
    Lc = ctx.shape[1]
    assert w_ada.shape[0] == 1, "single-layer stack"
    assert 2 * L == NFFT and D == D_MODEL and L // GRID_W == GRID_W
    TM = 512

    c8 = jnp.zeros((8, D), F32).at[0:B].set(c).at[B].set(c_ctx)
    mods = _mods(c8, w_ada[0], b_ada[0]).reshape(8, N_MOD, D)
    ng = norm_g[0]
    row_b = lambda b: b
    row_ctx = lambda b: B

    up1 = ffn1_up[0].astype(BF16)
    dn1 = ffn1_down[0].astype(BF16)
    up2 = ffn2_up[0].astype(BF16)
    dn2 = ffn2_down[0].astype(BF16)
    w_in_b = w_in[0].astype(BF16)

    x1 = _ffn(x, mods, row_b, ng, up1, dn1, sub=0, tm=TM, pos=_pos_table())
    ctx1 = _ffn(ctx, mods, row_ctx, ng, up1, dn1, sub=0, tm=Lc)

    widths = (D_A, D_A, 3 * D_B, 2 * D_MODEL)
    xm, z, hy, gab = _proj(x1, mods, row_b, ng, w_in_b, widths, tm=256)
    (xm_c,) = _proj(ctx1, mods, row_ctx, ng, w_in_b, (D_A,), tm=Lc)

    bq, bkt, bv, gqk, gv = _qkvw(a_wq[0], a_wk[0], a_wv[0], a_w_gate[0])
    fw = (a_conv_w[0], a_conv_b[0], bq, bkt, bv, gqk, gv, a_b_gate[0])
    q, xc, kt, v, g, gl = _feat(xm, *fw, tm=TM)
    kt_c, v_c, g_c, gl_c = _feat(xm_c, *fw, tm=Lc, with_q=False)

    c0, n0, m0 = _ctxstate(kt_c, v_c, g_c, gl_c)
    hf, hb = _scan(q, kt, v, g, gl, c0, n0, m0)

    up, x0c = _hyconv(hy, b_conv_w[0], b_conv_b[0], tm=TM)
    f1b, g3b, mfb, mbb = (a.astype(BF16) for a in _dft_tables())
    hp = _filters(L, b_filt_w1[0], b_filt_b1[0], b_filt_w2[0], b_filt_b2[0], b_filt_w3[0], b_filt_b3[0],
                  b_filt_w4[0], b_filt_freq[0], tm=TM)
    kf = _fspec(hp, f1b, mfb)
    yb = _lconv(up, x0c, kf, f1b, mfb, mbb, g3b, b_skip[0])

    x2 = _merge(hf, hb, xc, z, yb, gab, x1, mods, a_norm_g[0], a_skip[0],
                w_pa[0].astype(BF16), w_pb[0].astype(BF16), w_out[0].astype(BF16), tm=TM)

    return _ffn(x2, mods, row_b, ng, up2, dn2, sub=2, tm=TM, final_g=final_g)
```

```python
import functools
import math

import numpy as np
import jax
import jax.numpy as jnp
from jax import lax
from jax.experimental import pallas as pl
from jax.experimental.pallas import tpu as pltpu

F32 = jnp.float32
BF16 = jnp.bfloat16

D_MODEL = 1024
D_A = 2048
A_HEADS = 4
DH = D_A // A_HEADS
QKV_BLOCK = 4
D_B = 1024
D_FF = 2816
EPS = 1e-6
N_MOD = 9
GRID_W = 64
CHUNK = 256
HEADS_PER_STEP = 2
FILTER_BANDS = 16
FILTER_HIDDEN = 64
DECAY_TARGET = 1e-2
MAX_DECAY = math.log(DECAY_TARGET) / 0.3
MIN_DECAY = math.log(DECAY_TARGET) / 1.5
FILTER_SHIFT = 0.05

LANES = 128
SUBLANES = 8
BD = 256

N2 = SUBLANES
NFFT = 8192
N1 = NFFT // N2
KPG = BD // (2 * N2)
NG = (N1 // 2) // KPG + 1
F1R = NG * 2 * KPG
ZR = F1R * N2
CT = 256
GROUP_UNROLL = 11

VMEM_LIMIT = 56 * 1024 * 1024
HIGHEST = lax.Precision.HIGHEST
NEG = -1e30


def _cp(sem):
    return pltpu.CompilerParams(dimension_semantics=sem, vmem_limit_bytes=VMEM_LIMIT)


def _const_spec(shape):
    nd = len(shape)
    return pl.BlockSpec(shape, lambda *_: (0,) * nd, pipeline_mode=pl.Buffered(1))


def _silu(x):
    return x * jax.nn.sigmoid(x)


def _rms(x, g):
    return x * lax.rsqrt(jnp.mean(x * x, axis=-1, keepdims=True) + EPS) * g


def _mods_kernel(c_ref, w_ref, b_ref, o_ref):
    cs = _silu(c_ref[...])
    o_ref[...] = jnp.dot(cs, w_ref[...], preferred_element_type=F32, precision=HIGHEST) + b_ref[...]


def _mods(c8, w_ada, b_ada):
    n = w_ada.shape[1]
    tn = 1024
    return pl.pallas_call(
        _mods_kernel,
        out_shape=jax.ShapeDtypeStruct((8, n), F32),
        grid=(n // tn,),
        in_specs=[pl.BlockSpec((8, D_MODEL), lambda j: (0, 0)),
                  pl.BlockSpec((D_MODEL, tn), lambda j: (0, j)),
                  pl.BlockSpec((1, tn), lambda j: (0, j))],
        out_specs=pl.BlockSpec((8, tn), lambda j: (0, j)),
        compiler_params=_cp(("parallel",)),
        name="mods",
    )(c8, w_ada, b_ada.reshape(1, n))


def _postab_kernel(om_ref, o_ref):
    nf = om_ref.shape[1]
    p = lax.broadcasted_iota(jnp.int32, (GRID_W, nf), 0).astype(F32)
    a = p * om_ref[...]
    o_ref[:, 0:nf] = jnp.sin(a)
    o_ref[:, nf:2 * nf] = jnp.cos(a)


def _pos_table():
    nf = D_MODEL // 4
    omega = 1.0 / (10000.0 ** (jnp.arange(nf, dtype=F32) / nf))
    return pl.pallas_call(
        _postab_kernel,
        out_shape=jax.ShapeDtypeStruct((GRID_W, 2 * nf), F32),
        name="postab",
    )(omega.reshape(1, nf))


FF_CHUNKS = ((0, 6 * BD), (6 * BD, D_FF))


def _ffn_kernel(*refs, sub, has_pos, final):
    it = iter(refs)
    x_ref = next(it)
    pos_ref = next(it) if has_pos else None
    mod_ref = next(it)
    g_ref = next(it)
    up_ref = next(it)
    dn_ref = next(it)
    fg_ref = next(it) if final else None
    o_ref = next(it)

    x = x_ref[0]
    if has_pos:
        tm = x.shape[0]
        per = tm // GRID_W
        half = pos_ref.shape[1]
        i = pl.program_id(1)
        ecol = pos_ref[...]
        rows = []
        for q in range(per):
            erow = jnp.broadcast_to(pos_ref[pl.ds(i * per + q, 1), :], (GRID_W, half))
            rows.append(jnp.concatenate([erow, ecol], axis=1))
        x = x + jnp.concatenate(rows, axis=0)
    shift = mod_ref[0, 3 * sub:3 * sub + 1, :]
    scale = mod_ref[0, 3 * sub + 1:3 * sub + 2, :]
    gate = mod_ref[0, 3 * sub + 2:3 * sub + 3, :]
    h = _rms(x, g_ref[sub:sub + 1, :]) * (1.0 + scale) + shift
    hb = h.astype(BF16)
    acc = jnp.zeros(x.shape, F32)
    for lo, hi in FF_CHUNKS:
        gg = jnp.dot(hb, up_ref[:, lo:hi], preferred_element_type=F32)
        uu = jnp.dot(hb, up_ref[:, D_FF + lo:D_FF + hi], preferred_element_type=F32)
        a = (_silu(gg) * uu).astype(BF16)
        acc = acc + jnp.dot(a, dn_ref[lo:hi, :], preferred_element_type=F32)
    y = x + 0.5 * gate * acc
    if final:
        y = _rms(y, fg_ref[...])
    o_ref[0] = y


def _ffn(x, mods, mod_row, norm_g, up_b, dn_b, *, sub, tm, pos=None, final_g=None):
    B, L, D = x.shape
    args = [x]
    specs = [pl.BlockSpec((1, tm, D), lambda b, i: (b, i, 0))]
    if pos is not None:
        args.append(pos)
        specs.append(_const_spec(pos.shape))
    args += [mods, norm_g, up_b, dn_b]
    specs += [pl.BlockSpec((1, N_MOD, D), lambda b, i: (mod_row(b), 0, 0)),
              _const_spec((3, D)), _const_spec((D, 2 * D_FF)), _const_spec((D_FF, D))]
    if final_g is not None:
        args.append(final_g.reshape(1, D))
        specs.append(_const_spec((1, D)))
    return pl.pallas_call(
        functools.partial(_ffn_kernel, sub=sub, has_pos=pos is not None, final=final_g is not None),
        out_shape=jax.ShapeDtypeStruct((B, L, D), F32),
        grid=(B, L // tm),
        in_specs=specs,
        out_specs=pl.BlockSpec((1, tm, D), lambda b, i: (b, i, 0)),
        compiler_params=_cp(("parallel", "parallel")),
        name=f"ffn{sub}",
    )(*args)


def _proj_kernel(x_ref, mod_ref, g_ref, w_ref, *o_refs, widths):
    x = x_ref[0]
    shift = mod_ref[0, 3:4, :]
    scale = mod_ref[0, 4:5, :]
    hb = (_rms(x, g_ref[1:2, :]) * (1.0 + scale) + shift).astype(BF16)
    lo = 0
    for o_ref, w in zip(o_refs, widths):
        o_ref[0] = jnp.dot(hb, w_ref[:, lo:lo + w], preferred_element_type=F32).astype(o_ref.dtype)
        lo += w


def _proj(x, mods, mod_row, norm_g, w_b, widths, *, tm):
    B, L, D = x.shape
    n = sum(widths)
    return pl.pallas_call(
        functools.partial(_proj_kernel, widths=widths),
        out_shape=[jax.ShapeDtypeStruct((B, L, w), BF16) for w in widths],
        grid=(B, L // tm),
        in_specs=[pl.BlockSpec((1, tm, D), lambda b, i: (b, i, 0)),
                  pl.BlockSpec((1, N_MOD, D), lambda b, i: (mod_row(b), 0, 0)),
                  _const_spec((3, D)), _const_spec((D, n))],
        out_specs=[pl.BlockSpec((1, tm, w), lambda b, i: (b, i, 0)) for w in widths],
        compiler_params=_cp(("parallel", "parallel")),
        name="proj",
    )(x, mods, norm_g, w_b)


HALO = 16


def _shift_matrix():
    return jnp.concatenate([jnp.eye(BD, k=-1, dtype=BF16), jnp.eye(BD, k=1, dtype=BF16)], axis=0)


def _conv3(xb, prev_row, next_row, w_ref, b_ref, s_ref):
    T, C = xb.shape
    w0, w1, w2, b = w_ref[0:1, :], w_ref[1:2, :], w_ref[2:3, :], b_ref[...]
    x = xb.astype(F32)
    r8 = lax.broadcasted_iota(jnp.int32, (SUBLANES, C), 0)
    out = []
    for r0 in range(0, T, BD):
        r1 = r0 + BD
        sh = jnp.dot(s_ref[...], xb[r0:r1], preferred_element_type=F32)
        y = sh[0:BD] * w0 + x[r0:r1] * w1 + sh[BD:2 * BD] * w2 + b
        before = prev_row if r0 == 0 else x[r0 - 1:r0]
        after = next_row if r1 == T else x[r1:r1 + 1]
        first = before * w0 + x[r0:r0 + 1] * w1 + x[r0 + 1:r0 + 2] * w2 + b
        last = x[r1 - 2:r1 - 1] * w0 + x[r1 - 1:r1] * w1 + after * w2 + b
        top = jnp.where(r8 == 0, first, y[0:SUBLANES])
        bot = jnp.where(r8 == SUBLANES - 1, last, y[BD - SUBLANES:BD])
        out += [top, y[SUBLANES:BD - SUBLANES], bot]
    return jnp.concatenate(out, axis=0)


def _halo_rows(prev_ref, next_ref):
    i = pl.program_id(1)
    n = pl.num_programs(1)
    p = prev_ref[0, HALO - 1:HALO, :].astype(F32)
    q = next_ref[0, 0:1, :].astype(F32)
    p = jnp.where(i == 0, 0.0, p)
    q = jnp.where(i == n - 1, 0.0, q)
    return p, q


def _halo_specs(tm, L, C):
    r = tm // HALO
    nb = L // HALO
    return [pl.BlockSpec((1, HALO, C), lambda b, i: (b, jnp.maximum(i * r - 1, 0), 0)),
            pl.BlockSpec((1, tm, C), lambda b, i: (b, i, 0)),
            pl.BlockSpec((1, HALO, C), lambda b, i: (b, jnp.minimum((i + 1) * r, nb - 1), 0))]


def _qkvw_kernel(wq_ref, wk_ref, wv_ref, wg_ref, bq_ref, bkt_ref, bv_ref, gqk_ref, gv_ref):
    r = lax.broadcasted_iota(jnp.int32, (BD, BD), 0)
    c = lax.broadcasted_iota(jnp.int32, (BD, BD), 1)
    shift = QKV_BLOCK.bit_length() - 1
    same_block = (r >> shift) == (c >> shift)
    col_in_block = c & (QKV_BLOCK - 1)

    def tile(w_ref):
        w2 = w_ref[0]
        a = jnp.zeros((BD, BD), F32)
        for j in range(QKV_BLOCK):
            a = a + jnp.where(col_in_block == j, w2[:, j:j + 1], 0.0)
        return jnp.where(same_block, a, 0.0)

    tq = tile(wq_ref)
    tk = tile(wk_ref)
    tv = tile(wv_ref)
    bq_ref[0] = tq.astype(BF16)
    bkt_ref[0] = tk.T.astype(BF16)
    bv_ref[0] = tv.astype(BF16)
    gqk_ref[...] = (jnp.dot(tq, wg_ref[0], preferred_element_type=F32, precision=HIGHEST)
                    + jnp.dot(tk, wg_ref[1], preferred_element_type=F32, precision=HIGHEST)).astype(BF16)
    gv_ref[...] = jnp.dot(tv, wg_ref[2], preferred_element_type=F32, precision=HIGHEST).astype(BF16)


def _qkvw(wq, wk, wv, w_gate):
    nb = D_A // BD
    w2 = lambda w: w.reshape(nb, BD, QKV_BLOCK)
    wspec = pl.BlockSpec((1, BD, QKV_BLOCK), lambda j: (j, 0, 0))
    tspec = pl.BlockSpec((1, BD, BD), lambda j: (j, 0, 0))
    gspec = pl.BlockSpec((BD, 16), lambda j: (j, 0))
    tile = jax.ShapeDtypeStruct((nb, BD, BD), BF16)
    gw = jax.ShapeDtypeStruct((D_A, 16), BF16)
    return pl.pallas_call(
        _qkvw_kernel,
        out_shape=[tile, tile, tile, gw, gw],
        grid=(nb,),
        in_specs=[wspec, wspec, wspec, pl.BlockSpec((3, BD, 16), lambda j: (0, j, 0))],
        out_specs=[tspec, tspec, tspec, gspec, gspec],
        compiler_params=_cp(("parallel",)),
        name="qkvw",
    )(w2(wq), w2(wk), w2(wv), w_gate.reshape(3, D_A, 16))


def _feat_kernel(prev_ref, xm_ref, next_ref, cw_ref, cb_ref, sh_ref, bq_ref, bkt_ref, bv_ref, gqk_ref, gv_ref,
                 gb_ref, *o_refs, with_q):
    if with_q:
        q_ref, xc_ref, kt_ref, v_ref, g_ref, gl_ref = o_refs
    else:
        kt_ref, v_ref, g_ref, gl_ref = o_refs
    xmb = xm_ref[0]
    p, n = _halo_rows(prev_ref, next_ref)
    xc = _silu(_conv3(xmb, p, n, cw_ref, cb_ref, sh_ref))
    xcb = xc.astype(BF16)
    if with_q:
        xc_ref[0] = xcb
    g = (jnp.dot(xcb, gqk_ref[...], preferred_element_type=F32)
         + jnp.dot(xmb, gv_ref[...], preferred_element_type=F32) + gb_ref[...])
    g_ref[0] = g
    gl_ref[0] = g.T
    for j in range(D_A // BD):
        sl = slice(j * BD, (j + 1) * BD)
        if with_q:
            q = jnp.dot(xcb[:, sl], bq_ref[j], preferred_element_type=F32)
            q_ref[0, :, sl] = (q * (DH ** -0.5)).astype(BF16)
        kt_ref[0, sl, :] = lax.dot_general(bkt_ref[j], xcb[:, sl], (((1,), (1,)), ((), ())),
                                          preferred_element_type=F32).astype(BF16)
        v_ref[0, :, sl] = jnp.dot(xmb[:, sl], bv_ref[j], preferred_element_type=F32).astype(BF16)


def _feat(xm, cw, cb, bq, bkt, bv, gqk, gv, gb, *, tm, with_q=True):
    B, L, _ = xm.shape
    nb = D_A // BD
    tok = jax.ShapeDtypeStruct((B, L, D_A), BF16)
    tok_spec = pl.BlockSpec((1, tm, D_A), lambda b, i: (b, i, 0))
    shapes = [jax.ShapeDtypeStruct((B, D_A, L), BF16), tok, jax.ShapeDtypeStruct((B, L, 16), F32),
              jax.ShapeDtypeStruct((B, 16, L), F32)]
    specs = [pl.BlockSpec((1, D_A, tm), lambda b, i: (b, 0, i)), tok_spec,
             pl.BlockSpec((1, tm, 16), lambda b, i: (b, i, 0)), pl.BlockSpec((1, 16, tm), lambda b, i: (b, 0, i))]
    if with_q:
        shapes = [tok, tok] + shapes
        specs = [tok_spec, tok_spec] + specs
    return pl.pallas_call(
        functools.partial(_feat_kernel, with_q=with_q),
        out_shape=shapes,
        grid=(B, L // tm),
        in_specs=_halo_specs(tm, L, D_A) + [
            _const_spec((3, D_A)), _const_spec((1, D_A)), _const_spec((2 * BD, BD)),
            _const_spec((nb, BD, BD)), _const_spec((nb, BD, BD)), _const_spec((nb, BD, BD)),
            _const_spec((D_A, 16)), _const_spec((D_A, 16)), _const_spec((1, 16))],
        out_specs=specs,
        compiler_params=_cp(("parallel", "parallel")),
        name="feat" if with_q else "feat_ctx",
    )(xm, xm, xm, cw, cb.reshape(1, D_A), _shift_matrix(), bq, bkt, bv, gqk, gv, gb.reshape(1, 16))


def _gate_vectors(gt, gl, idx, sign):
    T = gt.shape[0]
    sub = lax.broadcasted_iota(jnp.int32, gl.shape, 0)
    lane = lax.broadcasted_iota(jnp.int32, gt.shape, 1)
    ig_row = jnp.sum(jnp.where(sub == idx, gl, 0.0), axis=0, keepdims=True)
    fg_row = jnp.sum(jnp.where(sub == idx + A_HEADS, gl, 0.0), axis=0, keepdims=True)
    fg_col = jnp.sum(jnp.where(lane == idx + A_HEADS, gt, 0.0), axis=1, keepdims=True)
    lf_row = jax.nn.log_sigmoid(fg_row)
    lf_col = jax.nn.log_sigmoid(fg_col)
    r = lax.broadcasted_iota(jnp.int32, (T, T), 0)
    c = lax.broadcasted_iota(jnp.int32, (T, T), 1)
    mask = sign * (r - c) >= 0
    mask_t = sign * (c - r) >= 0
    b_col = jnp.sum(jnp.where(mask, lf_row, 0.0), axis=1, keepdims=True)
    b_row = jnp.sum(jnp.where(mask_t, lf_col, 0.0), axis=0, keepdims=True)
    b_last = jnp.sum(lf_row, axis=1, keepdims=True)
    return ig_row, b_col, b_row, b_last, mask


def _state_step(kt, v, ig_row, b_row, b_last, m_prev):
    log_w = b_last - b_row + ig_row
    m_new = jnp.maximum(b_last + m_prev, jnp.max(log_w, axis=1, keepdims=True))
    decay = jnp.exp(b_last + m_prev - m_new)
    w = jnp.exp(log_w - m_new)
    kw = kt.astype(F32) * w
    dC = jnp.dot(kw.astype(BF16), v, preferred_element_type=F32)
    dn = jnp.sum(kw, axis=1, keepdims=True)
    return decay, m_new, dC, dn


def _ctxstate_kernel(kt_ref, v_ref, gt_ref, gl_ref, c_ref, n_ref, m_ref):
    d = pl.program_id(0)
    h = pl.program_id(2)
    sign = 1 - 2 * d
    ig_row, _, b_row, b_last, _ = _gate_vectors(gt_ref[0], gl_ref[0], 8 * d + h, sign)
    m0 = jnp.zeros((1, 1), F32)
    _, m_new, dC, dn = _state_step(kt_ref[0], v_ref[0], ig_row, b_row, b_last, m0)
    c_ref[0, 0, 0] = dC
    n_ref[0, 0, 0] = jnp.broadcast_to(dn, (DH, 128))
    m_ref[0, 0, 0] = jnp.broadcast_to(m_new, (8, 128))


def _ctxstate(kt, v, gt, gl):
    B, _, Lc = kt.shape
    H = A_HEADS
    return pl.pallas_call(
        _ctxstate_kernel,
        out_shape=[jax.ShapeDtypeStruct((2, B, H, DH, DH), F32),
                   jax.ShapeDtypeStruct((2, B, H, DH, 128), F32),
                   jax.ShapeDtypeStruct((2, B, H, 8, 128), F32)],
        grid=(2, B, H),
        in_specs=[pl.BlockSpec((1, DH, Lc), lambda d, b, h: (b, h, 0)),
                  pl.BlockSpec((1, Lc, DH), lambda d, b, h: (b, 0, h)),
                  pl.BlockSpec((1, Lc, 16), lambda d, b, h: (b, 0, 0)),
                  pl.BlockSpec((1, 16, Lc), lambda d, b, h: (b, 0, 0))],
        out_specs=[pl.BlockSpec((1, 1, 1, DH, DH), lambda d, b, h: (d, b, h, 0, 0)),
                   pl.BlockSpec((1, 1, 1, DH, 128), lambda d, b, h: (d, b, h, 0, 0)),
                   pl.BlockSpec((1, 1, 1, 8, 128), lambda d, b, h: (d, b, h, 0, 0))],
        compiler_params=_cp(("parallel", "parallel", "parallel")),
        name="ctxstate",
    )(kt, v, gt, gl)


def _scan_kernel(qf_ref, ktf_ref, vf_ref, gtf_ref, glf_ref, qb_ref, ktb_ref, vb_ref, gtb_ref, glb_ref,
                 c0_ref, n0_ref, m0_ref, hf_ref, hb_ref, *scratch):
    hp = pl.program_id(1)
    j = pl.program_id(2)
    io = ((qf_ref, ktf_ref, vf_ref, gtf_ref, glf_ref, hf_ref), (qb_ref, ktb_ref, vb_ref, gtb_ref, glb_ref, hb_ref))
    chains = [(d, hh) + io[d] + tuple(scratch[5 * (d * HEADS_PER_STEP + hh):5 * (d * HEADS_PER_STEP + hh) + 5])
              for d in range(2) for hh in range(HEADS_PER_STEP)]

    @pl.when(j == 0)
    def _():
        for d, hh, _, _, _, _, _, _, c_s, cq_s, n_s, nq_s, m_s in chains:
            c0 = c0_ref[d, 0, hh]
            c_s[...] = c0
            cq_s[...] = c0.astype(BF16)
            n0 = n0_ref[d, 0, hh]
            n_s[...] = n0
            nq_s[...] = n0.astype(BF16)
            m_s[...] = m0_ref[d, 0, hh]

    for d, hh, q_ref, kt_ref, v_ref, gt_ref, gl_ref, h_ref, c_s, cq_s, n_s, nq_s, m_s in chains:
        sign = 1 - 2 * d
        hsl = slice(hh * DH, (hh + 1) * DH)
        q = q_ref[0, :, hsl]
        kt = kt_ref[0, hsl, :]
        v = v_ref[0, :, hsl]
        head = hp * HEADS_PER_STEP + hh
        ig_row, b_col, b_row, b_last, mask = _gate_vectors(gt_ref[0], gl_ref[0], 8 * d + head, sign)
        m_prev = m_s[0:1, 0:1]

        log_d = jnp.where(mask, b_col - b_row + ig_row, NEG)
        m_inter = b_col + m_prev
        m_t = jnp.maximum(m_inter, jnp.max(log_d, axis=1, keepdims=True))
        dmat = jnp.exp(log_d - m_t)
        inter = jnp.exp(m_inter - m_t)
        s = jnp.dot(q, kt, preferred_element_type=F32) * dmat
        num = (jnp.dot(s.astype(BF16), v, preferred_element_type=F32)
               + inter * jnp.dot(q, cq_s[...], preferred_element_type=F32))
        qn = jnp.dot(q, nq_s[...], preferred_element_type=F32)[:, 0:1]
        den = jnp.sum(s, axis=1, keepdims=True) + inter * qn
        h_ref[0, :, hsl] = (num / jnp.maximum(jnp.abs(den), jnp.exp(-m_t))).astype(h_ref.dtype)

        decay, m_new, dC, dn = _state_step(kt, v, ig_row, b_row, b_last, m_prev)
        c_new = decay * c_s[...] + dC
        c_s[...] = c_new
        cq_s[...] = c_new.astype(BF16)
        n_new = decay * n_s[...] + dn
        n_s[...] = n_new
        nq_s[...] = n_new.astype(BF16)
        m_s[...] = jnp.broadcast_to(m_new, (8, 128))


def _scan(q, kt, v, gt, gl, c0, n0, m0):
    B, L, _ = q.shape
    hs = HEADS_PER_STEP
    nc = L // CHUNK
    fwd = lambda j: j
    bwd = lambda j: nc - 1 - j

    def specs(cj):
        return [pl.BlockSpec((1, CHUNK, hs * DH), lambda b, h, j: (b, cj(j), h)),
                pl.BlockSpec((1, hs * DH, CHUNK), lambda b, h, j: (b, h, cj(j))),
                pl.BlockSpec((1, CHUNK, hs * DH), lambda b, h, j: (b, cj(j), h)),
                pl.BlockSpec((1, CHUNK, 16), lambda b, h, j: (b, cj(j), 0)),
                pl.BlockSpec((1, 16, CHUNK), lambda b, h, j: (b, 0, cj(j)))]

    state = [pltpu.VMEM((DH, DH), F32), pltpu.VMEM((DH, DH), BF16),
             pltpu.VMEM((DH, 128), F32), pltpu.VMEM((DH, 128), BF16), pltpu.VMEM((8, 128), F32)]
    out = jax.ShapeDtypeStruct((B, L, D_A), BF16)
    return pl.pallas_call(
        _scan_kernel,
        out_shape=[out, out],
        grid=(B, A_HEADS // hs, nc),
        in_specs=specs(fwd) + specs(bwd) + [
            pl.BlockSpec((2, 1, hs, DH, DH), lambda b, h, j: (0, b, h, 0, 0)),
            pl.BlockSpec((2, 1, hs, DH, 128), lambda b, h, j: (0, b, h, 0, 0)),
            pl.BlockSpec((2, 1, hs, 8, 128), lambda b, h, j: (0, b, h, 0, 0))],
        out_specs=[pl.BlockSpec((1, CHUNK, hs * DH), lambda b, h, j: (b, fwd(j), h)),
                   pl.BlockSpec((1, CHUNK, hs * DH), lambda b, h, j: (b, bwd(j), h))],
        scratch_shapes=state * (2 * hs),
        compiler_params=_cp(("parallel", "parallel", "arbitrary")),
        name="scan",
    )(q, kt, v, gt, gl, q, kt, v, gt, gl, c0, n0, m0)


def _store_permuted(o_ref, lead, x, s_ref):
    T, C = x.shape
    for c in range(C // LANES):
        s_ref[c] = x[:, c * LANES:(c + 1) * LANES]
    for c in range(C // LANES):
        for n2 in range(N2):
            o_ref[lead + (n2, slice(None), slice(c * LANES, (c + 1) * LANES))] = (
                s_ref[c, pl.ds(n2, T // N2, stride=N2), :].astype(o_ref.dtype))


def _hyconv_kernel(prev_ref, hy_ref, next_ref, cw_ref, cb_ref, sh_ref, up_ref, x0_ref, s_ref):
    p, n = _halo_rows(prev_ref, next_ref)
    y = _conv3(hy_ref[0], p, n, cw_ref, cb_ref, sh_ref)
    x0_ref[0] = y[:, 0:D_B].astype(BF16)
    _store_permuted(up_ref, (0,), y[:, D_B:2 * D_B] * y[:, 2 * D_B:3 * D_B], s_ref)


def _hyconv(hy, cw, cb, *, tm):
    B, L, C = hy.shape
    return pl.pallas_call(
        _hyconv_kernel,
        out_shape=[jax.ShapeDtypeStruct((B, N2, L // N2, D_B), BF16), jax.ShapeDtypeStruct((B, L, D_B), BF16)],
        grid=(B, L // tm),
        in_specs=_halo_specs(tm, L, C) + [_const_spec((3, C)), _const_spec((1, C)), _const_spec((2 * BD, BD))],
        out_specs=[pl.BlockSpec((1, N2, tm // N2, D_B), lambda b, i: (b, 0, i, 0)),
                   pl.BlockSpec((1, tm, D_B), lambda b, i: (b, i, 0))],
        scratch_shapes=[pltpu.VMEM((D_B // LANES, tm, LANES), F32)],
        compiler_params=_cp(("parallel", "parallel")),
        name="hyconv",
    )(hy, hy, hy, cw, cb.reshape(1, C), _shift_matrix())


def _filt_kernel(w1_ref, b1_ref, w2_ref, b2_ref, w3_ref, b3_ref, w4_ref, fr_ref, o_ref, s_ref, *, L):
    i = pl.program_id(0)
    T = s_ref.shape[1]
    pos = (lax.broadcasted_iota(jnp.int32, (T, 128), 0) + i * T).astype(F32)
    lane = lax.broadcasted_iota(jnp.int32, (T, 128), 1)
    t = pos / (L - 1.0)
    w = (2.0 * math.pi) * pos / L
    band = jnp.where(lane <= FILTER_BANDS, lane - 1, lane - 1 - FILTER_BANDS).astype(F32)
    f = 1e-4 + band * ((FILTER_BANDS - 1 - 1e-4) / (FILTER_BANDS - 1))
    z = jnp.where(lane == 0, t,
                  jnp.where(lane <= FILTER_BANDS, jnp.cos(f * w),
                            jnp.where(lane <= 2 * FILTER_BANDS, -jnp.sin(f * w), 0.0)))
    fr = fr_ref[...]
    a = jnp.sin(fr * (jnp.dot(z, w1_ref[...], preferred_element_type=F32, precision=HIGHEST) + b1_ref[...]))
    a = jnp.sin(fr * (jnp.dot(a, w2_ref[...], preferred_element_type=F32, precision=HIGHEST) + b2_ref[...]))
    a = jnp.sin(fr * (jnp.dot(a, w3_ref[...], preferred_element_type=F32, precision=HIGHEST) + b3_ref[...]))
    hf = jnp.dot(a, w4_ref[...], preferred_element_type=F32, precision=HIGHEST)
    ch = lax.broadcasted_iota(jnp.int32, (1, D_B), 1).astype(F32)
    deltas = jnp.abs(MIN_DECAY + ch * ((MAX_DECAY - MIN_DECAY) / (D_B - 1)))
    window = jnp.exp(-t[:, 0:1] * deltas) + FILTER_SHIFT
    _store_permuted(o_ref, (0,), hf[:, 0:D_B] * window, s_ref)
    _store_permuted(o_ref, (1,), jnp.where(pos[:, 0:1] == 0.0, 0.0, hf[:, D_B:2 * D_B] * window), s_ref)


def _filters(L, w1, b1, w2, b2, w3, b3, w4, freq, *, tm):
    emb = w1.shape[0]
    w1p = jnp.zeros((128, FILTER_HIDDEN), F32).at[0:emb].set(w1)
    row = lambda a: a.reshape(1, -1)
    return pl.pallas_call(
        functools.partial(_filt_kernel, L=L),
        out_shape=jax.ShapeDtypeStruct((2, N2, L // N2, D_B), BF16),
        grid=(L // tm,),
        in_specs=[_const_spec((128, FILTER_HIDDEN)), _const_spec((1, FILTER_HIDDEN)),
                  _const_spec((FILTER_HIDDEN, FILTER_HIDDEN)), _const_spec((1, FILTER_HIDDEN)),
                  _const_spec((FILTER_HIDDEN, FILTER_HIDDEN)), _const_spec((1, FILTER_HIDDEN)),
                  _const_spec((FILTER_HIDDEN, 2 * D_B)), _const_spec((1, FILTER_HIDDEN))],
        out_specs=pl.BlockSpec((2, N2, tm // N2, D_B), lambda i: (0, 0, i, 0)),
        scratch_shapes=[pltpu.VMEM((D_B // LANES, tm, LANES), F32)],
        compiler_params=_cp(("parallel",)),
        name="filt",
    )(w1p, row(b1), w2, row(b2), w3, row(b3), w4, row(freq))


def _dft_tables():
    half = N1 // 2
    g, ri, kl = np.meshgrid(np.arange(NG), np.arange(2), np.arange(KPG), indexing="ij")
    k1 = (KPG * g + kl).reshape(-1).astype(np.float64)
    is_im = ri.reshape(-1).astype(bool)
    valid = k1 <= half
    n1 = np.arange(half, dtype=np.float64)
    th = 2.0 * np.pi * np.outer(k1, n1) / N1
    f1 = np.where(is_im[:, None], -np.sin(th), np.cos(th)) * valid[:, None]
    ck = np.where((k1 == 0) | (k1 == half), 1.0, 2.0) * valid
    g3 = (np.where(is_im[:, None], -np.sin(th), np.cos(th)) * ck[:, None] / NFFT).T
    mf = np.zeros((NG, BD, BD))
    mb = np.zeros((NG, BD, BD))
    a = np.arange(N2, dtype=np.float64)
    for gi in range(NG):
        for k in range(KPG):
            kk = KPG * gi + k
            if kk > half:
                continue
            phi = 2.0 * np.pi * (np.outer(a, a) / N2 + np.outer(np.ones(N2), a) * kk / NFFT)
            cr, ci = np.cos(phi), -np.sin(phi)
            r0, r1 = k * N2, BD // 2 + k * N2
            mf[gi, r0:r0 + N2, r0:r0 + N2] = cr
            mf[gi, r0:r0 + N2, r1:r1 + N2] = -ci
            mf[gi, r1:r1 + N2, r0:r0 + N2] = ci
            mf[gi, r1:r1 + N2, r1:r1 + N2] = cr
            br, bi = np.cos(phi).T, np.sin(phi).T
            mb[gi, r0:r0 + N2, r0:r0 + N2] = br
            mb[gi, r0:r0 + N2, r1:r1 + N2] = -bi
            mb[gi, r1:r1 + N2, r0:r0 + N2] = bi
            mb[gi, r1:r1 + N2, r1:r1 + N2] = br
    f = lambda t: jnp.asarray(t, dtype=F32)
    return f(f1), f(g3), f(mf), f(mb)


def _stage1(sig, f1_ref, z_s):
    for j in range(N2):
        zj = jnp.dot(f1_ref[...], sig(j), preferred_element_type=F32)
        for c in range(z_s.shape[0]):
            z_s[c, pl.ds(j, F1R, stride=N2), :] = zj[:, c * LANES:(c + 1) * LANES]


def _group_rows(z_s, r0):
    return jnp.concatenate([z_s[c, pl.ds(r0, BD), :] for c in range(z_s.shape[0])], axis=1)


def _fspec_kernel(hp_ref, f1_ref, mf_ref, kf_ref, z_s):
    half = BD // 2
    for s in range(2):
        _stage1(lambda j: hp_ref[s, j], f1_ref, z_s)

        def body(g, carry):
            r0 = pl.multiple_of(g * BD, BD)
            x = jnp.dot(mf_ref[g], _group_rows(z_s, r0).astype(BF16), preferred_element_type=F32)
            if s == 0:
                kf_ref[pl.ds(r0, BD), :] = x
            else:
                kf_ref[pl.ds(r0, half), :] += x[0:half]
                kf_ref[pl.ds(r0 + half, half), :] -= x[half:BD]
            return carry

        lax.fori_loop(0, NG, body, 0, unroll=GROUP_UNROLL)


def _fspec(hp, f1b, mfb):
    C = hp.shape[-1]
    half = N1 // 2
    return pl.pallas_call(
        _fspec_kernel,
        out_shape=jax.ShapeDtypeStruct((ZR, C), F32),
        grid=(C // CT,),
        in_specs=[pl.BlockSpec((2, N2, half, CT), lambda t: (0, 0, 0, t)),
                  _const_spec((F1R, half)), _const_spec((NG, BD, BD))],
        out_specs=pl.BlockSpec((ZR, CT), lambda t: (0, t)),
        scratch_shapes=[pltpu.VMEM((CT // LANES, ZR, LANES), F32)],
        compiler_params=_cp(("parallel",)),
        name="fspec",
    )(hp, f1b, mfb)


def _lconv_kernel(up_ref, x0_ref, kf_ref, f1_ref, mf_ref, mb_ref, g3_ref, ds_ref, o_ref, z_s, y_s):
    half = BD // 2
    nsl = z_s.shape[0]
    _stage1(lambda j: up_ref[0, j], f1_ref, z_s)

    def body(g, carry):
        r0 = pl.multiple_of(g * BD, BD)
        x = jnp.dot(mf_ref[g], _group_rows(z_s, r0).astype(BF16), preferred_element_type=F32)
        kf = kf_ref[pl.ds(r0, BD), :]
        xr, xi = x[0:half], x[half:BD]
        kr, ki = kf[0:half], kf[half:BD]
        y = jnp.concatenate([xr * kr - xi * ki, xr * ki + xi * kr], axis=0)
        v = jnp.dot(mb_ref[g], y.astype(BF16), preferred_element_type=F32)
        for c in range(nsl):
            z_s[c, pl.ds(r0, BD), :] = v[:, c * LANES:(c + 1) * LANES]
        return carry

    lax.fori_loop(0, NG, body, 0, unroll=GROUP_UNROLL)

    for j in range(N2):
        vj = jnp.concatenate([z_s[c, pl.ds(j, F1R, stride=N2), :] for c in range(nsl)], axis=1)
        yj = (jnp.dot(g3_ref[...], vj.astype(BF16), preferred_element_type=F32)
              + up_ref[0, j].astype(F32) * ds_ref[...])
        for c in range(nsl):
            y_s[c, pl.ds(j, N1 // 2, stride=N2), :] = yj[:, c * LANES:(c + 1) * LANES]
    for c in range(nsl):
        sl = slice(c * LANES, (c + 1) * LANES)
        o_ref[0, :, sl] = (x0_ref[0, :, sl].astype(F32) * y_s[c]).astype(o_ref.dtype)


def _lconv(up, x0, kf, f1b, mfb, mbb, g3b, dskip):
    B, L, C = x0.shape
    half = N1 // 2
    return pl.pallas_call(
        _lconv_kernel,
        out_shape=jax.ShapeDtypeStruct((B, L, C), BF16),
        grid=(C // CT, B),
        in_specs=[pl.BlockSpec((1, N2, half, CT), lambda t, b: (b, 0, 0, t)),
                  pl.BlockSpec((1, L, CT), lambda t, b: (b, 0, t)),
                  pl.BlockSpec((ZR, CT), lambda t, b: (0, t), pipeline_mode=pl.Buffered(1)),
                  _const_spec((F1R, half)), _const_spec((NG, BD, BD)), _const_spec((NG, BD, BD)),
                  _const_spec((half, F1R)),
                  pl.BlockSpec((1, CT), lambda t, b: (0, t))],
        out_specs=pl.BlockSpec((1, L, CT), lambda t, b: (b, 0, t)),
        scratch_shapes=[pltpu.VMEM((CT // LANES, ZR, LANES), F32), pltpu.VMEM((CT // LANES, L, LANES), F32)],
        compiler_params=_cp(("parallel", "parallel")),
        name="lconv",
    )(up, x0, kf, f1b, mfb, mbb, g3b, dskip.reshape(1, C))


def _merge_kernel(hf_ref, hb_ref, xc_ref, z_ref, yb_ref, gab_ref, x_ref, mod_ref, ng_ref, sk_ref,
                  wpa_ref, wpb_ref, wo_ref, o_ref):
    h = hf_ref[0].astype(F32) + hb_ref[0].astype(F32)
    xc = xc_ref[0].astype(F32)
    parts = []
    for k in range(A_HEADS):
        sl = slice(k * DH, (k + 1) * DH)
        hh = h[:, sl]
        hn = hh * lax.rsqrt(jnp.mean(hh * hh, axis=-1, keepdims=True) + EPS)
        parts.append(hn * ng_ref[:, sl] + sk_ref[:, sl] * xc[:, sl])
    ya = jax.nn.sigmoid(z_ref[0].astype(F32)) * jnp.concatenate(parts, axis=1)
    gab = gab_ref[0].astype(F32)
    mix = (jax.nn.sigmoid(gab[:, 0:D_MODEL]) * jnp.dot(ya.astype(BF16), wpa_ref[...], preferred_element_type=F32)
           + jax.nn.sigmoid(gab[:, D_MODEL:]) * jnp.dot(yb_ref[0], wpb_ref[...], preferred_element_type=F32))
    out = jnp.dot(mix.astype(BF16), wo_ref[...], preferred_element_type=F32)
    o_ref[0] = x_ref[0] + mod_ref[0, 5:6, :] * out


def _merge(hf, hb, xc, z, yb, gab, x, mods, a_norm_g, a_skip, wpa, wpb, wo, *, tm):
    B, L, D = x.shape
    tok = lambda w: pl.BlockSpec((1, tm, w), lambda b, i: (b, i, 0))
    return pl.pallas_call(
        _merge_kernel,
        out_shape=jax.ShapeDtypeStruct((B, L, D), F32),
        grid=(B, L // tm),
        in_specs=[tok(D_A), tok(D_A),
                  tok(D_A), tok(D_A), tok(D_B), tok(2 * D_MODEL), tok(D),
                  pl.BlockSpec((1, N_MOD, D), lambda b, i: (b, 0, 0)),
                  _const_spec((1, D_A)), _const_spec((1, D_A)),
                  _const_spec((D_A, D)), _const_spec((D_B, D)), _const_spec((D, D))],
        out_specs=tok(D),
        compiler_params=_cp(("parallel", "parallel")),
        name="merge",
    )(hf, hb, xc, z, yb, gab, x, mods, a_norm_g.reshape(1, D_A), a_skip.reshape(1, D_A), wpa, wpb, wo)


def kernel(x, c, ctx, c_ctx, w_ada, b_ada, norm_g, ffn1_up, ffn1_down, ffn2_up, ffn2_down, w_in, a_conv_w, a_conv_b, a_wq, a_wk, a_wv, a_w_gate, a_b_gate, a_norm_g, a_skip, b_conv_w, b_conv_b, b_filt_w1, b_filt_b1, b_filt_w2, b_filt_b2, b_filt_w3, b_filt_b3, b_filt_w4, b_filt_freq, b_skip, w_pa, w_pb, w_out, final_g):
    B, L, D = x.shape
    Lc = ctx.shape[1]
    assert w_ada.shape[0] == 1, "single-layer stack"
    assert 2 * L == NFFT and D == D_MODEL and L // GRID_W == GRID_W
    TM = 512

    c8 = jnp.zeros((8, D), F32).at[0:B].set(c).at[B].set(c_ctx)
    mods = _mods(c8, w_ada[0], b_ada[0]).reshape(8, N_MOD, D)
    ng = norm_g[0]
    row_b = lambda b: b
    row_ctx = lambda b: B

    up1 = ffn1_up[0].astype(BF16)
    dn1 = ffn1_down[0].astype(BF16)
    up2 = ffn2_up[0].astype(BF16)
    dn2 = ffn2_down[0].astype(BF16)
    w_in_b = w_in[0].astype(BF16)

    x1 = _ffn(x, mods, row_b, ng, up1, dn1, sub=0, tm=TM, pos=_pos_table())
    ctx1 = _ffn(ctx, mods, row_ctx, ng, up1, dn1, sub=0, tm=Lc)

    widths = (D_A, D_A, 3 * D_B, 2 * D_MODEL)
    xm, z, hy, gab = _proj(x1, mods, row_b, ng, w_in_b, widths, tm=256)
    (xm_c,) = _proj(ctx1, mods, row_ctx, ng, w_in_b, (D_A,), tm=Lc)

    bq, bkt, bv, gqk, gv = _qkvw(a_wq[0], a_wk[0], a_wv[0], a_w_gate[0])
    fw = (a_conv_w[0], a_conv_b[0], bq, bkt, bv, gqk, gv, a_b_gate[0])
    q, xc, kt, v, g, gl = _feat(xm, *fw, tm=TM)
    kt_c, v_c, g_c, gl_c = _feat(xm_c, *fw, tm=Lc, with_q=False)

    c0, n0, m0 = _ctxstate(kt_c, v_c, g_c, gl_c)
    hf, hb = _scan(q, kt, v, g, gl, c0, n0, m0)

    up, x0c = _hyconv(hy, b_conv_w[0], b_conv_b[0], tm=TM)
    f1b, g3b, mfb, mbb = (a.astype(BF16) for a in _dft_tables())
    hp = _filters(L, b_filt_w1[0], b_filt_b1[0], b_filt_w2[0], b_filt_b2[0], b_filt_w3[0], b_filt_b3[0],
                  b_filt_w4[0], b_filt_freq[0], tm=TM)
    kf = _fspec(hp, f1b, mfb)
    yb = _lconv(up, x0c, kf, f1b, mfb, mbb, g3b, b_skip[0])

    x2 = _merge(hf, hb, xc, z, yb, gab, x1, mods, a_norm_g[0], a_skip[0],
                w_pa[0].astype(BF16), w_pb[0].astype(BF16), w_out[0].astype(BF16), tm=TM)

    return _ffn(x2, mods, row_b, ng, up2, dn2, sub=2, tm=TM, final_g=final_g)
```

```python
import functools
import math

import numpy as np
import jax
import jax.numpy as jnp
from jax import lax
from jax.experimental import pallas as pl
from jax.experimental.pallas import tpu as pltpu

F32 = jnp.float32
BF16 = jnp.bfloat16

D_MODEL = 1024
D_A = 2048
A_HEADS = 4
DH = D_A // A_HEADS
QKV_BLOCK = 4
D_B = 1024
D_FF = 2816
EPS = 1e-6
N_MOD = 9
GRID_W = 64
CHUNK = 256
HEADS_PER_STEP = 2
FILTER_BANDS = 16
FILTER_HIDDEN = 64
DECAY_TARGET = 1e-2
MAX_DECAY = math.log(DECAY_TARGET) / 0.3
MIN_DECAY = math.log(DECAY_TARGET) / 1.5
FILTER_SHIFT = 0.05

LANES = 128
SUBLANES = 8
BD = 256

N2 = SUBLANES
NFFT = 8192
N1 = NFFT // N2
KPG = BD // (2 * N2)
NG = (N1 // 2) // KPG + 1
F1R = NG * 2 * KPG
ZR = F1R * N2
CT = 256
GROUP_UNROLL = 11

VMEM_LIMIT = 56 * 1024 * 1024
HIGHEST = lax.Precision.HIGHEST
NEG = -1e30


def _cp(sem):
    return pltpu.CompilerParams(dimension_semantics=sem, vmem_limit_bytes=VMEM_LIMIT)


def _const_spec(shape):
    nd = len(shape)
    return pl.BlockSpec(shape, lambda *_: (0,) * nd, pipeline_mode=pl.Buffered(1))


def _silu(x):
    return x * jax.nn.sigmoid(x)


def _rms(x, g):
    return x * lax.rsqrt(jnp.mean(x * x, axis=-1, keepdims=True) + EPS) * g


def _mods_kernel(c_ref, w_ref, b_ref, o_ref):
    cs = _silu(c_ref[...])
    o_ref[...] = jnp.dot(cs, w_ref[...], preferred_element_type=F32, precision=HIGHEST) + b_ref[...]


def _mods(c8, w_ada, b_ada):
    n = w_ada.shape[1]
    tn = 1024
    return pl.pallas_call(
        _mods_kernel,
        out_shape=jax.ShapeDtypeStruct((8, n), F32),
        grid=(n // tn,),
        in_specs=[pl.BlockSpec((8, D_MODEL), lambda j: (0, 0)),
                  pl.BlockSpec((D_MODEL, tn), lambda j: (0, j)),
                  pl.BlockSpec((1, tn), lambda j: (0, j))],
        out_specs=pl.BlockSpec((8, tn), lambda j: (0, j)),
        compiler_params=_cp(("parallel",)),
        name="mods",
    )(c8, w_ada, b_ada.reshape(1, n))


def _postab_kernel(om_ref, o_ref):
    nf = om_ref.shape[1]
    p = lax.broadcasted_iota(jnp.int32, (GRID_W, nf), 0).astype(F32)
    a = p * om_ref[...]
    o_ref[:, 0:nf] = jnp.sin(a)
    o_ref[:, nf:2 * nf] = jnp.cos(a)


def _pos_table():
    nf = D_MODEL // 4
    omega = 1.0 / (10000.0 ** (jnp.arange(nf, dtype=F32) / nf))
    return pl.pallas_call(
        _postab_kernel,
        out_shape=jax.ShapeDtypeStruct((GRID_W, 2 * nf), F32),
        name="postab",
    )(omega.reshape(1, nf))


FF_CHUNKS = ((0, 6 * BD), (6 * BD, D_FF))


def _ffn_kernel(*refs, sub, has_pos, final):
    it = iter(refs)
    x_ref = next(it)
    pos_ref = next(it) if has_pos else None
    mod_ref = next(it)
    g_ref = next(it)
    up_ref = next(it)
    dn_ref = next(it)
    fg_ref = next(it) if final else None
    o_ref = next(it)

    x = x_ref[0]
    if has_pos:
        tm = x.shape[0]
        per = tm // GRID_W
        half = pos_ref.shape[1]
        i = pl.program_id(1)
        ecol = pos_ref[...]
        rows = []
        for q in range(per):
            erow = jnp.broadcast_to(pos_ref[pl.ds(i * per + q, 1), :], (GRID_W, half))
            rows.append(jnp.concatenate([erow, ecol], axis=1))
        x = x + jnp.concatenate(rows, axis=0)
    shift = mod_ref[0, 3 * sub:3 * sub + 1, :]
    scale = mod_ref[0, 3 * sub + 1:3 * sub + 2, :]
    gate = mod_ref[0, 3 * sub + 2:3 * sub + 3, :]
    h = _rms(x, g_ref[sub:sub + 1, :]) * (1.0 + scale) + shift
    hb = h.astype(BF16)
    acc = jnp.zeros(x.shape, F32)
    for lo, hi in FF_CHUNKS:
        gg = jnp.dot(hb, up_ref[:, lo:hi], preferred_element_type=F32)
        uu = jnp.dot(hb, up_ref[:, D_FF + lo:D_FF + hi], preferred_element_type=F32)
        a = (_silu(gg) * uu).astype(BF16)
        acc = acc + jnp.dot(a, dn_ref[lo:hi, :], preferred_element_type=F32)
    y = x + 0.5 * gate * acc
    if final:
        y = _rms(y, fg_ref[...])
    o_ref[0] = y


def _ffn(x, mods, mod_row, norm_g, up_b, dn_b, *, sub, tm, pos=None, final_g=None):
    B, L, D = x.shape
    args = [x]
    specs = [pl.BlockSpec((1, tm, D), lambda b, i: (b, i, 0))]
    if pos is not None:
        args.append(pos)
        specs.append(_const_spec(pos.shape))
    args += [mods, norm_g, up_b, dn_b]
    specs += [pl.BlockSpec((1, N_MOD, D), lambda b, i: (mod_row(b), 0, 0)),
              _const_spec((3, D)), _const_spec((D, 2 * D_FF)), _const_spec((D_FF, D))]
    if final_g is not None:
        args.append(final_g.reshape(1, D))
        specs.append(_const_spec((1, D)))
    return pl.pallas_call(
        functools.partial(_ffn_kernel, sub=sub, has_pos=pos is not None, final=final_g is not None),
        out_shape=jax.ShapeDtypeStruct((B, L, D), F32),
        grid=(B, L // tm),
        in_specs=specs,
        out_specs=pl.BlockSpec((1, tm, D), lambda b, i: (b, i, 0)),
        compiler_params=_cp(("parallel", "parallel")),
        name=f"ffn{sub}",
    )(*args)


def _proj_kernel(x_ref, mod_ref, g_ref, w_ref, *o_refs, widths):
    x = x_ref[0]
    shift = mod_ref[0, 3:4, :]
    scale = mod_ref[0, 4:5, :]
    hb = (_rms(x, g_ref[1:2, :]) * (1.0 + scale) + shift).astype(BF16)
    lo = 0
    for o_ref, w in zip(o_refs, widths):
        o_ref[0] = jnp.dot(hb, w_ref[:, lo:lo + w], preferred_element_type=F32).astype(o_ref.dtype)
        lo += w


def _proj(x, mods, mod_row, norm_g, w_b, widths, *, tm):
    B, L, D = x.shape
    n = sum(widths)
    return pl.pallas_call(
        functools.partial(_proj_kernel, widths=widths),
        out_shape=[jax.ShapeDtypeStruct((B, L, w), BF16) for w in widths],
        grid=(B, L // tm),
        in_specs=[pl.BlockSpec((1, tm, D), lambda b, i: (b, i, 0)),
                  pl.BlockSpec((1, N_MOD, D), lambda b, i: (mod_row(b), 0, 0)),
                  _const_spec((3, D)), _const_spec((D, n))],
        out_specs=[pl.BlockSpec((1, tm, w), lambda b, i: (b, i, 0)) for w in widths],
        compiler_params=_cp(("parallel", "parallel")),
        name="proj",
    )(x, mods, norm_g, w_b)


HALO = 16


def _shift_matrix():
    return jnp.concatenate([jnp.eye(BD, k=-1, dtype=BF16), jnp.eye(BD, k=1, dtype=BF16)], axis=0)


def _conv3(xb, prev_row, next_row, w_ref, b_ref, s_ref):
    T, C = xb.shape
    w0, w1, w2, b = w_ref[0:1, :], w_ref[1:2, :], w_ref[2:3, :], b_ref[...]
    x = xb.astype(F32)
    r8 = lax.broadcasted_iota(jnp.int32, (SUBLANES, C), 0)
    out = []
    for r0 in range(0, T, BD):
        r1 = r0 + BD
        sh = jnp.dot(s_ref[...], xb[r0:r1], preferred_element_type=F32)
        y = sh[0:BD] * w0 + x[r0:r1] * w1 + sh[BD:2 * BD] * w2 + b
        before = prev_row if r0 == 0 else x[r0 - 1:r0]
        after = next_row if r1 == T else x[r1:r1 + 1]
        first = before * w0 + x[r0:r0 + 1] * w1 + x[r0 + 1:r0 + 2] * w2 + b
        last = x[r1 - 2:r1 - 1] * w0 + x[r1 - 1:r1] * w1 + after * w2 + b
        top = jnp.where(r8 == 0, first, y[0:SUBLANES])
        bot = jnp.where(r8 == SUBLANES - 1, last, y[BD - SUBLANES:BD])
        out += [top, y[SUBLANES:BD - SUBLANES], bot]
    return jnp.concatenate(out, axis=0)


def _halo_rows(prev_ref, next_ref):
    i = pl.program_id(1)
    n = pl.num_programs(1)
    p = prev_ref[0, HALO - 1:HALO, :].astype(F32)
    q = next_ref[0, 0:1, :].astype(F32)
    p = jnp.where(i == 0, 0.0, p)
    q = jnp.where(i == n - 1, 0.0, q)
    return p, q


def _halo_specs(tm, L, C):
    r = tm // HALO
    nb = L // HALO
    return [pl.BlockSpec((1, HALO, C), lambda b, i: (b, jnp.maximum(i * r - 1, 0), 0)),
            pl.BlockSpec((1, tm, C), lambda b, i: (b, i, 0)),
            pl.BlockSpec((1, HALO, C), lambda b, i: (b, jnp.minimum((i + 1) * r, nb - 1), 0))]


def _qkvw_kernel(wq_ref, wk_ref, wv_ref, wg_ref, bq_ref, bkt_ref, bv_ref, gqk_ref, gv_ref):
    r = lax.broadcasted_iota(jnp.int32, (BD, BD), 0)
    c = lax.broadcasted_iota(jnp.int32, (BD, BD), 1)
    shift = QKV_BLOCK.bit_length() - 1
    same_block = (r >> shift) == (c >> shift)
    col_in_block = c & (QKV_BLOCK - 1)

    def tile(w_ref):
        w2 = w_ref[0]
        a = jnp.zeros((BD, BD), F32)
        for j in range(QKV_BLOCK):
            a = a + jnp.where(col_in_block == j, w2[:, j:j + 1], 0.0)
        return jnp.where(same_block, a, 0.0)

    tq = tile(wq_ref)
    tk = tile(wk_ref)
    tv = tile(wv_ref)
    bq_ref[0] = tq.astype(BF16)
    bkt_ref[0] = tk.T.astype(BF16)
    bv_ref[0] = tv.astype(BF16)
    gqk_ref[...] = (jnp.dot(tq, wg_ref[0], preferred_element_type=F32, precision=HIGHEST)
                    + jnp.dot(tk, wg_ref[1], preferred_element_type=F32, precision=HIGHEST)).astype(BF16)
    gv_ref[...] = jnp.dot(tv, wg_ref[2], preferred_element_type=F32, precision=HIGHEST).astype(BF16)


def _qkvw(wq, wk, wv, w_gate):
    nb = D_A // BD
    w2 = lambda w: w.reshape(nb, BD, QKV_BLOCK)
    wspec = pl.BlockSpec((1, BD, QKV_BLOCK), lambda j: (j, 0, 0))
    tspec = pl.BlockSpec((1, BD, BD), lambda j: (j, 0, 0))
    gspec = pl.BlockSpec((BD, 16), lambda j: (j, 0))
    tile = jax.ShapeDtypeStruct((nb, BD, BD), BF16)
    gw = jax.ShapeDtypeStruct((D_A, 16), BF16)
    return pl.pallas_call(
        _qkvw_kernel,
        out_shape=[tile, tile, tile, gw, gw],
        grid=(nb,),
        in_specs=[wspec, wspec, wspec, pl.BlockSpec((3, BD, 16), lambda j: (0, j, 0))],
        out_specs=[tspec, tspec, tspec, gspec, gspec],
        compiler_params=_cp(("parallel",)),
        name="qkvw",
    )(w2(wq), w2(wk), w2(wv), w_gate.reshape(3, D_A, 16))


def _feat_kernel(prev_ref, xm_ref, next_ref, cw_ref, cb_ref, sh_ref, bq_ref, bkt_ref, bv_ref, gqk_ref, gv_ref,
                 gb_ref, *o_refs, with_q):
    if with_q:
        q_ref, xc_ref, kt_ref, v_ref, g_ref, gl_ref = o_refs
    else:
        kt_ref, v_ref, g_ref, gl_ref = o_refs
    xmb = xm_ref[0]
    p, n = _halo_rows(prev_ref, next_ref)
    xc = _silu(_conv3(xmb, p, n, cw_ref, cb_ref, sh_ref))
    xcb = xc.astype(BF16)
    if with_q:
        xc_ref[0] = xcb
    g = (jnp.dot(xcb, gqk_ref[...], preferred_element_type=F32)
         + jnp.dot(xmb, gv_ref[...], preferred_element_type=F32) + gb_ref[...])
    g_ref[0] = g
    gl_ref[0] = g.T
    for j in range(D_A // BD):
        sl = slice(j * BD, (j + 1) * BD)
        if with_q:
            q = jnp.dot(xcb[:, sl], bq_ref[j], preferred_element_type=F32)
            q_ref[0, :, sl] = (q * (DH ** -0.5)).astype(BF16)
        kt_ref[0, sl, :] = lax.dot_general(bkt_ref[j], xcb[:, sl], (((1,), (1,)), ((), ())),
                                          preferred_element_type=F32).astype(BF16)
        v_ref[0, :, sl] = jnp.dot(xmb[:, sl], bv_ref[j], preferred_element_type=F32).astype(BF16)


def _feat(xm, cw, cb, bq, bkt, bv, gqk, gv, gb, *, tm, with_q=True):
    B, L, _ = xm.shape
    nb = D_A // BD
    tok = jax.ShapeDtypeStruct((B, L, D_A), BF16)
    tok_spec = pl.BlockSpec((1, tm, D_A), lambda b, i: (b, i, 0))
    shapes = [jax.ShapeDtypeStruct((B, D_A, L), BF16), tok, jax.ShapeDtypeStruct((B, L, 16), F32),
              jax.ShapeDtypeStruct((B, 16, L), F32)]
    specs = [pl.BlockSpec((1, D_A, tm), lambda b, i: (b, 0, i)), tok_spec,
             pl.BlockSpec((1, tm, 16), lambda b, i: (b, i, 0)), pl.BlockSpec((1, 16, tm), lambda b, i: (b, 0, i))]
    if with_q:
        shapes = [tok, tok] + shapes
        specs = [tok_spec, tok_spec] + specs
    return pl.pallas_call(
        functools.partial(_feat_kernel, with_q=with_q),
        out_shape=shapes,
        grid=(B, L // tm),
        in_specs=_halo_specs(tm, L, D_A) + [
            _const_spec((3, D_A)), _const_spec((1, D_A)), _const_spec((2 * BD, BD)),
            _const_spec((nb, BD, BD)), _const_spec((nb, BD, BD)), _const_spec((nb, BD, BD)),
            _const_spec((D_A, 16)), _const_spec((D_A, 16)), _const_spec((1, 16))],
        out_specs=specs,
        compiler_params=_cp(("parallel", "parallel")),
        name="feat" if with_q else "feat_ctx",
    )(xm, xm, xm, cw, cb.reshape(1, D_A), _shift_matrix(), bq, bkt, bv, gqk, gv, gb.reshape(1, 16))


def _gate_vectors(gt, gl, idx, sign):
    T = gt.shape[0]
    sub = lax.broadcasted_iota(jnp.int32, gl.shape, 0)
    lane = lax.broadcasted_iota(jnp.int32, gt.shape, 1)
    ig_row = jnp.sum(jnp.where(sub == idx, gl, 0.0), axis=0, keepdims=True)
    fg_row = jnp.sum(jnp.where(sub == idx + A_HEADS, gl, 0.0), axis=0, keepdims=True)
    fg_col = jnp.sum(jnp.where(lane == idx + A_HEADS, gt, 0.0), axis=1, keepdims=True)
    lf_row = jax.nn.log_sigmoid(fg_row)
    lf_col = jax.nn.log_sigmoid(fg_col)
    r = lax.broadcasted_iota(jnp.int32, (T, T), 0)
    c = lax.broadcasted_iota(jnp.int32, (T, T), 1)
    mask = sign * (r - c) >= 0
    mask_t = sign * (c - r) >= 0
    b_col = jnp.sum(jnp.where(mask, lf_row, 0.0), axis=1, keepdims=True)
    b_row = jnp.sum(jnp.where(mask_t, lf_col, 0.0), axis=0, keepdims=True)
    b_last = jnp.sum(lf_row, axis=1, keepdims=True)
    return ig_row, b_col, b_row, b_last, mask


def _state_step(kt, v, ig_row, b_row, b_last, m_prev):
    log_w = b_last - b_row + ig_row
    m_new = jnp.maximum(b_last + m_prev, jnp.max(log_w, axis=1, keepdims=True))
    decay = jnp.exp(b_last + m_prev - m_new)
    w = jnp.exp(log_w - m_new)
    kw = kt.astype(F32) * w
    dC = jnp.dot(kw.astype(BF16), v, preferred_element_type=F32)
    dn = jnp.sum(kw, axis=1, keepdims=True)
    return decay, m_new, dC, dn


def _ctxstate_kernel(kt_ref, v_ref, gt_ref, gl_ref, c_ref, n_ref, m_ref):
    d = pl.program_id(0)
    h = pl.program_id(2)
    sign = 1 - 2 * d
    ig_row, _, b_row, b_last, _ = _gate_vectors(gt_ref[0], gl_ref[0], 8 * d + h, sign)
    m0 = jnp.zeros((1, 1), F32)
    _, m_new, dC, dn = _state_step(kt_ref[0], v_ref[0], ig_row, b_row, b_last, m0)
    c_ref[0, 0, 0] = dC
    n_ref[0, 0, 0] = jnp.broadcast_to(dn, (DH, 128))
    m_ref[0, 0, 0] = jnp.broadcast_to(m_new, (8, 128))


def _ctxstate(kt, v, gt, gl):
    B, _, Lc = kt.shape
    H = A_HEADS
    return pl.pallas_call(
        _ctxstate_kernel,
        out_shape=[jax.ShapeDtypeStruct((2, B, H, DH, DH), F32),
                   jax.ShapeDtypeStruct((2, B, H, DH, 128), F32),
                   jax.ShapeDtypeStruct((2, B, H, 8, 128), F32)],
        grid=(2, B, H),
        in_specs=[pl.BlockSpec((1, DH, Lc), lambda d, b, h: (b, h, 0)),
                  pl.BlockSpec((1, Lc, DH), lambda d, b, h: (b, 0, h)),
                  pl.BlockSpec((1, Lc, 16), lambda d, b, h: (b, 0, 0)),
                  pl.BlockSpec((1, 16, Lc), lambda d, b, h: (b, 0, 0))],
        out_specs=[pl.BlockSpec((1, 1, 1, DH, DH), lambda d, b, h: (d, b, h, 0, 0)),
                   pl.BlockSpec((1, 1, 1, DH, 128), lambda d, b, h: (d, b, h, 0, 0)),
                   pl.BlockSpec((1, 1, 1, 8, 128), lambda d, b, h: (d, b, h, 0, 0))],
        compiler_params=_cp(("parallel", "parallel", "parallel")),
        name="ctxstate",
    )(kt, v, gt, gl)


def _scan_kernel(qf_ref, ktf_ref, vf_ref, gtf_ref, glf_ref, qb_ref, ktb_ref, vb_ref, gtb_ref, glb_ref,
                 c0_ref, n0_ref, m0_ref, hf_ref, hb_ref, *scratch):
    hp = pl.program_id(1)
    j = pl.program_id(2)
    io = ((qf_ref, ktf_ref, vf_ref, gtf_ref, glf_ref, hf_ref), (qb_ref, ktb_ref, vb_ref, gtb_ref, glb_ref, hb_ref))
    chains = [(d, hh) + io[d] + tuple(scratch[5 * (d * HEADS_PER_STEP + hh):5 * (d * HEADS_PER_STEP + hh) + 5])
              for d in range(2) for hh in range(HEADS_PER_STEP)]

    @pl.when(j == 0)
    def _():
        for d, hh, _, _, _, _, _, _, c_s, cq_s, n_s, nq_s, m_s in chains:
            c0 = c0_ref[d, 0, hh]
            c_s[...] = c0
            cq_s[...] = c0.astype(BF16)
            n0 = n0_ref[d, 0, hh]
            n_s[...] = n0
            nq_s[...] = n0.astype(BF16)
            m_s[...] = m0_ref[d, 0, hh]

    for d, hh, q_ref, kt_ref, v_ref, gt_ref, gl_ref, h_ref, c_s, cq_s, n_s, nq_s, m_s in chains:
        sign = 1 - 2 * d
        hsl = slice(hh * DH, (hh + 1) * DH)
        q = q_ref[0, :, hsl]
        kt = kt_ref[0, hsl, :]
        v = v_ref[0, :, hsl]
        head = hp * HEADS_PER_STEP + hh
        ig_row, b_col, b_row, b_last, mask = _gate_vectors(gt_ref[0], gl_ref[0], 8 * d + head, sign)
        m_prev = m_s[0:1, 0:1]

        log_d = jnp.where(mask, b_col - b_row + ig_row, NEG)
        m_inter = b_col + m_prev
        m_t = jnp.maximum(m_inter, jnp.max(log_d, axis=1, keepdims=True))
        dmat = jnp.exp(log_d - m_t)
        inter = jnp.exp(m_inter - m_t)
        s = jnp.dot(q, kt, preferred_element_type=F32) * dmat
        num = (jnp.dot(s.astype(BF16), v, preferred_element_type=F32)
               + inter * jnp.dot(q, cq_s[...], preferred_element_type=F32))
        qn = jnp.dot(q, nq_s[...], preferred_element_type=F32)[:, 0:1]
        den = jnp.sum(s, axis=1, keepdims=True) + inter * qn
        h_ref[0, :, hsl] = (num / jnp.maximum(jnp.abs(den), jnp.exp(-m_t))).astype(h_ref.dtype)

        decay, m_new, dC, dn = _state_step(kt, v, ig_row, b_row, b_last, m_prev)
        c_new = decay * c_s[...] + dC
        c_s[...] = c_new
        cq_s[...] = c_new.astype(BF16)
        n_new = decay * n_s[...] + dn
        n_s[...] = n_new
        nq_s[...] = n_new.astype(BF16)
        m_s[...] = jnp.broadcast_to(m_new, (8, 128))


def _scan(q, kt, v, gt, gl, c0, n0, m0):
    B, L, _ = q.shape
    hs = HEADS_PER_STEP
    nc = L // CHUNK
    fwd = lambda j: j
    bwd = lambda j: nc - 1 - j

    def specs(cj):
        return [pl.BlockSpec((1, CHUNK, hs * DH), lambda b, h, j: (b, cj(j), h)),
                pl.BlockSpec((1, hs * DH, CHUNK), lambda b, h, j: (b, h, cj(j))),
                pl.BlockSpec((1, CHUNK, hs * DH), lambda b, h, j: (b, cj(j), h)),
                pl.BlockSpec((1, CHUNK, 16), lambda b, h, j: (b, cj(j), 0)),
                pl.BlockSpec((1, 16, CHUNK), lambda b, h, j: (b, 0, cj(j)))]

    state = [pltpu.VMEM((DH, DH), F32), pltpu.VMEM((DH, DH), BF16),
             pltpu.VMEM((DH, 128), F32), pltpu.VMEM((DH, 128), BF16), pltpu.VMEM((8, 128), F32)]
    out = jax.ShapeDtypeStruct((B, L, D_A), BF16)
    return pl.pallas_call(
        _scan_kernel,
        out_shape=[out, out],
        grid=(B, A_HEADS // hs, nc),
        in_specs=specs(fwd) + specs(bwd) + [
            pl.BlockSpec((2, 1, hs, DH, DH), lambda b, h, j: (0, b, h, 0, 0)),
            pl.BlockSpec((2, 1, hs, DH, 128), lambda b, h, j: (0, b, h, 0, 0)),
            pl.BlockSpec((2, 1, hs, 8, 128), lambda b, h, j: (0, b, h, 0, 0))],
        out_specs=[pl.BlockSpec((1, CHUNK, hs * DH), lambda b, h, j: (b, fwd(j), h)),
                   pl.BlockSpec((1, CHUNK, hs * DH), lambda b, h, j: (b, bwd(j), h))],
        scratch_shapes=state * (2 * hs),
        compiler_params=_cp(("parallel", "parallel", "arbitrary")),
        name="scan",
    )(q, kt, v, gt, gl, q, kt, v, gt, gl, c0, n0, m0)


def _store_permuted(o_ref, lead, x, s_ref):
    T, C = x.shape
    for c in range(C // LANES):
        s_ref[c] = x[:, c * LANES:(c + 1) * LANES]
    for c in range(C // LANES):
        for n2 in range(N2):
            o_ref[lead + (n2, slice(None), slice(c * LANES, (c + 1) * LANES))] = (
                s_ref[c, pl.ds(n2, T // N2, stride=N2), :].astype(o_ref.dtype))


def _hyconv_kernel(prev_ref, hy_ref, next_ref, cw_ref, cb_ref, sh_ref, up_ref, x0_ref, s_ref):
    p, n = _halo_rows(prev_ref, next_ref)
    y = _conv3(hy_ref[0], p, n, cw_ref, cb_ref, sh_ref)
    x0_ref[0] = y[:, 0:D_B].astype(BF16)
    _store_permuted(up_ref, (0,), y[:, D_B:2 * D_B] * y[:, 2 * D_B:3 * D_B], s_ref)


def _hyconv(hy, cw, cb, *, tm):
    B, L, C = hy.shape
    return pl.pallas_call(
        _hyconv_kernel,
        out_shape=[jax.ShapeDtypeStruct((B, N2, L // N2, D_B), BF16), jax.ShapeDtypeStruct((B, L, D_B), BF16)],
        grid=(B, L // tm),
        in_specs=_halo_specs(tm, L, C) + [_const_spec((3, C)), _const_spec((1, C)), _const_spec((2 * BD, BD))],
        out_specs=[pl.BlockSpec((1, N2, tm // N2, D_B), lambda b, i: (b, 0, i, 0)),
                   pl.BlockSpec((1, tm, D_B), lambda b, i: (b, i, 0))],
        scratch_shapes=[pltpu.VMEM((D_B // LANES, tm, LANES), F32)],
        compiler_params=_cp(("parallel", "parallel")),
        name="hyconv",
    )(hy, hy, hy, cw, cb.reshape(1, C), _shift_matrix())


def _filt_kernel(w1_ref, b1_ref, w2_ref, b2_ref, w3_ref, b3_ref, w4_ref, fr_ref, o_ref, s_ref, *, L):
    i = pl.program_id(0)
    T = s_ref.shape[1]
    pos = (lax.broadcasted_iota(jnp.int32, (T, 128), 0) + i * T).astype(F32)
    lane = lax.broadcasted_iota(jnp.int32, (T, 128), 1)
    t = pos / (L - 1.0)
    w = (2.0 * math.pi) * pos / L
    band = jnp.where(lane <= FILTER_BANDS, lane - 1, lane - 1 - FILTER_BANDS).astype(F32)
    f = 1e-4 + band * ((FILTER_BANDS - 1 - 1e-4) / (FILTER_BANDS - 1))
    z = jnp.where(lane == 0, t,
                  jnp.where(lane <= FILTER_BANDS, jnp.cos(f * w),
                            jnp.where(lane <= 2 * FILTER_BANDS, -jnp.sin(f * w), 0.0)))
    fr = fr_ref[...]
    a = jnp.sin(fr * (jnp.dot(z, w1_ref[...], preferred_element_type=F32, precision=HIGHEST) + b1_ref[...]))
    a = jnp.sin(fr * (jnp.dot(a, w2_ref[...], preferred_element_type=F32, precision=HIGHEST) + b2_ref[...]))
    a = jnp.sin(fr * (jnp.dot(a, w3_ref[...], preferred_element_type=F32, precision=HIGHEST) + b3_ref[...]))
    hf = jnp.dot(a, w4_ref[...], preferred_element_type=F32, precision=HIGHEST)
    ch = lax.broadcasted_iota(jnp.int32, (1, D_B), 1).astype(F32)
    deltas = jnp.abs(MIN_DECAY + ch * ((MAX_DECAY - MIN_DECAY) / (D_B - 1)))
    window = jnp.exp(-t[:, 0:1] * deltas) + FILTER_SHIFT
    _store_permuted(o_ref, (0,), hf[:, 0:D_B] * window, s_ref)
    _store_permuted(o_ref, (1,), jnp.where(pos[:, 0:1] == 0.0, 0.0, hf[:, D_B:2 * D_B] * window), s_ref)


def _filters(L, w1, b1, w2, b2, w3, b3, w4, freq, *, tm):
    emb = w1.shape[0]
    w1p = jnp.zeros((128, FILTER_HIDDEN), F32).at[0:emb].set(w1)
    row = lambda a: a.reshape(1, -1)
    return pl.pallas_call(
        functools.partial(_filt_kernel, L=L),
        out_shape=jax.ShapeDtypeStruct((2, N2, L // N2, D_B), BF16),
        grid=(L // tm,),
        in_specs=[_const_spec((128, FILTER_HIDDEN)), _const_spec((1, FILTER_HIDDEN)),
                  _const_spec((FILTER_HIDDEN, FILTER_HIDDEN)), _const_spec((1, FILTER_HIDDEN)),
                  _const_spec((FILTER_HIDDEN, FILTER_HIDDEN)), _const_spec((1, FILTER_HIDDEN)),
                  _const_spec((FILTER_HIDDEN, 2 * D_B)), _const_spec((1, FILTER_HIDDEN))],
        out_specs=pl.BlockSpec((2, N2, tm // N2, D_B), lambda i: (0, 0, i, 0)),
        scratch_shapes=[pltpu.VMEM((D_B // LANES, tm, LANES), F32)],
        compiler_params=_cp(("parallel",)),
        name="filt",
    )(w1p, row(b1), w2, row(b2), w3, row(b3), w4, row(freq))


def _dft_tables():
    half = N1 // 2
    g, ri, kl = np.meshgrid(np.arange(NG), np.arange(2), np.arange(KPG), indexing="ij")
    k1 = (KPG * g + kl).reshape(-1).astype(np.float64)
    is_im = ri.reshape(-1).astype(bool)
    valid = k1 <= half
    n1 = np.arange(half, dtype=np.float64)
    th = 2.0 * np.pi * np.outer(k1, n1) / N1
    f1 = np.where(is_im[:, None], -np.sin(th), np.cos(th)) * valid[:, None]
    ck = np.where((k1 == 0) | (k1 == half), 1.0, 2.0) * valid
    g3 = (np.where(is_im[:, None], -np.sin(th), np.cos(th)) * ck[:, None] / NFFT).T
    mf = np.zeros((NG, BD, BD))
    mb = np.zeros((NG, BD, BD))
    a = np.arange(N2, dtype=np.float64)
    for gi in range(NG):
        for k in range(KPG):
            kk = KPG * gi + k
            if kk > half:
                continue
            phi = 2.0 * np.pi * (np.outer(a, a) / N2 + np.outer(np.ones(N2), a) * kk / NFFT)
            cr, ci = np.cos(phi), -np.sin(phi)
            r0, r1 = k * N2, BD // 2 + k * N2
            mf[gi, r0:r0 + N2, r0:r0 + N2] = cr
            mf[gi, r0:r0 + N2, r1:r1 + N2] = -ci
            mf[gi, r1:r1 + N2, r0:r0 + N2] = ci
            mf[gi, r1:r1 + N2, r1:r1 + N2] = cr
            br, bi = np.cos(phi).T, np.sin(phi).T
            mb[gi, r0:r0 + N2, r0:r0 + N2] = br
            mb[gi, r0:r0 + N2, r1:r1 + N2] = -bi
            mb[gi, r1:r1 + N2, r0:r0 + N2] = bi
            mb[gi, r1:r1 + N2, r1:r1 + N2] = br
    f = lambda t: jnp.asarray(t, dtype=F32)
    return f(f1), f(g3), f(mf), f(mb)


def _stage1(sig, f1_ref, z_s):
    for j in range(N2):
        zj = jnp.dot(f1_ref[...], sig(j), preferred_element_type=F32)
        for c in range(z_s.shape[0]):
            z_s[c, pl.ds(j, F1R, stride=N2), :] = zj[:, c * LANES:(c + 1) * LANES]


def _group_rows(z_s, r0):
    return jnp.concatenate([z_s[c, pl.ds(r0, BD), :] for c in range(z_s.shape[0])], axis=1)


def _fspec_kernel(hp_ref, f1_ref, mf_ref, kf_ref, z_s):
    half = BD // 2
    for s in range(2):
        _stage1(lambda j: hp_ref[s, j], f1_ref, z_s)

        def body(g, carry):
            r0 = pl.multiple_of(g * BD, BD)
            x = jnp.dot(mf_ref[g], _group_rows(z_s, r0).astype(BF16), preferred_element_type=F32)
            if s == 0:
                kf_ref[pl.ds(r0, BD), :] = x
            else:
                kf_ref[pl.ds(r0, half), :] += x[0:half]
                kf_ref[pl.ds(r0 + half, half), :] -= x[half:BD]
            return carry

        lax.fori_loop(0, NG, body, 0, unroll=GROUP_UNROLL)


def _fspec(hp, f1b, mfb):
    C = hp.shape[-1]
    half = N1 // 2
    return pl.pallas_call(
        _fspec_kernel,
        out_shape=jax.ShapeDtypeStruct((ZR, C), F32),
        grid=(C // CT,),
        in_specs=[pl.BlockSpec((2, N2, half, CT), lambda t: (0, 0, 0, t)),
                  _const_spec((F1R, half)), _const_spec((NG, BD, BD))],
        out_specs=pl.BlockSpec((ZR, CT), lambda t: (0, t)),
        scratch_shapes=[pltpu.VMEM((CT // LANES, ZR, LANES), F32)],
        compiler_params=_cp(("parallel",)),
        name="fspec",
    )(hp, f1b, mfb)


def _lconv_kernel(up_ref, x0_ref, kf_ref, f1_ref, mf_ref, mb_ref, g3_ref, ds_ref, o_ref, z_s, y_s):
    half = BD // 2
    nsl = z_s.shape[0]
    _stage1(lambda j: up_ref[0, j], f1_ref, z_s)

    def body(g, carry):
        r0 = pl.multiple_of(g * BD, BD)
        x = jnp.dot(mf_ref[g], _group_rows(z_s, r0).astype(BF16), preferred_element_type=F32)
        kf = kf_ref[pl.ds(r0, BD), :]
        xr, xi = x[0:half], x[half:BD]
        kr, ki = kf[0:half], kf[half:BD]
        y = jnp.concatenate([xr * kr - xi * ki, xr * ki + xi * kr], axis=0)
        v = jnp.dot(mb_ref[g], y.astype(BF16), preferred_element_type=F32)
        for c in range(nsl):
            z_s[c, pl.ds(r0, BD), :] = v[:, c * LANES:(c + 1) * LANES]
        return carry

    lax.fori_loop(0, NG, body, 0, unroll=GROUP_UNROLL)

    for j in range(N2):
        vj = jnp.concatenate([z_s[c, pl.ds(j, F1R, stride=N2), :] for c in range(nsl)], axis=1)
        yj = (jnp.dot(g3_ref[...], vj.astype(BF16), preferred_element_type=F32)
              + up_ref[0, j].astype(F32) * ds_ref[...])
        for c in range(nsl):
            y_s[c, pl.ds(j, N1 // 2, stride=N2), :] = yj[:, c * LANES:(c + 1) * LANES]
    for c in range(nsl):
        sl = slice(c * LANES, (c + 1) * LANES)
        o_ref[0, :, sl] = (x0_ref[0, :, sl].astype(F32) * y_s[c]).astype(o_ref.dtype)


def _lconv(up, x0, kf, f1b, mfb, mbb, g3b, dskip):
    B, L, C = x0.shape
    half = N1 // 2
    return pl.pallas_call(
        _lconv_kernel,
        out_shape=jax.ShapeDtypeStruct((B, L, C), BF16),
        grid=(C // CT, B),
        in_specs=[pl.BlockSpec((1, N2, half, CT), lambda t, b: (b, 0, 0, t)),
                  pl.BlockSpec((1, L, CT), lambda t, b: (b, 0, t)),
                  pl.BlockSpec((ZR, CT), lambda t, b: (0, t), pipeline_mode=pl.Buffered(1)),
                  _const_spec((F1R, half)), _const_spec((NG, BD, BD)), _const_spec((NG, BD, BD)),
                  _const_spec((half, F1R)),
                  pl.BlockSpec((1, CT), lambda t, b: (0, t))],
        out_specs=pl.BlockSpec((1, L, CT), lambda t, b: (b, 0, t)),
        scratch_shapes=[pltpu.VMEM((CT // LANES, ZR, LANES), F32), pltpu.VMEM((CT // LANES, L, LANES), F32)],
        compiler_params=_cp(("parallel", "parallel")),
        name="lconv",
    )(up, x0, kf, f1b, mfb, mbb, g3b, dskip.reshape(1, C))


def _merge_kernel(hf_ref, hb_ref, xc_ref, z_ref, yb_ref, gab_ref, x_ref, mod_ref, ng_ref, sk_ref,
                  wpa_ref, wpb_ref, wo_ref, o_ref):
    h = hf_ref[0].astype(F32) + hb_ref[0].astype(F32)
    xc = xc_ref[0].astype(F32)
    parts = []
    for k in range(A_HEADS):
        sl = slice(k * DH, (k + 1) * DH)
        hh = h[:, sl]
        hn = hh * lax.rsqrt(jnp.mean(hh * hh, axis=-1, keepdims=True) + EPS)
        parts.append(hn * ng_ref[:, sl] + sk_ref[:, sl] * xc[:, sl])
    ya = jax.nn.sigmoid(z_ref[0].astype(F32)) * jnp.concatenate(parts, axis=1)
    gab = gab_ref[0].astype(F32)
    mix = (jax.nn.sigmoid(gab[:, 0:D_MODEL]) * jnp.dot(ya.astype(BF16), wpa_ref[...], preferred_element_type=F32)
           + jax.nn.sigmoid(gab[:, D_MODEL:]) * jnp.dot(yb_ref[0], wpb_ref[...], preferred_element_type=F32))
    out = jnp.dot(mix.astype(BF16), wo_ref[...], preferred_element_type=F32)
    o_ref[0] = x_ref[0] + mod_ref[0, 5:6, :] * out


def _merge(hf, hb, xc, z, yb, gab, x, mods, a_norm_g, a_skip, wpa, wpb, wo, *, tm):
    B, L, D = x.shape
    tok = lambda w: pl.BlockSpec((1, tm, w), lambda b, i: (b, i, 0))
    return pl.pallas_call(
        _merge_kernel,
        out_shape=jax.ShapeDtypeStruct((B, L, D), F32),
        grid=(B, L // tm),
        in_specs=[tok(D_A), tok(D_A),
                  tok(D_A), tok(D_A), tok(D_B), tok(2 * D_MODEL), tok(D),
                  pl.BlockSpec((1, N_MOD, D), lambda b, i: (b, 0, 0)),
                  _const_spec((1, D_A)), _const_spec((1, D_A)),
                  _const_spec((D_A, D)), _const_spec((D_B, D)), _const_spec((D, D))],
        out_specs=tok(D),
        compiler_params=_cp(("parallel", "parallel")),
        name="merge",
    )(hf, hb, xc, z, yb, gab, x, mods, a_norm_g.reshape(1, D_A), a_skip.reshape(1, D_A), wpa, wpb, wo)


def kernel(x, c, ctx, c_ctx, w_ada, b_ada, norm_g, ffn1_up, ffn1_down, ffn2_up, ffn2_down, w_in, a_conv_w, a_conv_b, a_wq, a_wk, a_wv, a_w_gate, a_b_gate, a_norm_g, a_skip, b_conv_w, b_conv_b, b_filt_w1, b_filt_b1, b_filt_w2, b_filt_b2, b_filt_w3, b_filt_b3, b_filt_w4, b_filt_freq, b_skip, w_pa, w_pb, w_out, final_g):
    B, L, D = x.shape
    Lc = ctx.shape[1]
    assert w_ada.shape[0] == 1, "single-layer stack"
    assert 2 * L == NFFT and D == D_MODEL and L // GRID_W == GRID_W
    TM = 512
    FFN_TM = 1024

    c8 = jnp.zeros((8, D), F32).at[0:B].set(c).at[B].set(c_ctx)
    mods = _mods(c8, w_ada[0], b_ada[0]).reshape(8, N_MOD, D)
    ng = norm_g[0]
    row_b = lambda b: b
    row_ctx = lambda b: B

    up1 = ffn1_up[0].astype(BF16)
    dn1 = ffn1_down[0].astype(BF16)
    up2 = ffn2_up[0].astype(BF16)
    dn2 = ffn2_down[0].astype(BF16)
    w_in_b = w_in[0].astype(BF16)

    x1 = _ffn(x, mods, row_b, ng, up1, dn1, sub=0, tm=FFN_TM, pos=_pos_table())
    ctx1 = _ffn(ctx.reshape(1, B * Lc, D), mods, row_ctx, ng, up1, dn1, sub=0, tm=TM)

    widths = (D_A, D_A, 3 * D_B, 2 * D_MODEL)
    xm, z, hy, gab = _proj(x1, mods, row_b, ng, w_in_b, widths, tm=TM)
    (xm_c,) = _proj(ctx1, mods, row_ctx, ng, w_in_b, (D_A,), tm=TM)

    bq, bkt, bv, gqk, gv = _qkvw(a_wq[0], a_wk[0], a_wv[0], a_w_gate[0])
    fw = (a_conv_w[0], a_conv_b[0], bq, bkt, bv, gqk, gv, a_b_gate[0])
    q, xc, kt, v, g, gl = _feat(xm, *fw, tm=TM)
    kt_c, v_c, g_c, gl_c = _feat(xm_c.reshape(B, Lc, D_A), *fw, tm=Lc, with_q=False)

    c0, n0, m0 = _ctxstate(kt_c, v_c, g_c, gl_c)
    hf, hb = _scan(q, kt, v, g, gl, c0, n0, m0)

    up, x0c = _hyconv(hy, b_conv_w[0], b_conv_b[0], tm=TM)
    f1b, g3b, mfb, mbb = (a.astype(BF16) for a in _dft_tables())
    hp = _filters(L, b_filt_w1[0], b_filt_b1[0], b_filt_w2[0], b_filt_b2[0], b_filt_w3[0], b_filt_b3[0],
                  b_filt_w4[0], b_filt_freq[0], tm=TM)
    kf = _fspec(hp, f1b, mfb)
    yb = _lconv(up, x0c, kf, f1b, mfb, mbb, g3b, b_skip[0])

    x2 = _merge(hf, hb, xc, z, yb, gab, x1, mods, a_norm_g[0], a_skip[0],
                w_pa[0].astype(BF16), w_pb[0].astype(BF16), w_out[0].astype(BF16), tm=TM)

    return _ffn(x2, mods, row_b, ng, up2, dn2, sub=2, tm=FFN_TM, final_g=final_g)
```

```python
import functools
import math

import numpy as np
import jax
import jax.numpy as jnp
from jax import lax
from jax.experimental import pallas as pl
from jax.experimental.pallas import tpu as pltpu

F32 = jnp.float32
BF16 = jnp.bfloat16

D_MODEL = 1024
D_A = 2048
A_HEADS = 4
DH = D_A // A_HEADS
QKV_BLOCK = 4
D_B = 1024
D_FF = 2816
EPS = 1e-6
N_MOD = 9
GRID_W = 64
CHUNK = 256
HEADS_PER_STEP = 2
SCAN_BLOCK = 2
FILTER_BANDS = 16
FILTER_HIDDEN = 64
DECAY_TARGET = 1e-2
MAX_DECAY = math.log(DECAY_TARGET) / 0.3
MIN_DECAY = math.log(DECAY_TARGET) / 1.5
FILTER_SHIFT = 0.05

LANES = 128
SUBLANES = 8
BD = 256

N2 = SUBLANES
NFFT = 8192
N1 = NFFT // N2
KPG = BD // (2 * N2)
NG = (N1 // 2) // KPG + 1
F1R = NG * 2 * KPG
ZR = F1R * N2
NYQ = (NG - 1) * 2 * KPG
CT = 256
GROUP_UNROLL = 11

VMEM_LIMIT = 56 * 1024 * 1024
HIGHEST = lax.Precision.HIGHEST
NEG = -1e30


def _cp(sem):
    return pltpu.CompilerParams(dimension_semantics=sem, vmem_limit_bytes=VMEM_LIMIT)


def _const_spec(shape):
    nd = len(shape)
    return pl.BlockSpec(shape, lambda *_: (0,) * nd, pipeline_mode=pl.Buffered(1))


def _silu(x):
    return x * jax.nn.sigmoid(x)


def _rms(x, g):
    return x * lax.rsqrt(jnp.mean(x * x, axis=-1, keepdims=True) + EPS) * g


def _mods_kernel(c_ref, w_ref, b_ref, o_ref):
    cs = _silu(c_ref[...])
    o_ref[...] = jnp.dot(cs, w_ref[...], preferred_element_type=F32, precision=HIGHEST) + b_ref[...]


def _mods(c8, w_ada, b_ada):
    n = w_ada.shape[1]
    tn = 1024
    return pl.pallas_call(
        _mods_kernel,
        out_shape=jax.ShapeDtypeStruct((8, n), F32),
        grid=(n // tn,),
        in_specs=[pl.BlockSpec((8, D_MODEL), lambda j: (0, 0)),
                  pl.BlockSpec((D_MODEL, tn), lambda j: (0, j)),
                  pl.BlockSpec((1, tn), lambda j: (0, j))],
        out_specs=pl.BlockSpec((8, tn), lambda j: (0, j)),
        compiler_params=_cp(("parallel",)),
        name="mods",
    )(c8, w_ada, b_ada.reshape(1, n))


def _postab_kernel(om_ref, o_ref):
    nf = om_ref.shape[1]
    p = lax.broadcasted_iota(jnp.int32, (GRID_W, nf), 0).astype(F32)
    a = p * om_ref[...]
    o_ref[:, 0:nf] = jnp.sin(a)
    o_ref[:, nf:2 * nf] = jnp.cos(a)


def _pos_table():
    nf = D_MODEL // 4
    omega = 1.0 / (10000.0 ** (jnp.arange(nf, dtype=F32) / nf))
    return pl.pallas_call(
        _postab_kernel,
        out_shape=jax.ShapeDtypeStruct((GRID_W, 2 * nf), F32),
        name="postab",
    )(omega.reshape(1, nf))


FF_CHUNKS = ((0, 6 * BD), (6 * BD, D_FF))


def _ffn_kernel(*refs, sub, has_pos, final):
    it = iter(refs)
    x_ref = next(it)
    pos_ref = next(it) if has_pos else None
    mod_ref = next(it)
    g_ref = next(it)
    up_ref = next(it)
    dn_ref = next(it)
    fg_ref = next(it) if final else None
    o_ref = next(it)

    x = x_ref[0]
    if has_pos:
        tm = x.shape[0]
        per = tm // GRID_W
        half = pos_ref.shape[1]
        i = pl.program_id(1)
        ecol = pos_ref[...]
        rows = []
        for q in range(per):
            erow = jnp.broadcast_to(pos_ref[pl.ds(i * per + q, 1), :], (GRID_W, half))
            rows.append(jnp.concatenate([erow, ecol], axis=1))
        x = x + jnp.concatenate(rows, axis=0)
    shift = mod_ref[0, 3 * sub:3 * sub + 1, :]
    scale = mod_ref[0, 3 * sub + 1:3 * sub + 2, :]
    gate = mod_ref[0, 3 * sub + 2:3 * sub + 3, :]
    h = _rms(x, g_ref[sub:sub + 1, :]) * (1.0 + scale) + shift
    hb = h.astype(BF16)
    acc = jnp.zeros(x.shape, F32)
    for lo, hi in FF_CHUNKS:
        gg = jnp.dot(hb, up_ref[:, lo:hi], preferred_element_type=F32)
        uu = jnp.dot(hb, up_ref[:, D_FF + lo:D_FF + hi], preferred_element_type=F32)
        a = (_silu(gg) * uu).astype(BF16)
        acc = acc + jnp.dot(a, dn_ref[lo:hi, :], preferred_element_type=F32)
    y = x + 0.5 * gate * acc
    if final:
        y = _rms(y, fg_ref[...])
    o_ref[0] = y


def _ffn(x, mods, mod_row, norm_g, up_b, dn_b, *, sub, tm, pos=None, final_g=None):
    B, L, D = x.shape
    args = [x]
    specs = [pl.BlockSpec((1, tm, D), lambda b, i: (b, i, 0))]
    if pos is not None:
        args.append(pos)
        specs.append(_const_spec(pos.shape))
    args += [mods, norm_g, up_b, dn_b]
    specs += [pl.BlockSpec((1, N_MOD, D), lambda b, i: (mod_row(b), 0, 0)),
              _const_spec((3, D)), _const_spec((D, 2 * D_FF)), _const_spec((D_FF, D))]
    if final_g is not None:
        args.append(final_g.reshape(1, D))
        specs.append(_const_spec((1, D)))
    return pl.pallas_call(
        functools.partial(_ffn_kernel, sub=sub, has_pos=pos is not None, final=final_g is not None),
        out_shape=jax.ShapeDtypeStruct((B, L, D), F32),
        grid=(B, L // tm),
        in_specs=specs,
        out_specs=pl.BlockSpec((1, tm, D), lambda b, i: (b, i, 0)),
        compiler_params=_cp(("parallel", "parallel")),
        name=f"ffn{sub}",
    )(*args)


def _proj_kernel(x_ref, mod_ref, g_ref, w_ref, *o_refs, widths):
    x = x_ref[0]
    shift = mod_ref[0, 3:4, :]
    scale = mod_ref[0, 4:5, :]
    hb = (_rms(x, g_ref[1:2, :]) * (1.0 + scale) + shift).astype(BF16)
    lo = 0
    for o_ref, w in zip(o_refs, widths):
        o_ref[0] = jnp.dot(hb, w_ref[:, lo:lo + w], preferred_element_type=F32).astype(o_ref.dtype)
        lo += w


def _proj(x, mods, mod_row, norm_g, w_b, widths, *, tm):
    B, L, D = x.shape
    n = sum(widths)
    return pl.pallas_call(
        functools.partial(_proj_kernel, widths=widths),
        out_shape=[jax.ShapeDtypeStruct((B, L, w), BF16) for w in widths],
        grid=(B, L // tm),
        in_specs=[pl.BlockSpec((1, tm, D), lambda b, i: (b, i, 0)),
                  pl.BlockSpec((1, N_MOD, D), lambda b, i: (mod_row(b), 0, 0)),
                  _const_spec((3, D)), _const_spec((D, n))],
        out_specs=[pl.BlockSpec((1, tm, w), lambda b, i: (b, i, 0)) for w in widths],
        compiler_params=_cp(("parallel", "parallel")),
        name="proj",
    )(x, mods, norm_g, w_b)


HALO = 16


def _shift_matrix():
    return jnp.concatenate([jnp.eye(BD, k=-1, dtype=BF16), jnp.eye(BD, k=1, dtype=BF16)], axis=0)


def _conv3(xb, prev_row, next_row, w_ref, b_ref, s_ref):
    T, C = xb.shape
    w0, w1, w2, b = w_ref[0:1, :], w_ref[1:2, :], w_ref[2:3, :], b_ref[...]
    x = xb.astype(F32)
    r8 = lax.broadcasted_iota(jnp.int32, (SUBLANES, C), 0)
    out = []
    for r0 in range(0, T, BD):
        r1 = r0 + BD
        sh = jnp.dot(s_ref[...], xb[r0:r1], preferred_element_type=F32)
        y = sh[0:BD] * w0 + x[r0:r1] * w1 + sh[BD:2 * BD] * w2 + b
        before = prev_row if r0 == 0 else x[r0 - 1:r0]
        after = next_row if r1 == T else x[r1:r1 + 1]
        first = before * w0 + x[r0:r0 + 1] * w1 + x[r0 + 1:r0 + 2] * w2 + b
        last = x[r1 - 2:r1 - 1] * w0 + x[r1 - 1:r1] * w1 + after * w2 + b
        top = jnp.where(r8 == 0, first, y[0:SUBLANES])
        bot = jnp.where(r8 == SUBLANES - 1, last, y[BD - SUBLANES:BD])
        out += [top, y[SUBLANES:BD - SUBLANES], bot]
    return jnp.concatenate(out, axis=0)


def _halo_rows(prev_ref, next_ref):
    i = pl.program_id(1)
    n = pl.num_programs(1)
    p = prev_ref[0, HALO - 1:HALO, :].astype(F32)
    q = next_ref[0, 0:1, :].astype(F32)
    p = jnp.where(i == 0, 0.0, p)
    q = jnp.where(i == n - 1, 0.0, q)
    return p, q


def _halo_specs(tm, L, C):
    r = tm // HALO
    nb = L // HALO
    return [pl.BlockSpec((1, HALO, C), lambda b, i: (b, jnp.maximum(i * r - 1, 0), 0)),
            pl.BlockSpec((1, tm, C), lambda b, i: (b, i, 0)),
            pl.BlockSpec((1, HALO, C), lambda b, i: (b, jnp.minimum((i + 1) * r, nb - 1), 0))]


def _qkvw_kernel(wq_ref, wk_ref, wv_ref, wg_ref, bq_ref, bkt_ref, bv_ref, gqk_ref, gv_ref):
    r = lax.broadcasted_iota(jnp.int32, (BD, BD), 0)
    c = lax.broadcasted_iota(jnp.int32, (BD, BD), 1)
    shift = QKV_BLOCK.bit_length() - 1
    same_block = (r >> shift) == (c >> shift)
    col_in_block = c & (QKV_BLOCK - 1)

    def tile(w_ref):
        w2 = w_ref[0]
        a = jnp.zeros((BD, BD), F32)
        for j in range(QKV_BLOCK):
            a = a + jnp.where(col_in_block == j, w2[:, j:j + 1], 0.0)
        return jnp.where(same_block, a, 0.0)

    tq = tile(wq_ref)
    tk = tile(wk_ref)
    tv = tile(wv_ref)
    bq_ref[0] = tq.astype(BF16)
    bkt_ref[0] = tk.T.astype(BF16)
    bv_ref[0] = tv.astype(BF16)
    gqk_ref[...] = (jnp.dot(tq, wg_ref[0], preferred_element_type=F32, precision=HIGHEST)
                    + jnp.dot(tk, wg_ref[1], preferred_element_type=F32, precision=HIGHEST)).astype(BF16)
    gv_ref[...] = jnp.dot(tv, wg_ref[2], preferred_element_type=F32, precision=HIGHEST).astype(BF16)


def _qkvw(wq, wk, wv, w_gate):
    nb = D_A // BD
    w2 = lambda w: w.reshape(nb, BD, QKV_BLOCK)
    wspec = pl.BlockSpec((1, BD, QKV_BLOCK), lambda j: (j, 0, 0))
    tspec = pl.BlockSpec((1, BD, BD), lambda j: (j, 0, 0))
    gspec = pl.BlockSpec((BD, 16), lambda j: (j, 0))
    tile = jax.ShapeDtypeStruct((nb, BD, BD), BF16)
    gw = jax.ShapeDtypeStruct((D_A, 16), BF16)
    return pl.pallas_call(
        _qkvw_kernel,
        out_shape=[tile, tile, tile, gw, gw],
        grid=(nb,),
        in_specs=[wspec, wspec, wspec, pl.BlockSpec((3, BD, 16), lambda j: (0, j, 0))],
        out_specs=[tspec, tspec, tspec, gspec, gspec],
        compiler_params=_cp(("parallel",)),
        name="qkvw",
    )(w2(wq), w2(wk), w2(wv), w_gate.reshape(3, D_A, 16))


def _feat_kernel(prev_ref, xm_ref, next_ref, cw_ref, cb_ref, sh_ref, bq_ref, bkt_ref, bv_ref, gqk_ref, gv_ref,
                 gb_ref, *o_refs, with_q):
    if with_q:
        q_ref, xc_ref, kt_ref, v_ref, g_ref, gl_ref = o_refs
    else:
        kt_ref, v_ref, g_ref, gl_ref = o_refs
    xmb = xm_ref[0]
    p, n = _halo_rows(prev_ref, next_ref)
    xc = _silu(_conv3(xmb, p, n, cw_ref, cb_ref, sh_ref))
    xcb = xc.astype(BF16)
    if with_q:
        xc_ref[0] = xcb
    g = (jnp.dot(xcb, gqk_ref[...], preferred_element_type=F32)
         + jnp.dot(xmb, gv_ref[...], preferred_element_type=F32) + gb_ref[...])
    g_ref[0] = g
    gl_ref[0] = g.T
    for j in range(D_A // BD):
        sl = slice(j * BD, (j + 1) * BD)
        if with_q:
            q = jnp.dot(xcb[:, sl], bq_ref[j], preferred_element_type=F32)
            q_ref[0, :, sl] = (q * (DH ** -0.5)).astype(BF16)
        kt_ref[0, sl, :] = lax.dot_general(bkt_ref[j], xcb[:, sl], (((1,), (1,)), ((), ())),
                                          preferred_element_type=F32).astype(BF16)
        v_ref[0, :, sl] = jnp.dot(xmb[:, sl], bv_ref[j], preferred_element_type=F32).astype(BF16)


def _feat(xm, cw, cb, bq, bkt, bv, gqk, gv, gb, *, tm, with_q=True):
    B, L, _ = xm.shape
    nb = D_A // BD
    tok = jax.ShapeDtypeStruct((B, L, D_A), BF16)
    tok_spec = pl.BlockSpec((1, tm, D_A), lambda b, i: (b, i, 0))
    shapes = [jax.ShapeDtypeStruct((B, D_A, L), BF16), tok, jax.ShapeDtypeStruct((B, L, 16), F32),
              jax.ShapeDtypeStruct((B, 16, L), F32)]
    specs = [pl.BlockSpec((1, D_A, tm), lambda b, i: (b, 0, i)), tok_spec,
             pl.BlockSpec((1, tm, 16), lambda b, i: (b, i, 0)), pl.BlockSpec((1, 16, tm), lambda b, i: (b, 0, i))]
    if with_q:
        shapes = [tok, tok] + shapes
        specs = [tok_spec, tok_spec] + specs
    return pl.pallas_call(
        functools.partial(_feat_kernel, with_q=with_q),
        out_shape=shapes,
        grid=(B, L // tm),
        in_specs=_halo_specs(tm, L, D_A) + [
            _const_spec((3, D_A)), _const_spec((1, D_A)), _const_spec((2 * BD, BD)),
            _const_spec((nb, BD, BD)), _const_spec((nb, BD, BD)), _const_spec((nb, BD, BD)),
            _const_spec((D_A, 16)), _const_spec((D_A, 16)), _const_spec((1, 16))],
        out_specs=specs,
        compiler_params=_cp(("parallel", "parallel")),
        name="feat" if with_q else "feat_ctx",
    )(xm, xm, xm, cw, cb.reshape(1, D_A), _shift_matrix(), bq, bkt, bv, gqk, gv, gb.reshape(1, 16))


def _gate_vectors(gt, gl, idx, sign):
    T = gt.shape[0]
    sub = lax.broadcasted_iota(jnp.int32, gl.shape, 0)
    lane = lax.broadcasted_iota(jnp.int32, gt.shape, 1)
    ig_row = jnp.sum(jnp.where(sub == idx, gl, 0.0), axis=0, keepdims=True)
    fg_row = jnp.sum(jnp.where(sub == idx + A_HEADS, gl, 0.0), axis=0, keepdims=True)
    fg_col = jnp.sum(jnp.where(lane == idx + A_HEADS, gt, 0.0), axis=1, keepdims=True)
    lf_row = jax.nn.log_sigmoid(fg_row)
    lf_col = jax.nn.log_sigmoid(fg_col)
    r = lax.broadcasted_iota(jnp.int32, (T, T), 0)
    c = lax.broadcasted_iota(jnp.int32, (T, T), 1)
    mask = sign * (r - c) >= 0
    mask_t = sign * (c - r) >= 0
    b_col = jnp.sum(jnp.where(mask, lf_row, 0.0), axis=1, keepdims=True)
    b_row = jnp.sum(jnp.where(mask_t, lf_col, 0.0), axis=0, keepdims=True)
    b_last = jnp.sum(lf_row, axis=1, keepdims=True)
    return ig_row, b_col, b_row, b_last, mask


def _state_step(kt, v, ig_row, b_row, b_last, m_prev):
    log_w = b_last - b_row + ig_row
    m_new = jnp.maximum(b_last + m_prev, jnp.max(log_w, axis=1, keepdims=True))
    decay = jnp.exp(b_last + m_prev - m_new)
    w = jnp.exp(log_w - m_new)
    kw = kt.astype(F32) * w
    dC = jnp.dot(kw.astype(BF16), v, preferred_element_type=F32)
    dn = jnp.sum(kw, axis=1, keepdims=True)
    return decay, m_new, dC, dn


def _ctxstate_kernel(kt_ref, v_ref, gt_ref, gl_ref, c_ref, n_ref, m_ref):
    hp = pl.program_id(1)
    m0 = jnp.zeros((1, 1), F32)
    for d in range(2):
        for hh in range(HEADS_PER_STEP):
            hsl = slice(hh * DH, (hh + 1) * DH)
            head = hp * HEADS_PER_STEP + hh
            ig_row, _, b_row, b_last, _ = _gate_vectors(gt_ref[0], gl_ref[0], 8 * d + head, 1 - 2 * d)
            _, m_new, dC, dn = _state_step(kt_ref[0, hsl, :], v_ref[0, :, hsl], ig_row, b_row, b_last, m0)
            c_ref[d, 0, hh] = dC
            n_ref[d, 0, hh] = jnp.broadcast_to(dn, (DH, 128))
            m_ref[d, 0, hh] = jnp.broadcast_to(m_new, (8, 128))


def _ctxstate(kt, v, gt, gl):
    B, _, Lc = kt.shape
    H = A_HEADS
    hs = HEADS_PER_STEP
    return pl.pallas_call(
        _ctxstate_kernel,
        out_shape=[jax.ShapeDtypeStruct((2, B, H, DH, DH), F32),
                   jax.ShapeDtypeStruct((2, B, H, DH, 128), F32),
                   jax.ShapeDtypeStruct((2, B, H, 8, 128), F32)],
        grid=(B, H // hs),
        in_specs=[pl.BlockSpec((1, hs * DH, Lc), lambda b, h: (b, h, 0)),
                  pl.BlockSpec((1, Lc, hs * DH), lambda b, h: (b, 0, h)),
                  pl.BlockSpec((1, Lc, 16), lambda b, h: (b, 0, 0)),
                  pl.BlockSpec((1, 16, Lc), lambda b, h: (b, 0, 0))],
        out_specs=[pl.BlockSpec((2, 1, hs, DH, DH), lambda b, h: (0, b, h, 0, 0)),
                   pl.BlockSpec((2, 1, hs, DH, 128), lambda b, h: (0, b, h, 0, 0)),
                   pl.BlockSpec((2, 1, hs, 8, 128), lambda b, h: (0, b, h, 0, 0))],
        compiler_params=_cp(("parallel", "parallel")),
        name="ctxstate",
    )(kt, v, gt, gl)


def _scan_kernel(qf_ref, ktf_ref, vf_ref, gtf_ref, glf_ref, qb_ref, ktb_ref, vb_ref, gtb_ref, glb_ref,
                 c0_ref, n0_ref, m0_ref, hf_ref, hb_ref, *scratch):
    hp = pl.program_id(1)
    j = pl.program_id(2)
    io = ((qf_ref, ktf_ref, vf_ref, gtf_ref, glf_ref, hf_ref), (qb_ref, ktb_ref, vb_ref, gtb_ref, glb_ref, hb_ref))
    chains = [(d, hh) + io[d] + tuple(scratch[5 * (d * HEADS_PER_STEP + hh):5 * (d * HEADS_PER_STEP + hh) + 5])
              for d in range(2) for hh in range(HEADS_PER_STEP)]

    @pl.when(j == 0)
    def _():
        for d, hh, _, _, _, _, _, _, c_s, cq_s, n_s, nq_s, m_s in chains:
            c0 = c0_ref[d, 0, hh]
            c_s[...] = c0
            cq_s[...] = c0.astype(BF16)
            n0 = n0_ref[d, 0, hh]
            n_s[...] = n0
            nq_s[...] = n0.astype(BF16)
            m_s[...] = m0_ref[d, 0, hh]

    for sub, (d, hh, q_ref, kt_ref, v_ref, gt_ref, gl_ref, h_ref, c_s, cq_s, n_s, nq_s, m_s) in (
            (sub, chain) for sub in range(SCAN_BLOCK) for chain in chains):
        sign = 1 - 2 * d
        hsl = slice(hh * DH, (hh + 1) * DH)
        r0 = (sub if d == 0 else SCAN_BLOCK - 1 - sub) * CHUNK
        rows = slice(r0, r0 + CHUNK)
        q = q_ref[0, rows, hsl]
        kt = kt_ref[0, hsl, rows]
        v = v_ref[0, rows, hsl]
        head = hp * HEADS_PER_STEP + hh
        ig_row, b_col, b_row, b_last, mask = _gate_vectors(gt_ref[0, rows, :], gl_ref[0, :, rows], 8 * d + head, sign)
        m_prev = m_s[0:1, 0:1]

        log_d = jnp.where(mask, b_col - b_row + ig_row, NEG)
        m_inter = b_col + m_prev
        m_t = jnp.maximum(m_inter, jnp.max(log_d, axis=1, keepdims=True))
        dmat = jnp.exp(log_d - m_t)
        inter = jnp.exp(m_inter - m_t)
        s = jnp.dot(q, kt, preferred_element_type=F32) * dmat
        num = (jnp.dot(s.astype(BF16), v, preferred_element_type=F32)
               + inter * jnp.dot(q, cq_s[...], preferred_element_type=F32))
        qn = jnp.dot(q, nq_s[...], preferred_element_type=F32)[:, 0:1]
        den = jnp.sum(s, axis=1, keepdims=True) + inter * qn
        h_ref[0, rows, hsl] = (num / jnp.maximum(jnp.abs(den), jnp.exp(-m_t))).astype(h_ref.dtype)

        decay, m_new, dC, dn = _state_step(kt, v, ig_row, b_row, b_last, m_prev)
        c_new = decay * c_s[...] + dC
        c_s[...] = c_new
        cq_s[...] = c_new.astype(BF16)
        n_new = decay * n_s[...] + dn
        n_s[...] = n_new
        nq_s[...] = n_new.astype(BF16)
        m_s[...] = jnp.broadcast_to(m_new, (8, 128))


def _scan(q, kt, v, gt, gl, c0, n0, m0):
    B, L, _ = q.shape
    hs = HEADS_PER_STEP
    T = SCAN_BLOCK * CHUNK
    nc = L // T
    fwd = lambda j: j
    bwd = lambda j: nc - 1 - j

    def specs(cj):
        return [pl.BlockSpec((1, T, hs * DH), lambda b, h, j: (b, cj(j), h)),
                pl.BlockSpec((1, hs * DH, T), lambda b, h, j: (b, h, cj(j))),
                pl.BlockSpec((1, T, hs * DH), lambda b, h, j: (b, cj(j), h)),
                pl.BlockSpec((1, T, 16), lambda b, h, j: (b, cj(j), 0)),
                pl.BlockSpec((1, 16, T), lambda b, h, j: (b, 0, cj(j)))]

    state = [pltpu.VMEM((DH, DH), F32), pltpu.VMEM((DH, DH), BF16),
             pltpu.VMEM((DH, 128), F32), pltpu.VMEM((DH, 128), BF16), pltpu.VMEM((8, 128), F32)]
    out = jax.ShapeDtypeStruct((B, L, D_A), BF16)
    return pl.pallas_call(
        _scan_kernel,
        out_shape=[out, out],
        grid=(B, A_HEADS // hs, nc),
        in_specs=specs(fwd) + specs(bwd) + [
            pl.BlockSpec((2, 1, hs, DH, DH), lambda b, h, j: (0, b, h, 0, 0)),
            pl.BlockSpec((2, 1, hs, DH, 128), lambda b, h, j: (0, b, h, 0, 0)),
            pl.BlockSpec((2, 1, hs, 8, 128), lambda b, h, j: (0, b, h, 0, 0))],
        out_specs=[pl.BlockSpec((1, T, hs * DH), lambda b, h, j: (b, fwd(j), h)),
                   pl.BlockSpec((1, T, hs * DH), lambda b, h, j: (b, bwd(j), h))],
        scratch_shapes=state * (2 * hs),
        compiler_params=_cp(("parallel", "parallel", "arbitrary")),
        name="scan",
    )(q, kt, v, gt, gl, q, kt, v, gt, gl, c0, n0, m0)


def _store_permuted(o_ref, lead, x, s_ref):
    T, C = x.shape
    for c in range(C // LANES):
        s_ref[c] = x[:, c * LANES:(c + 1) * LANES]
    for c in range(C // LANES):
        for n2 in range(N2):
            o_ref[lead + (n2, slice(None), slice(c * LANES, (c + 1) * LANES))] = (
                s_ref[c, pl.ds(n2, T // N2, stride=N2), :].astype(o_ref.dtype))


def _hyconv_kernel(prev_ref, hy_ref, next_ref, cw_ref, cb_ref, sh_ref, up_ref, x0_ref, s_ref):
    p, n = _halo_rows(prev_ref, next_ref)
    y = _conv3(hy_ref[0], p, n, cw_ref, cb_ref, sh_ref)
    x0_ref[0] = y[:, 0:D_B].astype(BF16)
    _store_permuted(up_ref, (0,), y[:, D_B:2 * D_B] * y[:, 2 * D_B:3 * D_B], s_ref)


def _hyconv(hy, cw, cb, *, tm):
    B, L, C = hy.shape
    return pl.pallas_call(
        _hyconv_kernel,
        out_shape=[jax.ShapeDtypeStruct((B, N2, L // N2, D_B), BF16), jax.ShapeDtypeStruct((B, L, D_B), BF16)],
        grid=(B, L // tm),
        in_specs=_halo_specs(tm, L, C) + [_const_spec((3, C)), _const_spec((1, C)), _const_spec((2 * BD, BD))],
        out_specs=[pl.BlockSpec((1, N2, tm // N2, D_B), lambda b, i: (b, 0, i, 0)),
                   pl.BlockSpec((1, tm, D_B), lambda b, i: (b, i, 0))],
        scratch_shapes=[pltpu.VMEM((D_B // LANES, tm, LANES), F32)],
        compiler_params=_cp(("parallel", "parallel")),
        name="hyconv",
    )(hy, hy, hy, cw, cb.reshape(1, C), _shift_matrix())


def _filt_kernel(w1_ref, b1_ref, w2_ref, b2_ref, w3_ref, b3_ref, w4_ref, fr_ref, o_ref, s_ref, *, L):
    i = pl.program_id(0)
    T = s_ref.shape[1]
    pos = (lax.broadcasted_iota(jnp.int32, (T, 128), 0) + i * T).astype(F32)
    lane = lax.broadcasted_iota(jnp.int32, (T, 128), 1)
    t = pos / (L - 1.0)
    w = (2.0 * math.pi) * pos / L
    band = jnp.where(lane <= FILTER_BANDS, lane - 1, lane - 1 - FILTER_BANDS).astype(F32)
    f = 1e-4 + band * ((FILTER_BANDS - 1 - 1e-4) / (FILTER_BANDS - 1))
    z = jnp.where(lane == 0, t,
                  jnp.where(lane <= FILTER_BANDS, jnp.cos(f * w),
                            jnp.where(lane <= 2 * FILTER_BANDS, -jnp.sin(f * w), 0.0)))
    fr = fr_ref[...]
    a = jnp.sin(fr * (jnp.dot(z, w1_ref[...], preferred_element_type=F32, precision=HIGHEST) + b1_ref[...]))
    a = jnp.sin(fr * (jnp.dot(a, w2_ref[...], preferred_element_type=F32, precision=HIGHEST) + b2_ref[...]))
    a = jnp.sin(fr * (jnp.dot(a, w3_ref[...], preferred_element_type=F32, precision=HIGHEST) + b3_ref[...]))
    hf = jnp.dot(a, w4_ref[...], preferred_element_type=F32, precision=HIGHEST)
    ch = lax.broadcasted_iota(jnp.int32, (1, D_B), 1).astype(F32)
    deltas = jnp.abs(MIN_DECAY + ch * ((MAX_DECAY - MIN_DECAY) / (D_B - 1)))
    window = jnp.exp(-t[:, 0:1] * deltas) + FILTER_SHIFT
    _store_permuted(o_ref, (0,), hf[:, 0:D_B] * window, s_ref)
    _store_permuted(o_ref, (1,), jnp.where(pos[:, 0:1] == 0.0, 0.0, hf[:, D_B:2 * D_B] * window), s_ref)


def _filters(L, w1, b1, w2, b2, w3, b3, w4, freq, *, tm):
    emb = w1.shape[0]
    w1p = jnp.zeros((128, FILTER_HIDDEN), F32).at[0:emb].set(w1)
    row = lambda a: a.reshape(1, -1)
    return pl.pallas_call(
        functools.partial(_filt_kernel, L=L),
        out_shape=jax.ShapeDtypeStruct((2, N2, L // N2, D_B), BF16),
        grid=(L // tm,),
        in_specs=[_const_spec((128, FILTER_HIDDEN)), _const_spec((1, FILTER_HIDDEN)),
                  _const_spec((FILTER_HIDDEN, FILTER_HIDDEN)), _const_spec((1, FILTER_HIDDEN)),
                  _const_spec((FILTER_HIDDEN, FILTER_HIDDEN)), _const_spec((1, FILTER_HIDDEN)),
                  _const_spec((FILTER_HIDDEN, 2 * D_B)), _const_spec((1, FILTER_HIDDEN))],
        out_specs=pl.BlockSpec((2, N2, tm // N2, D_B), lambda i: (0, 0, i, 0)),
        scratch_shapes=[pltpu.VMEM((D_B // LANES, tm, LANES), F32)],
        compiler_params=_cp(("parallel",)),
        name="filt",
    )(w1p, row(b1), w2, row(b2), w3, row(b3), w4, row(freq))


def _dft_tables():
    half = N1 // 2
    g, ri, kl = np.meshgrid(np.arange(NG), np.arange(2), np.arange(KPG), indexing="ij")
    k1 = (KPG * g + kl).reshape(-1).astype(np.float64)
    is_im = ri.reshape(-1).astype(bool)
    valid = k1 <= half
    n1 = np.arange(half, dtype=np.float64)
    th = 2.0 * np.pi * np.outer(k1, n1) / N1
    f1 = np.where(is_im[:, None], -np.sin(th), np.cos(th)) * valid[:, None]
    ck = np.where((k1 == 0) | (k1 == half), 1.0, 2.0) * valid
    g3 = (np.where(is_im[:, None], -np.sin(th), np.cos(th)) * ck[:, None] / NFFT).T
    mf = np.zeros((NG, BD, BD))
    mb = np.zeros((NG, BD, BD))
    a = np.arange(N2, dtype=np.float64)
    for gi in range(NG):
        for k in range(KPG):
            kk = KPG * gi + k
            if kk > half:
                continue
            phi = 2.0 * np.pi * (np.outer(a, a) / N2 + np.outer(np.ones(N2), a) * kk / NFFT)
            cr, ci = np.cos(phi), -np.sin(phi)
            r0, r1 = k * N2, BD // 2 + k * N2
            mf[gi, r0:r0 + N2, r0:r0 + N2] = cr
            mf[gi, r0:r0 + N2, r1:r1 + N2] = -ci
            mf[gi, r1:r1 + N2, r0:r0 + N2] = ci
            mf[gi, r1:r1 + N2, r1:r1 + N2] = cr
            br, bi = np.cos(phi).T, np.sin(phi).T
            mb[gi, r0:r0 + N2, r0:r0 + N2] = br
            mb[gi, r0:r0 + N2, r1:r1 + N2] = -bi
            mb[gi, r1:r1 + N2, r0:r0 + N2] = bi
            mb[gi, r1:r1 + N2, r1:r1 + N2] = br
    f = lambda t: jnp.asarray(t, dtype=F32)
    return f(f1), f(g3[:, :NYQ]), f(g3[:, NYQ:NYQ + 1]), f(mf), f(mb)


def _stage1(sig, f1_ref, z_s):
    for j in range(N2):
        zj = jnp.dot(f1_ref[...], sig(j), preferred_element_type=F32)
        for c in range(z_s.shape[0]):
            z_s[c, pl.ds(j, F1R, stride=N2), :] = zj[:, c * LANES:(c + 1) * LANES]


def _group_rows(z_s, r0):
    return jnp.concatenate([z_s[c, pl.ds(r0, BD), :] for c in range(z_s.shape[0])], axis=1)


def _fspec_kernel(hp_ref, f1_ref, mf_ref, kf_ref, z_s):
    half = BD // 2
    for s in range(2):
        _stage1(lambda j: hp_ref[s, j], f1_ref, z_s)

        def body(g, carry):
            r0 = pl.multiple_of(g * BD, BD)
            x = jnp.dot(mf_ref[g], _group_rows(z_s, r0).astype(BF16), preferred_element_type=F32)
            if s == 0:
                kf_ref[pl.ds(r0, BD), :] = x
            else:
                kf_ref[pl.ds(r0, half), :] += x[0:half]
                kf_ref[pl.ds(r0 + half, half), :] -= x[half:BD]
            return carry

        lax.fori_loop(0, NG, body, 0, unroll=GROUP_UNROLL)


def _fspec(hp, f1b, mfb):
    C = hp.shape[-1]
    half = N1 // 2
    return pl.pallas_call(
        _fspec_kernel,
        out_shape=jax.ShapeDtypeStruct((ZR, C), F32),
        grid=(C // CT,),
        in_specs=[pl.BlockSpec((2, N2, half, CT), lambda t: (0, 0, 0, t)),
                  _const_spec((F1R, half)), _const_spec((NG, BD, BD))],
        out_specs=pl.BlockSpec((ZR, CT), lambda t: (0, t)),
        scratch_shapes=[pltpu.VMEM((CT // LANES, ZR, LANES), F32)],
        compiler_params=_cp(("parallel",)),
        name="fspec",
    )(hp, f1b, mfb)


def _lconv_kernel(up_ref, x0_ref, kf_ref, f1_ref, mf_ref, mb_ref, g3_ref, g3n_ref, ds_ref, o_ref, z_s, y_s):
    half = BD // 2
    nsl = z_s.shape[0]
    _stage1(lambda j: up_ref[0, j], f1_ref, z_s)

    def body(g, carry):
        r0 = pl.multiple_of(g * BD, BD)
        x = jnp.dot(mf_ref[g], _group_rows(z_s, r0).astype(BF16), preferred_element_type=F32)
        kf = kf_ref[pl.ds(r0, BD), :]
        xr, xi = x[0:half], x[half:BD]
        kr, ki = kf[0:half], kf[half:BD]
        y = jnp.concatenate([xr * kr - xi * ki, xr * ki + xi * kr], axis=0)
        v = jnp.dot(mb_ref[g], y.astype(BF16), preferred_element_type=F32)
        for c in range(nsl):
            z_s[c, pl.ds(r0, BD), :] = v[:, c * LANES:(c + 1) * LANES]
        return carry

    lax.fori_loop(0, NG, body, 0, unroll=GROUP_UNROLL)

    for j in range(N2):
        vj = jnp.concatenate([z_s[c, pl.ds(j, F1R, stride=N2), :] for c in range(nsl)], axis=1)
        yj = (jnp.dot(g3_ref[...], vj[0:NYQ].astype(BF16), preferred_element_type=F32)
              + g3n_ref[...] * vj[NYQ:NYQ + 1] + up_ref[0, j].astype(F32) * ds_ref[...])
        for c in range(nsl):
            y_s[c, pl.ds(j, N1 // 2, stride=N2), :] = yj[:, c * LANES:(c + 1) * LANES]
    for c in range(nsl):
        sl = slice(c * LANES, (c + 1) * LANES)
        o_ref[0, :, sl] = (x0_ref[0, :, sl].astype(F32) * y_s[c]).astype(o_ref.dtype)


def _lconv(up, x0, kf, f1b, mfb, mbb, g3b, g3n, dskip):
    B, L, C = x0.shape
    half = N1 // 2
    return pl.pallas_call(
        _lconv_kernel,
        out_shape=jax.ShapeDtypeStruct((B, L, C), BF16),
        grid=(C // CT, B),
        in_specs=[pl.BlockSpec((1, N2, half, CT), lambda t, b: (b, 0, 0, t)),
                  pl.BlockSpec((1, L, CT), lambda t, b: (b, 0, t)),
                  pl.BlockSpec((ZR, CT), lambda t, b: (0, t)),
                  _const_spec((F1R, half)), _const_spec((NG, BD, BD)), _const_spec((NG, BD, BD)),
                  _const_spec((half, NYQ)), _const_spec((half, 1)),
                  pl.BlockSpec((1, CT), lambda t, b: (0, t))],
        out_specs=pl.BlockSpec((1, L, CT), lambda t, b: (b, 0, t)),
        scratch_shapes=[pltpu.VMEM((CT // LANES, ZR, LANES), F32), pltpu.VMEM((CT // LANES, L, LANES), F32)],
        compiler_params=_cp(("parallel", "parallel")),
        name="lconv",
    )(up, x0, kf, f1b, mfb, mbb, g3b, g3n, dskip.reshape(1, C))


def _merge_kernel(hf_ref, hb_ref, xc_ref, z_ref, yb_ref, gab_ref, x_ref, mod_ref, ng_ref, sk_ref,
                  wpa_ref, wpb_ref, wo_ref, o_ref):
    h = hf_ref[0].astype(F32) + hb_ref[0].astype(F32)
    xc = xc_ref[0].astype(F32)
    parts = []
    for k in range(A_HEADS):
        sl = slice(k * DH, (k + 1) * DH)
        hh = h[:, sl]
        hn = hh * lax.rsqrt(jnp.mean(hh * hh, axis=-1, keepdims=True) + EPS)
        parts.append(hn * ng_ref[:, sl] + sk_ref[:, sl] * xc[:, sl])
    ya = jax.nn.sigmoid(z_ref[0].astype(F32)) * jnp.concatenate(parts, axis=1)
    gab = gab_ref[0].astype(F32)
    mix = (jax.nn.sigmoid(gab[:, 0:D_MODEL]) * jnp.dot(ya.astype(BF16), wpa_ref[...], preferred_element_type=F32)
           + jax.nn.sigmoid(gab[:, D_MODEL:]) * jnp.dot(yb_ref[0], wpb_ref[...], preferred_element_type=F32))
    out = jnp.dot(mix.astype(BF16), wo_ref[...], preferred_element_type=F32)
    o_ref[0] = x_ref[0] + mod_ref[0, 5:6, :] * out


def _merge(hf, hb, xc, z, yb, gab, x, mods, a_norm_g, a_skip, wpa, wpb, wo, *, tm):
    B, L, D = x.shape
    tok = lambda w: pl.BlockSpec((1, tm, w), lambda b, i: (b, i, 0))
    return pl.pallas_call(
        _merge_kernel,
        out_shape=jax.ShapeDtypeStruct((B, L, D), F32),
        grid=(B, L // tm),
        in_specs=[tok(D_A), tok(D_A),
                  tok(D_A), tok(D_A), tok(D_B), tok(2 * D_MODEL), tok(D),
                  pl.BlockSpec((1, N_MOD, D), lambda b, i: (b, 0, 0)),
                  _const_spec((1, D_A)), _const_spec((1, D_A)),
                  _const_spec((D_A, D)), _const_spec((D_B, D)), _const_spec((D, D))],
        out_specs=tok(D),
        compiler_params=_cp(("parallel", "parallel")),
        name="merge",
    )(hf, hb, xc, z, yb, gab, x, mods, a_norm_g.reshape(1, D_A), a_skip.reshape(1, D_A), wpa, wpb, wo)


def kernel(x, c, ctx, c_ctx, w_ada, b_ada, norm_g, ffn1_up, ffn1_down, ffn2_up, ffn2_down, w_in, a_conv_w, a_conv_b, a_wq, a_wk, a_wv, a_w_gate, a_b_gate, a_norm_g, a_skip, b_conv_w, b_conv_b, b_filt_w1, b_filt_b1, b_filt_w2, b_filt_b2, b_filt_w3, b_filt_b3, b_filt_w4, b_filt_freq, b_skip, w_pa, w_pb, w_out, final_g):
    B, L, D = x.shape
    Lc = ctx.shape[1]
    assert w_ada.shape[0] == 1, "single-layer stack"
    assert 2 * L == NFFT and D == D_MODEL and L // GRID_W == GRID_W
    TM = 512
    FFN_TM = 1024

    c8 = jnp.zeros((8, D), F32).at[0:B].set(c).at[B].set(c_ctx)
    mods = _mods(c8, w_ada[0], b_ada[0]).reshape(8, N_MOD, D)
    ng = norm_g[0]
    row_b = lambda b: b
    row_ctx = lambda b: B

    up1 = ffn1_up[0].astype(BF16)
    dn1 = ffn1_down[0].astype(BF16)
    up2 = ffn2_up[0].astype(BF16)
    dn2 = ffn2_down[0].astype(BF16)
    w_in_b = w_in[0].astype(BF16)

    x1 = _ffn(x, mods, row_b, ng, up1, dn1, sub=0, tm=FFN_TM, pos=_pos_table())
    ctx1 = _ffn(ctx.reshape(1, B * Lc, D), mods, row_ctx, ng, up1, dn1, sub=0, tm=TM)

    widths = (D_A, D_A, 3 * D_B, 2 * D_MODEL)
    xm, z, hy, gab = _proj(x1, mods, row_b, ng, w_in_b, widths, tm=TM)
    (xm_c,) = _proj(ctx1, mods, row_ctx, ng, w_in_b, (D_A,), tm=TM)

    bq, bkt, bv, gqk, gv = _qkvw(a_wq[0], a_wk[0], a_wv[0], a_w_gate[0])
    fw = (a_conv_w[0], a_conv_b[0], bq, bkt, bv, gqk, gv, a_b_gate[0])
    q, xc, kt, v, g, gl = _feat(xm, *fw, tm=TM)
    kt_c, v_c, g_c, gl_c = _feat(xm_c.reshape(B, Lc, D_A), *fw, tm=Lc, with_q=False)

    c0, n0, m0 = _ctxstate(kt_c, v_c, g_c, gl_c)
    hf, hb = _scan(q, kt, v, g, gl, c0, n0, m0)

    up, x0c = _hyconv(hy, b_conv_w[0], b_conv_b[0], tm=TM)
    f1, g3, g3n, mf, mb = _dft_tables()
    f1b, g3b, mfb, mbb = (a.astype(BF16) for a in (f1, g3, mf, mb))
    hp = _filters(L, b_filt_w1[0], b_filt_b1[0], b_filt_w2[0], b_filt_b2[0], b_filt_w3[0], b_filt_b3[0],
                  b_filt_w4[0], b_filt_freq[0], tm=TM)
    kf = _fspec(hp, f1b, mfb)
    yb = _lconv(up, x0c, kf, f1b, mfb, mbb, g3b, g3n, b_skip[0])

    x2 = _merge(hf, hb, xc, z, yb, gab, x1, mods, a_norm_g[0], a_skip[0],
                w_pa[0].astype(BF16), w_pb[0].astype(BF16), w_out[0].astype(BF16), tm=TM)

    return _ffn(x2, mods, row_b, ng, up2, dn2, sub=2, tm=FFN_TM, final_g=final_g)
```

```python
import functools
import math

import numpy as np
import jax
import jax.numpy as jnp
from jax import lax
from jax.experimental import pallas as pl
from jax.experimental.pallas import tpu as pltpu

F32 = jnp.float32
BF16 = jnp.bfloat16

D_MODEL = 1024
D_A = 2048
A_HEADS = 4
DH = D_A // A_HEADS
QKV_BLOCK = 4
D_B = 1024
D_FF = 2816
EPS = 1e-6
N_MOD = 9
GRID_W = 64
CHUNK = 256
HEADS_PER_STEP = 2
SCAN_BLOCK = 2
FILTER_BANDS = 16
FILTER_HIDDEN = 64
DECAY_TARGET = 1e-2
MAX_DECAY = math.log(DECAY_TARGET) / 0.3
MIN_DECAY = math.log(DECAY_TARGET) / 1.5
FILTER_SHIFT = 0.05

LANES = 128
SUBLANES = 8
BD = 256

N2 = SUBLANES
NFFT = 8192
N1 = NFFT // N2
KPG = BD // (2 * N2)
NG = (N1 // 2) // KPG + 1
F1R = NG * 2 * KPG
ZR = F1R * N2
NYQ = (NG - 1) * 2 * KPG
CT = 256
GROUP_UNROLL = 11

VMEM_LIMIT = 56 * 1024 * 1024
HIGHEST = lax.Precision.HIGHEST
NEG = -1e30


def _cp(sem):
    return pltpu.CompilerParams(dimension_semantics=sem, vmem_limit_bytes=VMEM_LIMIT)


def _const_spec(shape):
    nd = len(shape)
    return pl.BlockSpec(shape, lambda *_: (0,) * nd, pipeline_mode=pl.Buffered(1))


def _silu(x):
    return x * jax.nn.sigmoid(x)


def _load_cast(w_hbm, w_s, stage, sem):
    rows = stage.shape[1]
    n = w_hbm.shape[0] // rows

    def copy(k):
        return pltpu.make_async_copy(w_hbm.at[pl.ds(k * rows, rows), :], stage.at[k % 2], sem.at[k % 2])

    copy(0).start()
    for k in range(n):
        if k + 1 < n:
            copy(k + 1).start()
        copy(k).wait()
        w_s[pl.ds(k * rows, rows), :] = stage[k % 2].astype(BF16)


def _first_step():
    return (pl.program_id(0) == 0) & (pl.program_id(1) == 0)


ANY_SPEC = pl.BlockSpec(memory_space=pl.ANY)


def _rms(x, g):
    return x * lax.rsqrt(jnp.mean(x * x, axis=-1, keepdims=True) + EPS) * g


def _mods_kernel(c_ref, w_ref, b_ref, o_ref):
    cs = _silu(c_ref[...])
    o_ref[...] = jnp.dot(cs, w_ref[...], preferred_element_type=F32, precision=HIGHEST) + b_ref[...]


def _mods(c8, w_ada, b_ada):
    n = w_ada.shape[1]
    tn = 1024
    return pl.pallas_call(
        _mods_kernel,
        out_shape=jax.ShapeDtypeStruct((8, n), F32),
        grid=(n // tn,),
        in_specs=[pl.BlockSpec((8, D_MODEL), lambda j: (0, 0)),
                  pl.BlockSpec((D_MODEL, tn), lambda j: (0, j)),
                  pl.BlockSpec((1, tn), lambda j: (0, j))],
        out_specs=pl.BlockSpec((8, tn), lambda j: (0, j)),
        compiler_params=_cp(("parallel",)),
        name="mods",
    )(c8, w_ada, b_ada.reshape(1, n))


def _postab_kernel(om_ref, o_ref):
    nf = om_ref.shape[1]
    p = lax.broadcasted_iota(jnp.int32, (GRID_W, nf), 0).astype(F32)
    a = p * om_ref[...]
    o_ref[:, 0:nf] = jnp.sin(a)
    o_ref[:, nf:2 * nf] = jnp.cos(a)


def _pos_table():
    nf = D_MODEL // 4
    omega = 1.0 / (10000.0 ** (jnp.arange(nf, dtype=F32) / nf))
    return pl.pallas_call(
        _postab_kernel,
        out_shape=jax.ShapeDtypeStruct((GRID_W, 2 * nf), F32),
        name="postab",
    )(omega.reshape(1, nf))


FF_CHUNKS = ((0, 6 * BD), (6 * BD, D_FF))


def _ffn_kernel(*refs, sub, has_pos, final, f32_weights):
    it = iter(refs)
    x_ref = next(it)
    pos_ref = next(it) if has_pos else None
    mod_ref = next(it)
    g_ref = next(it)
    up_ref = next(it)
    dn_ref = next(it)
    fg_ref = next(it) if final else None
    o_ref = next(it)
    if f32_weights:
        up_hbm, dn_hbm = up_ref, dn_ref
        up_ref, dn_ref, up_stage, dn_stage, up_sem, dn_sem = (next(it) for _ in range(6))

        @pl.when(_first_step())
        def _():
            _load_cast(up_hbm, up_ref, up_stage, up_sem)
            _load_cast(dn_hbm, dn_ref, dn_stage, dn_sem)

    x = x_ref[0]
    if has_pos:
        tm = x.shape[0]
        per = tm // GRID_W
        half = pos_ref.shape[1]
        i = pl.program_id(1)
        ecol = pos_ref[...]
        rows = []
        for q in range(per):
            erow = jnp.broadcast_to(pos_ref[pl.ds(i * per + q, 1), :], (GRID_W, half))
            rows.append(jnp.concatenate([erow, ecol], axis=1))
        x = x + jnp.concatenate(rows, axis=0)
    shift = mod_ref[0, 3 * sub:3 * sub + 1, :]
    scale = mod_ref[0, 3 * sub + 1:3 * sub + 2, :]
    gate = mod_ref[0, 3 * sub + 2:3 * sub + 3, :]
    h = _rms(x, g_ref[sub:sub + 1, :]) * (1.0 + scale) + shift
    hb = h.astype(BF16)
    acc = jnp.zeros(x.shape, F32)
    for lo, hi in FF_CHUNKS:
        gg = jnp.dot(hb, up_ref[:, lo:hi], preferred_element_type=F32)
        uu = jnp.dot(hb, up_ref[:, D_FF + lo:D_FF + hi], preferred_element_type=F32)
        a = (_silu(gg) * uu).astype(BF16)
        acc = acc + jnp.dot(a, dn_ref[lo:hi, :], preferred_element_type=F32)
    y = x + 0.5 * gate * acc
    if final:
        y = _rms(y, fg_ref[...])
    o_ref[0] = y


W_STAGE_STEPS = 16


def _ffn(x, mods, mod_row, norm_g, up, dn, *, sub, tm, pos=None, final_g=None):
    B, L, D = x.shape
    f32_weights = up.dtype == F32
    args = [x]
    specs = [pl.BlockSpec((1, tm, D), lambda b, i: (b, i, 0))]
    if pos is not None:
        args.append(pos)
        specs.append(_const_spec(pos.shape))
    args += [mods, norm_g, up, dn]
    wspecs = [ANY_SPEC, ANY_SPEC] if f32_weights else [_const_spec((D, 2 * D_FF)), _const_spec((D_FF, D))]
    specs += [pl.BlockSpec((1, N_MOD, D), lambda b, i: (mod_row(b), 0, 0)), _const_spec((3, D))] + wspecs
    if final_g is not None:
        args.append(final_g.reshape(1, D))
        specs.append(_const_spec((1, D)))
    scratch = []
    if f32_weights:
        scratch = [pltpu.VMEM((D, 2 * D_FF), BF16), pltpu.VMEM((D_FF, D), BF16),
                   pltpu.VMEM((2, D // W_STAGE_STEPS, 2 * D_FF), F32), pltpu.VMEM((2, D_FF // W_STAGE_STEPS, D), F32),
                   pltpu.SemaphoreType.DMA((2,)), pltpu.SemaphoreType.DMA((2,))]
    return pl.pallas_call(
        functools.partial(_ffn_kernel, sub=sub, has_pos=pos is not None, final=final_g is not None,
                          f32_weights=f32_weights),
        out_shape=jax.ShapeDtypeStruct((B, L, D), F32),
        grid=(B, L // tm),
        in_specs=specs,
        out_specs=pl.BlockSpec((1, tm, D), lambda b, i: (b, i, 0)),
        scratch_shapes=scratch,
        compiler_params=_cp(("arbitrary", "arbitrary") if f32_weights else ("parallel", "parallel")),
        name=f"ffn{sub}",
    )(*args)


def _proj_kernel(x_ref, mod_ref, g_ref, w_ref, *refs, widths, f32_weights):
    o_refs = refs[:len(widths)]
    if f32_weights:
        w_hbm = w_ref
        w_ref, w_stage, w_sem = refs[len(widths):]

        @pl.when(_first_step())
        def _():
            _load_cast(w_hbm, w_ref, w_stage, w_sem)

    x = x_ref[0]
    shift = mod_ref[0, 3:4, :]
    scale = mod_ref[0, 4:5, :]
    hb = (_rms(x, g_ref[1:2, :]) * (1.0 + scale) + shift).astype(BF16)
    lo = 0
    for o_ref, w in zip(o_refs, widths):
        o_ref[0] = jnp.dot(hb, w_ref[:, lo:lo + w], preferred_element_type=F32).astype(o_ref.dtype)
        lo += w


def _proj(x, mods, mod_row, norm_g, w_b, widths, *, tm):
    B, L, D = x.shape
    n = sum(widths)
    f32_weights = w_b.dtype == F32
    assert not f32_weights or w_b.shape == (D, n)
    scratch = []
    if f32_weights:
        scratch = [pltpu.VMEM((D, n), BF16), pltpu.VMEM((2, D // (2 * W_STAGE_STEPS), n), F32),
                   pltpu.SemaphoreType.DMA((2,))]
    return pl.pallas_call(
        functools.partial(_proj_kernel, widths=widths, f32_weights=f32_weights),
        out_shape=[jax.ShapeDtypeStruct((B, L, w), BF16) for w in widths],
        grid=(B, L // tm),
        in_specs=[pl.BlockSpec((1, tm, D), lambda b, i: (b, i, 0)),
                  pl.BlockSpec((1, N_MOD, D), lambda b, i: (mod_row(b), 0, 0)),
                  _const_spec((3, D)), ANY_SPEC if f32_weights else _const_spec((D, n))],
        out_specs=[pl.BlockSpec((1, tm, w), lambda b, i: (b, i, 0)) for w in widths],
        scratch_shapes=scratch,
        compiler_params=_cp(("arbitrary", "arbitrary") if f32_weights else ("parallel", "parallel")),
        name="proj",
    )(x, mods, norm_g, w_b)


HALO = 16


def _shift_matrix():
    return jnp.concatenate([jnp.eye(BD, k=-1, dtype=BF16), jnp.eye(BD, k=1, dtype=BF16)], axis=0)


def _conv3(xb, prev_row, next_row, w_ref, b_ref, s_ref):
    T, C = xb.shape
    w0, w1, w2, b = w_ref[0:1, :], w_ref[1:2, :], w_ref[2:3, :], b_ref[...]
    x = xb.astype(F32)
    r8 = lax.broadcasted_iota(jnp.int32, (SUBLANES, C), 0)
    out = []
    for r0 in range(0, T, BD):
        r1 = r0 + BD
        sh = jnp.dot(s_ref[...], xb[r0:r1], preferred_element_type=F32)
        y = sh[0:BD] * w0 + x[r0:r1] * w1 + sh[BD:2 * BD] * w2 + b
        before = prev_row if r0 == 0 else x[r0 - 1:r0]
        after = next_row if r1 == T else x[r1:r1 + 1]
        first = before * w0 + x[r0:r0 + 1] * w1 + x[r0 + 1:r0 + 2] * w2 + b
        last = x[r1 - 2:r1 - 1] * w0 + x[r1 - 1:r1] * w1 + after * w2 + b
        top = jnp.where(r8 == 0, first, y[0:SUBLANES])
        bot = jnp.where(r8 == SUBLANES - 1, last, y[BD - SUBLANES:BD])
        out += [top, y[SUBLANES:BD - SUBLANES], bot]
    return jnp.concatenate(out, axis=0)


def _halo_rows(prev_ref, next_ref):
    i = pl.program_id(1)
    n = pl.num_programs(1)
    p = prev_ref[0, HALO - 1:HALO, :].astype(F32)
    q = next_ref[0, 0:1, :].astype(F32)
    p = jnp.where(i == 0, 0.0, p)
    q = jnp.where(i == n - 1, 0.0, q)
    return p, q


def _halo_specs(tm, L, C):
    r = tm // HALO
    nb = L // HALO
    return [pl.BlockSpec((1, HALO, C), lambda b, i: (b, jnp.maximum(i * r - 1, 0), 0)),
            pl.BlockSpec((1, tm, C), lambda b, i: (b, i, 0)),
            pl.BlockSpec((1, HALO, C), lambda b, i: (b, jnp.minimum((i + 1) * r, nb - 1), 0))]


def _qkvw_kernel(wq_ref, wk_ref, wv_ref, wg_ref, bq_ref, bkt_ref, bv_ref, gqk_ref, gv_ref):
    r = lax.broadcasted_iota(jnp.int32, (BD, BD), 0)
    c = lax.broadcasted_iota(jnp.int32, (BD, BD), 1)
    shift = QKV_BLOCK.bit_length() - 1
    same_block = (r >> shift) == (c >> shift)
    col_in_block = c & (QKV_BLOCK - 1)

    def tile(w_ref):
        w2 = w_ref[0]
        a = jnp.zeros((BD, BD), F32)
        for j in range(QKV_BLOCK):
            a = a + jnp.where(col_in_block == j, w2[:, j:j + 1], 0.0)
        return jnp.where(same_block, a, 0.0)

    tq = tile(wq_ref)
    tk = tile(wk_ref)
    tv = tile(wv_ref)
    bq_ref[0] = tq.astype(BF16)
    bkt_ref[0] = tk.T.astype(BF16)
    bv_ref[0] = tv.astype(BF16)
    gqk_ref[...] = (jnp.dot(tq, wg_ref[0], preferred_element_type=F32, precision=HIGHEST)
                    + jnp.dot(tk, wg_ref[1], preferred_element_type=F32, precision=HIGHEST)).astype(BF16)
    gv_ref[...] = jnp.dot(tv, wg_ref[2], preferred_element_type=F32, precision=HIGHEST).astype(BF16)


def _qkvw(wq, wk, wv, w_gate):
    nb = D_A // BD
    w2 = lambda w: w.reshape(nb, BD, QKV_BLOCK)
    wspec = pl.BlockSpec((1, BD, QKV_BLOCK), lambda j: (j, 0, 0))
    tspec = pl.BlockSpec((1, BD, BD), lambda j: (j, 0, 0))
    gspec = pl.BlockSpec((BD, 16), lambda j: (j, 0))
    tile = jax.ShapeDtypeStruct((nb, BD, BD), BF16)
    gw = jax.ShapeDtypeStruct((D_A, 16), BF16)
    return pl.pallas_call(
        _qkvw_kernel,
        out_shape=[tile, tile, tile, gw, gw],
        grid=(nb,),
        in_specs=[wspec, wspec, wspec, pl.BlockSpec((3, BD, 16), lambda j: (0, j, 0))],
        out_specs=[tspec, tspec, tspec, gspec, gspec],
        compiler_params=_cp(("parallel",)),
        name="qkvw",
    )(w2(wq), w2(wk), w2(wv), w_gate.reshape(3, D_A, 16))


def _feat_kernel(prev_ref, xm_ref, next_ref, cw_ref, cb_ref, sh_ref, bq_ref, bkt_ref, bv_ref, gqk_ref, gv_ref,
                 gb_ref, *o_refs, with_q):
    if with_q:
        q_ref, xc_ref, kt_ref, v_ref, g_ref, gl_ref = o_refs
    else:
        kt_ref, v_ref, g_ref, gl_ref = o_refs
    xmb = xm_ref[0]
    p, n = _halo_rows(prev_ref, next_ref)
    xc = _silu(_conv3(xmb, p, n, cw_ref, cb_ref, sh_ref))
    xcb = xc.astype(BF16)
    if with_q:
        xc_ref[0] = xcb
    g = (jnp.dot(xcb, gqk_ref[...], preferred_element_type=F32)
         + jnp.dot(xmb, gv_ref[...], preferred_element_type=F32) + gb_ref[...])
    g_ref[0] = g
    gl_ref[0] = g.T
    for j in range(D_A // BD):
        sl = slice(j * BD, (j + 1) * BD)
        if with_q:
            q = jnp.dot(xcb[:, sl], bq_ref[j], preferred_element_type=F32)
            q_ref[0, :, sl] = (q * (DH ** -0.5)).astype(BF16)
        kt_ref[0, sl, :] = lax.dot_general(bkt_ref[j], xcb[:, sl], (((1,), (1,)), ((), ())),
                                          preferred_element_type=F32).astype(BF16)
        v_ref[0, :, sl] = jnp.dot(xmb[:, sl], bv_ref[j], preferred_element_type=F32).astype(BF16)


def _feat(xm, cw, cb, bq, bkt, bv, gqk, gv, gb, *, tm, with_q=True):
    B, L, _ = xm.shape
    nb = D_A // BD
    tok = jax.ShapeDtypeStruct((B, L, D_A), BF16)
    tok_spec = pl.BlockSpec((1, tm, D_A), lambda b, i: (b, i, 0))
    shapes = [jax.ShapeDtypeStruct((B, D_A, L), BF16), tok, jax.ShapeDtypeStruct((B, L, 16), F32),
              jax.ShapeDtypeStruct((B, 16, L), F32)]
    specs = [pl.BlockSpec((1, D_A, tm), lambda b, i: (b, 0, i)), tok_spec,
             pl.BlockSpec((1, tm, 16), lambda b, i: (b, i, 0)), pl.BlockSpec((1, 16, tm), lambda b, i: (b, 0, i))]
    if with_q:
        shapes = [tok, tok] + shapes
        specs = [tok_spec, tok_spec] + specs
    return pl.pallas_call(
        functools.partial(_feat_kernel, with_q=with_q),
        out_shape=shapes,
        grid=(B, L // tm),
        in_specs=_halo_specs(tm, L, D_A) + [
            _const_spec((3, D_A)), _const_spec((1, D_A)), _const_spec((2 * BD, BD)),
            _const_spec((nb, BD, BD)), _const_spec((nb, BD, BD)), _const_spec((nb, BD, BD)),
            _const_spec((D_A, 16)), _const_spec((D_A, 16)), _const_spec((1, 16))],
        out_specs=specs,
        compiler_params=_cp(("parallel", "parallel")),
        name="feat" if with_q else "feat_ctx",
    )(xm, xm, xm, cw, cb.reshape(1, D_A), _shift_matrix(), bq, bkt, bv, gqk, gv, gb.reshape(1, 16))


def _gate_vectors(gt, gl, idx, sign):
    T = gt.shape[0]
    sub = lax.broadcasted_iota(jnp.int32, gl.shape, 0)
    lane = lax.broadcasted_iota(jnp.int32, gt.shape, 1)
    ig_row = jnp.sum(jnp.where(sub == idx, gl, 0.0), axis=0, keepdims=True)
    fg_row = jnp.sum(jnp.where(sub == idx + A_HEADS, gl, 0.0), axis=0, keepdims=True)
    fg_col = jnp.sum(jnp.where(lane == idx + A_HEADS, gt, 0.0), axis=1, keepdims=True)
    lf_row = jax.nn.log_sigmoid(fg_row)
    lf_col = jax.nn.log_sigmoid(fg_col)
    r = lax.broadcasted_iota(jnp.int32, (T, T), 0)
    c = lax.broadcasted_iota(jnp.int32, (T, T), 1)
    mask = sign * (r - c) >= 0
    mask_t = sign * (c - r) >= 0
    b_col = jnp.sum(jnp.where(mask, lf_row, 0.0), axis=1, keepdims=True)
    b_row = jnp.sum(jnp.where(mask_t, lf_col, 0.0), axis=0, keepdims=True)
    b_last = jnp.sum(lf_row, axis=1, keepdims=True)
    return ig_row, b_col, b_row, b_last, mask


def _state_step(kt, v, ig_row, b_row, b_last, m_prev):
    log_w = b_last - b_row + ig_row
    m_new = jnp.maximum(b_last + m_prev, jnp.max(log_w, axis=1, keepdims=True))
    decay = jnp.exp(b_last + m_prev - m_new)
    w = jnp.exp(log_w - m_new)
    kw = kt.astype(F32) * w
    dC = jnp.dot(kw.astype(BF16), v, preferred_element_type=F32)
    dn = jnp.sum(kw, axis=1, keepdims=True)
    return decay, m_new, dC, dn


def _ctxstate_kernel(kt_ref, v_ref, gt_ref, gl_ref, c_ref, n_ref, m_ref):
    hp = pl.program_id(1)
    m0 = jnp.zeros((1, 1), F32)
    for d in range(2):
        for hh in range(HEADS_PER_STEP):
            hsl = slice(hh * DH, (hh + 1) * DH)
            head = hp * HEADS_PER_STEP + hh
            ig_row, _, b_row, b_last, _ = _gate_vectors(gt_ref[0], gl_ref[0], 8 * d + head, 1 - 2 * d)
            _, m_new, dC, dn = _state_step(kt_ref[0, hsl, :], v_ref[0, :, hsl], ig_row, b_row, b_last, m0)
            c_ref[d, 0, hh] = dC
            n_ref[d, 0, hh] = jnp.broadcast_to(dn, (DH, 128))
            m_ref[d, 0, hh] = jnp.broadcast_to(m_new, (8, 128))


def _ctxstate(kt, v, gt, gl):
    B, _, Lc = kt.shape
    H = A_HEADS
    hs = HEADS_PER_STEP
    return pl.pallas_call(
        _ctxstate_kernel,
        out_shape=[jax.ShapeDtypeStruct((2, B, H, DH, DH), F32),
                   jax.ShapeDtypeStruct((2, B, H, DH, 128), F32),
                   jax.ShapeDtypeStruct((2, B, H, 8, 128), F32)],
        grid=(B, H // hs),
        in_specs=[pl.BlockSpec((1, hs * DH, Lc), lambda b, h: (b, h, 0)),
                  pl.BlockSpec((1, Lc, hs * DH), lambda b, h: (b, 0, h)),
                  pl.BlockSpec((1, Lc, 16), lambda b, h: (b, 0, 0)),
                  pl.BlockSpec((1, 16, Lc), lambda b, h: (b, 0, 0))],
        out_specs=[pl.BlockSpec((2, 1, hs, DH, DH), lambda b, h: (0, b, h, 0, 0)),
                   pl.BlockSpec((2, 1, hs, DH, 128), lambda b, h: (0, b, h, 0, 0)),
                   pl.BlockSpec((2, 1, hs, 8, 128), lambda b, h: (0, b, h, 0, 0))],
        compiler_params=_cp(("parallel", "parallel")),
        name="ctxstate",
    )(kt, v, gt, gl)


def _scan_kernel(qf_ref, ktf_ref, vf_ref, gtf_ref, glf_ref, qb_ref, ktb_ref, vb_ref, gtb_ref, glb_ref,
                 c0_ref, n0_ref, m0_ref, hf_ref, hb_ref, *scratch):
    hp = pl.program_id(1)
    j = pl.program_id(2)
    io = ((qf_ref, ktf_ref, vf_ref, gtf_ref, glf_ref, hf_ref), (qb_ref, ktb_ref, vb_ref, gtb_ref, glb_ref, hb_ref))
    chains = [(d, hh) + io[d] + tuple(scratch[5 * (d * HEADS_PER_STEP + hh):5 * (d * HEADS_PER_STEP + hh) + 5])
              for d in range(2) for hh in range(HEADS_PER_STEP)]

    @pl.when(j == 0)
    def _():
        for d, hh, _, _, _, _, _, _, c_s, cq_s, n_s, nq_s, m_s in chains:
            c0 = c0_ref[d, 0, hh]
            c_s[...] = c0
            cq_s[...] = c0.astype(BF16)
            n0 = n0_ref[d, 0, hh]
            n_s[...] = n0
            nq_s[...] = n0.astype(BF16)
            m_s[...] = m0_ref[d, 0, hh]

    for sub, (d, hh, q_ref, kt_ref, v_ref, gt_ref, gl_ref, h_ref, c_s, cq_s, n_s, nq_s, m_s) in (
            (sub, chain) for sub in range(SCAN_BLOCK) for chain in chains):
        sign = 1 - 2 * d
        hsl = slice(hh * DH, (hh + 1) * DH)
        r0 = (sub if d == 0 else SCAN_BLOCK - 1 - sub) * CHUNK
        rows = slice(r0, r0 + CHUNK)
        q = q_ref[0, rows, hsl]
        kt = kt_ref[0, hsl, rows]
        v = v_ref[0, rows, hsl]
        head = hp * HEADS_PER_STEP + hh
        ig_row, b_col, b_row, b_last, mask = _gate_vectors(gt_ref[0, rows, :], gl_ref[0, :, rows], 8 * d + head, sign)
        m_prev = m_s[0:1, 0:1]

        log_d = jnp.where(mask, b_col - b_row + ig_row, NEG)
        m_inter = b_col + m_prev
        m_t = jnp.maximum(m_inter, jnp.max(log_d, axis=1, keepdims=True))
        dmat = jnp.exp(log_d - m_t)
        inter = jnp.exp(m_inter - m_t)
        s = jnp.dot(q, kt, preferred_element_type=F32) * dmat
        num = (jnp.dot(s.astype(BF16), v, preferred_element_type=F32)
               + inter * jnp.dot(q, cq_s[...], preferred_element_type=F32))
        qn = jnp.dot(q, nq_s[...], preferred_element_type=F32)[:, 0:1]
        den = jnp.sum(s, axis=1, keepdims=True) + inter * qn
        h_ref[0, rows, hsl] = (num / jnp.maximum(jnp.abs(den), jnp.exp(-m_t))).astype(h_ref.dtype)

        decay, m_new, dC, dn = _state_step(kt, v, ig_row, b_row, b_last, m_prev)
        c_new = decay * c_s[...] + dC
        c_s[...] = c_new
        cq_s[...] = c_new.astype(BF16)
        n_new = decay * n_s[...] + dn
        n_s[...] = n_new
        nq_s[...] = n_new.astype(BF16)
        m_s[...] = jnp.broadcast_to(m_new, (8, 128))


def _scan(q, kt, v, gt, gl, c0, n0, m0):
    B, L, _ = q.shape
    hs = HEADS_PER_STEP
    T = SCAN_BLOCK * CHUNK
    nc = L // T
    fwd = lambda j: j
    bwd = lambda j: nc - 1 - j

    def specs(cj):
        return [pl.BlockSpec((1, T, hs * DH), lambda b, h, j: (b, cj(j), h)),
                pl.BlockSpec((1, hs * DH, T), lambda b, h, j: (b, h, cj(j))),
                pl.BlockSpec((1, T, hs * DH), lambda b, h, j: (b, cj(j), h)),
                pl.BlockSpec((1, T, 16), lambda b, h, j: (b, cj(j), 0)),
                pl.BlockSpec((1, 16, T), lambda b, h, j: (b, 0, cj(j)))]

    state = [pltpu.VMEM((DH, DH), F32), pltpu.VMEM((DH, DH), BF16),
             pltpu.VMEM((DH, 128), F32), pltpu.VMEM((DH, 128), BF16), pltpu.VMEM((8, 128), F32)]
    out = jax.ShapeDtypeStruct((B, L, D_A), BF16)
    return pl.pallas_call(
        _scan_kernel,
        out_shape=[out, out],
        grid=(B, A_HEADS // hs, nc),
        in_specs=specs(fwd) + specs(bwd) + [
            pl.BlockSpec((2, 1, hs, DH, DH), lambda b, h, j: (0, b, h, 0, 0)),
            pl.BlockSpec((2, 1, hs, DH, 128), lambda b, h, j: (0, b, h, 0, 0)),
            pl.BlockSpec((2, 1, hs, 8, 128), lambda b, h, j: (0, b, h, 0, 0))],
        out_specs=[pl.BlockSpec((1, T, hs * DH), lambda b, h, j: (b, fwd(j), h)),
                   pl.BlockSpec((1, T, hs * DH), lambda b, h, j: (b, bwd(j), h))],
        scratch_shapes=state * (2 * hs),
        compiler_params=_cp(("parallel", "parallel", "arbitrary")),
        name="scan",
    )(q, kt, v, gt, gl, q, kt, v, gt, gl, c0, n0, m0)


def _store_permuted(o_ref, lead, x, s_ref):
    T, C = x.shape
    for c in range(C // LANES):
        s_ref[c] = x[:, c * LANES:(c + 1) * LANES]
    for c in range(C // LANES):
        for n2 in range(N2):
            o_ref[lead + (n2, slice(None), slice(c * LANES, (c + 1) * LANES))] = (
                s_ref[c, pl.ds(n2, T // N2, stride=N2), :].astype(o_ref.dtype))


def _hyconv_kernel(prev_ref, hy_ref, next_ref, cw_ref, cb_ref, sh_ref, up_ref, x0_ref, s_ref):
    p, n = _halo_rows(prev_ref, next_ref)
    y = _conv3(hy_ref[0], p, n, cw_ref, cb_ref, sh_ref)
    x0_ref[0] = y[:, 0:D_B].astype(BF16)
    _store_permuted(up_ref, (0,), y[:, D_B:2 * D_B] * y[:, 2 * D_B:3 * D_B], s_ref)


def _hyconv(hy, cw, cb, *, tm):
    B, L, C = hy.shape
    return pl.pallas_call(
        _hyconv_kernel,
        out_shape=[jax.ShapeDtypeStruct((B, N2, L // N2, D_B), BF16), jax.ShapeDtypeStruct((B, L, D_B), BF16)],
        grid=(B, L // tm),
        in_specs=_halo_specs(tm, L, C) + [_const_spec((3, C)), _const_spec((1, C)), _const_spec((2 * BD, BD))],
        out_specs=[pl.BlockSpec((1, N2, tm // N2, D_B), lambda b, i: (b, 0, i, 0)),
                   pl.BlockSpec((1, tm, D_B), lambda b, i: (b, i, 0))],
        scratch_shapes=[pltpu.VMEM((D_B // LANES, tm, LANES), F32)],
        compiler_params=_cp(("parallel", "parallel")),
        name="hyconv",
    )(hy, hy, hy, cw, cb.reshape(1, C), _shift_matrix())


def _filt_kernel(w1_ref, b1_ref, w2_ref, b2_ref, w3_ref, b3_ref, w4_ref, fr_ref, o_ref, s_ref, *, L):
    i = pl.program_id(0)
    T = s_ref.shape[1]
    pos = (lax.broadcasted_iota(jnp.int32, (T, 128), 0) + i * T).astype(F32)
    lane = lax.broadcasted_iota(jnp.int32, (T, 128), 1)
    t = pos / (L - 1.0)
    w = (2.0 * math.pi) * pos / L
    band = jnp.where(lane <= FILTER_BANDS, lane - 1, lane - 1 - FILTER_BANDS).astype(F32)
    f = 1e-4 + band * ((FILTER_BANDS - 1 - 1e-4) / (FILTER_BANDS - 1))
    z = jnp.where(lane == 0, t,
                  jnp.where(lane <= FILTER_BANDS, jnp.cos(f * w),
                            jnp.where(lane <= 2 * FILTER_BANDS, -jnp.sin(f * w), 0.0)))
    fr = fr_ref[...]
    a = jnp.sin(fr * (jnp.dot(z, w1_ref[...], preferred_element_type=F32, precision=HIGHEST) + b1_ref[...]))
    a = jnp.sin(fr * (jnp.dot(a, w2_ref[...], preferred_element_type=F32, precision=HIGHEST) + b2_ref[...]))
    a = jnp.sin(fr * (jnp.dot(a, w3_ref[...], preferred_element_type=F32, precision=HIGHEST) + b3_ref[...]))
    w4 = w4_ref[...]
    a_hi, w_hi = a.astype(BF16), w4.astype(BF16)
    a_lo, w_lo = (a - a_hi.astype(F32)).astype(BF16), (w4 - w_hi.astype(F32)).astype(BF16)
    hf = (jnp.dot(a_hi, w_hi, preferred_element_type=F32) + jnp.dot(a_lo, w_hi, preferred_element_type=F32)
          + jnp.dot(a_hi, w_lo, preferred_element_type=F32))
    ch = lax.broadcasted_iota(jnp.int32, (1, D_B), 1).astype(F32)
    deltas = jnp.abs(MIN_DECAY + ch * ((MAX_DECAY - MIN_DECAY) / (D_B - 1)))
    window = jnp.exp(-t[:, 0:1] * deltas) + FILTER_SHIFT
    _store_permuted(o_ref, (0,), hf[:, 0:D_B] * window, s_ref)
    _store_permuted(o_ref, (1,), jnp.where(pos[:, 0:1] == 0.0, 0.0, hf[:, D_B:2 * D_B] * window), s_ref)


def _filters(L, w1, b1, w2, b2, w3, b3, w4, freq, *, tm):
    emb = w1.shape[0]
    w1p = jnp.zeros((128, FILTER_HIDDEN), F32).at[0:emb].set(w1)
    row = lambda a: a.reshape(1, -1)
    return pl.pallas_call(
        functools.partial(_filt_kernel, L=L),
        out_shape=jax.ShapeDtypeStruct((2, N2, L // N2, D_B), BF16),
        grid=(L // tm,),
        in_specs=[_const_spec((128, FILTER_HIDDEN)), _const_spec((1, FILTER_HIDDEN)),
                  _const_spec((FILTER_HIDDEN, FILTER_HIDDEN)), _const_spec((1, FILTER_HIDDEN)),
                  _const_spec((FILTER_HIDDEN, FILTER_HIDDEN)), _const_spec((1, FILTER_HIDDEN)),
                  _const_spec((FILTER_HIDDEN, 2 * D_B)), _const_spec((1, FILTER_HIDDEN))],
        out_specs=pl.BlockSpec((2, N2, tm // N2, D_B), lambda i: (0, 0, i, 0)),
        scratch_shapes=[pltpu.VMEM((D_B // LANES, tm, LANES), F32)],
        compiler_params=_cp(("parallel",)),
        name="filt",
    )(w1p, row(b1), w2, row(b2), w3, row(b3), w4, row(freq))


def _dft_tables():
    half = N1 // 2
    g, ri, kl = np.meshgrid(np.arange(NG), np.arange(2), np.arange(KPG), indexing="ij")
    k1 = (KPG * g + kl).reshape(-1).astype(np.float64)
    is_im = ri.reshape(-1).astype(bool)
    valid = k1 <= half
    n1 = np.arange(half, dtype=np.float64)
    th = 2.0 * np.pi * np.outer(k1, n1) / N1
    f1 = np.where(is_im[:, None], -np.sin(th), np.cos(th)) * valid[:, None]
    ck = np.where((k1 == 0) | (k1 == half), 1.0, 2.0) * valid
    g3 = (np.where(is_im[:, None], -np.sin(th), np.cos(th)) * ck[:, None] / NFFT).T
    mf = np.zeros((NG, BD, BD))
    mb = np.zeros((NG, BD, BD))
    a = np.arange(N2, dtype=np.float64)
    for gi in range(NG):
        for k in range(KPG):
            kk = KPG * gi + k
            if kk > half:
                continue
            phi = 2.0 * np.pi * (np.outer(a, a) / N2 + np.outer(np.ones(N2), a) * kk / NFFT)
            cr, ci = np.cos(phi), -np.sin(phi)
            r0, r1 = k * N2, BD // 2 + k * N2
            mf[gi, r0:r0 + N2, r0:r0 + N2] = cr
            mf[gi, r0:r0 + N2, r1:r1 + N2] = -ci
            mf[gi, r1:r1 + N2, r0:r0 + N2] = ci
            mf[gi, r1:r1 + N2, r1:r1 + N2] = cr
            br, bi = np.cos(phi).T, np.sin(phi).T
            mb[gi, r0:r0 + N2, r0:r0 + N2] = br
            mb[gi, r0:r0 + N2, r1:r1 + N2] = -bi
            mb[gi, r1:r1 + N2, r0:r0 + N2] = bi
            mb[gi, r1:r1 + N2, r1:r1 + N2] = br
    f = lambda t: jnp.asarray(t, dtype=F32)
    return f(f1), f(g3[:, :NYQ]), f(g3[:, NYQ:NYQ + 1]), f(mf), f(mb)


def _stage1(sig, f1_ref, z_s):
    for j in range(N2):
        zj = jnp.dot(f1_ref[...], sig(j), preferred_element_type=F32)
        for c in range(z_s.shape[0]):
            z_s[c, pl.ds(j, F1R, stride=N2), :] = zj[:, c * LANES:(c + 1) * LANES]


def _group_rows(z_s, r0):
    return jnp.concatenate([z_s[c, pl.ds(r0, BD), :] for c in range(z_s.shape[0])], axis=1)


def _fspec_kernel(hp_ref, f1_ref, mf_ref, kf_ref, z_s):
    half = BD // 2
    for s in range(2):
        _stage1(lambda j: hp_ref[s, j], f1_ref, z_s)

        def body(g, carry):
            r0 = pl.multiple_of(g * BD, BD)
            x = jnp.dot(mf_ref[g], _group_rows(z_s, r0).astype(BF16), preferred_element_type=F32)
            if s == 0:
                kf_ref[pl.ds(r0, BD), :] = x
            else:
                kf_ref[pl.ds(r0, half), :] += x[0:half]
                kf_ref[pl.ds(r0 + half, half), :] -= x[half:BD]
            return carry

        lax.fori_loop(0, NG, body, 0, unroll=GROUP_UNROLL)


def _fspec(hp, f1b, mfb):
    C = hp.shape[-1]
    half = N1 // 2
    return pl.pallas_call(
        _fspec_kernel,
        out_shape=jax.ShapeDtypeStruct((ZR, C), F32),
        grid=(C // CT,),
        in_specs=[pl.BlockSpec((2, N2, half, CT), lambda t: (0, 0, 0, t)),
                  _const_spec((F1R, half)), _const_spec((NG, BD, BD))],
        out_specs=pl.BlockSpec((ZR, CT), lambda t: (0, t)),
        scratch_shapes=[pltpu.VMEM((CT // LANES, ZR, LANES), F32)],
        compiler_params=_cp(("parallel",)),
        name="fspec",
    )(hp, f1b, mfb)


def _lconv_kernel(up_ref, x0_ref, kf_ref, f1_ref, mf_ref, mb_ref, g3_ref, g3n_ref, ds_ref, o_ref, z_s, y_s):
    half = BD // 2
    nsl = z_s.shape[0]
    _stage1(lambda j: up_ref[0, j], f1_ref, z_s)

    def body(g, carry):
        r0 = pl.multiple_of(g * BD, BD)
        x = jnp.dot(mf_ref[g], _group_rows(z_s, r0).astype(BF16), preferred_element_type=F32)
        kf = kf_ref[pl.ds(r0, BD), :]
        xr, xi = x[0:half], x[half:BD]
        kr, ki = kf[0:half], kf[half:BD]
        y = jnp.concatenate([xr * kr - xi * ki, xr * ki + xi * kr], axis=0)
        v = jnp.dot(mb_ref[g], y.astype(BF16), preferred_element_type=F32)
        for c in range(nsl):
            z_s[c, pl.ds(r0, BD), :] = v[:, c * LANES:(c + 1) * LANES]
        return carry

    lax.fori_loop(0, NG, body, 0, unroll=GROUP_UNROLL)

    for j in range(N2):
        vj = jnp.concatenate([z_s[c, pl.ds(j, F1R, stride=N2), :] for c in range(nsl)], axis=1)
        yj = (jnp.dot(g3_ref[...], vj[0:NYQ].astype(BF16), preferred_element_type=F32)
              + g3n_ref[...] * vj[NYQ:NYQ + 1] + up_ref[0, j].astype(F32) * ds_ref[...])
        for c in range(nsl):
            y_s[c, pl.ds(j, N1 // 2, stride=N2), :] = yj[:, c * LANES:(c + 1) * LANES]
    for c in range(nsl):
        sl = slice(c * LANES, (c + 1) * LANES)
        o_ref[0, :, sl] = (x0_ref[0, :, sl].astype(F32) * y_s[c]).astype(o_ref.dtype)


def _lconv(up, x0, kf, f1b, mfb, mbb, g3b, g3n, dskip):
    B, L, C = x0.shape
    half = N1 // 2
    return pl.pallas_call(
        _lconv_kernel,
        out_shape=jax.ShapeDtypeStruct((B, L, C), BF16),
        grid=(C // CT, B),
        in_specs=[pl.BlockSpec((1, N2, half, CT), lambda t, b: (b, 0, 0, t)),
                  pl.BlockSpec((1, L, CT), lambda t, b: (b, 0, t)),
                  pl.BlockSpec((ZR, CT), lambda t, b: (0, t)),
                  _const_spec((F1R, half)), _const_spec((NG, BD, BD)), _const_spec((NG, BD, BD)),
                  _const_spec((half, NYQ)), _const_spec((half, 1)),
                  pl.BlockSpec((1, CT), lambda t, b: (0, t))],
        out_specs=pl.BlockSpec((1, L, CT), lambda t, b: (b, 0, t)),
        scratch_shapes=[pltpu.VMEM((CT // LANES, ZR, LANES), F32), pltpu.VMEM((CT // LANES, L, LANES), F32)],
        compiler_params=_cp(("parallel", "parallel")),
        name="lconv",
    )(up, x0, kf, f1b, mfb, mbb, g3b, g3n, dskip.reshape(1, C))


def _merge_kernel(hf_ref, hb_ref, xc_ref, z_ref, yb_ref, gab_ref, x_ref, mod_ref, ng_ref, sk_ref,
                  wpa_hbm, wpb_hbm, wo_hbm, o_ref, wpa_ref, wpb_ref, wo_ref, w_stage, w_sem):
    @pl.when(_first_step())
    def _():
        _load_cast(wpa_hbm, wpa_ref, w_stage, w_sem)
        _load_cast(wpb_hbm, wpb_ref, w_stage, w_sem)
        _load_cast(wo_hbm, wo_ref, w_stage, w_sem)

    h = hf_ref[0].astype(F32) + hb_ref[0].astype(F32)
    xc = xc_ref[0].astype(F32)
    parts = []
    for k in range(A_HEADS):
        sl = slice(k * DH, (k + 1) * DH)
        hh = h[:, sl]
        hn = hh * lax.rsqrt(jnp.mean(hh * hh, axis=-1, keepdims=True) + EPS)
        parts.append(hn * ng_ref[:, sl] + sk_ref[:, sl] * xc[:, sl])
    ya = jax.nn.sigmoid(z_ref[0].astype(F32)) * jnp.concatenate(parts, axis=1)
    gab = gab_ref[0].astype(F32)
    mix = (jax.nn.sigmoid(gab[:, 0:D_MODEL]) * jnp.dot(ya.astype(BF16), wpa_ref[...], preferred_element_type=F32)
           + jax.nn.sigmoid(gab[:, D_MODEL:]) * jnp.dot(yb_ref[0], wpb_ref[...], preferred_element_type=F32))
    out = jnp.dot(mix.astype(BF16), wo_ref[...], preferred_element_type=F32)
    o_ref[0] = x_ref[0] + mod_ref[0, 5:6, :] * out


def _merge(hf, hb, xc, z, yb, gab, x, mods, a_norm_g, a_skip, wpa, wpb, wo, *, tm):
    B, L, D = x.shape
    tok = lambda w: pl.BlockSpec((1, tm, w), lambda b, i: (b, i, 0))
    return pl.pallas_call(
        _merge_kernel,
        out_shape=jax.ShapeDtypeStruct((B, L, D), F32),
        grid=(B, L // tm),
        in_specs=[tok(D_A), tok(D_A),
                  tok(D_A), tok(D_A), tok(D_B), tok(2 * D_MODEL), tok(D),
                  pl.BlockSpec((1, N_MOD, D), lambda b, i: (b, 0, 0)),
                  _const_spec((1, D_A)), _const_spec((1, D_A)), ANY_SPEC, ANY_SPEC, ANY_SPEC],
        out_specs=tok(D),
        scratch_shapes=[pltpu.VMEM((D_A, D), BF16), pltpu.VMEM((D_B, D), BF16), pltpu.VMEM((D, D), BF16),
                        pltpu.VMEM((2, BD, D), F32), pltpu.SemaphoreType.DMA((2,))],
        compiler_params=_cp(("arbitrary", "arbitrary")),
        name="merge",
    )(hf, hb, xc, z, yb, gab, x, mods, a_norm_g.reshape(1, D_A), a_skip.reshape(1, D_A), wpa, wpb, wo)


def kernel(x, c, ctx, c_ctx, w_ada, b_ada, norm_g, ffn1_up, ffn1_down, ffn2_up, ffn2_down, w_in, a_conv_w, a_conv_b, a_wq, a_wk, a_wv, a_w_gate, a_b_gate, a_norm_g, a_skip, b_conv_w, b_conv_b, b_filt_w1, b_filt_b1, b_filt_w2, b_filt_b2, b_filt_w3, b_filt_b3, b_filt_w4, b_filt_freq, b_skip, w_pa, w_pb, w_out, final_g):
    B, L, D = x.shape
    Lc = ctx.shape[1]
    assert w_ada.shape[0] == 1, "single-layer stack"
    assert 2 * L == NFFT and D == D_MODEL and L // GRID_W == GRID_W
    TM = 512
    FFN_TM = 1024

    c8 = jnp.zeros((8, D), F32).at[0:B].set(c).at[B].set(c_ctx)
    mods = _mods(c8, w_ada[0], b_ada[0]).reshape(8, N_MOD, D)
    ng = norm_g[0]
    row_b = lambda b: b
    row_ctx = lambda b: B

    up1 = ffn1_up[0].astype(BF16)
    dn1 = ffn1_down[0].astype(BF16)
    w_xm_b = w_in[0][:, 0:D_A].astype(BF16)

    x1 = _ffn(x, mods, row_b, ng, up1, dn1, sub=0, tm=FFN_TM, pos=_pos_table())
    ctx1 = _ffn(ctx.reshape(1, B * Lc, D), mods, row_ctx, ng, up1, dn1, sub=0, tm=TM)

    widths = (D_A, D_A, 3 * D_B, 2 * D_MODEL)
    xm, z, hy, gab = _proj(x1, mods, row_b, ng, w_in[0], widths, tm=TM)
    (xm_c,) = _proj(ctx1, mods, row_ctx, ng, w_xm_b, (D_A,), tm=TM)

    bq, bkt, bv, gqk, gv = _qkvw(a_wq[0], a_wk[0], a_wv[0], a_w_gate[0])
    fw = (a_conv_w[0], a_conv_b[0], bq, bkt, bv, gqk, gv, a_b_gate[0])
    q, xc, kt, v, g, gl = _feat(xm, *fw, tm=TM)
    kt_c, v_c, g_c, gl_c = _feat(xm_c.reshape(B, Lc, D_A), *fw, tm=Lc, with_q=False)

    c0, n0, m0 = _ctxstate(kt_c, v_c, g_c, gl_c)
    hf, hb = _scan(q, kt, v, g, gl, c0, n0, m0)

    up, x0c = _hyconv(hy, b_conv_w[0], b_conv_b[0], tm=TM)
    f1, g3, g3n, mf, mb = _dft_tables()
    f1b, g3b, mfb, mbb = (a.astype(BF16) for a in (f1, g3, mf, mb))
    hp = _filters(L, b_filt_w1[0], b_filt_b1[0], b_filt_w2[0], b_filt_b2[0], b_filt_w3[0], b_filt_b3[0],
                  b_filt_w4[0], b_filt_freq[0], tm=TM)
    kf = _fspec(hp, f1b, mfb)
    yb = _lconv(up, x0c, kf, f1b, mfb, mbb, g3b, g3n, b_skip[0])

    x2 = _merge(hf, hb, xc, z, yb, gab, x1, mods, a_norm_g[0], a_skip[0],
                w_pa[0], w_pb[0], w_out[0], tm=TM)

    return _ffn(x2, mods, row_b, ng, ffn2_up[0], ffn2_down[0], sub=2, tm=FFN_TM, final_g=final_g)
```

```python
import functools
import math

import numpy as np
import jax
import jax.numpy as jnp
from jax import lax
from jax.experimental import pallas as pl
from jax.experimental.pallas import tpu as pltpu

F32 = jnp.float32
BF16 = jnp.bfloat16

D_MODEL = 1024
D_A = 2048
A_HEADS = 4
DH = D_A // A_HEADS
QKV_BLOCK = 4
D_B = 1024
D_FF = 2816
EPS = 1e-6
N_MOD = 9
GRID_W = 64
CHUNK = 256
HEADS_PER_STEP = 2
SCAN_BLOCK = 2
FILTER_BANDS = 16
FILTER_HIDDEN = 64
DECAY_TARGET = 1e-2
MAX_DECAY = math.log(DECAY_TARGET) / 0.3
MIN_DECAY = math.log(DECAY_TARGET) / 1.5
FILTER_SHIFT = 0.05

LANES = 128
SUBLANES = 8
BD = 256

N2 = SUBLANES
NFFT = 8192
N1 = NFFT // N2
KPG = BD // (2 * N2)
NG = (N1 // 2) // KPG + 1
F1R = NG * 2 * KPG
ZR = F1R * N2
NYQ = (NG - 1) * 2 * KPG
CT = 256
GROUP_UNROLL = 11

VMEM_LIMIT = 56 * 1024 * 1024
HIGHEST = lax.Precision.HIGHEST
NEG = -1e30


def _cp(sem):
    return pltpu.CompilerParams(dimension_semantics=sem, vmem_limit_bytes=VMEM_LIMIT)


def _const_spec(shape):
    nd = len(shape)
    return pl.BlockSpec(shape, lambda *_: (0,) * nd, pipeline_mode=pl.Buffered(1))


def _silu(x):
    return x * jax.nn.sigmoid(x)


def _rms(x, g):
    return x * lax.rsqrt(jnp.mean(x * x, axis=-1, keepdims=True) + EPS) * g


def _mods_kernel(c_ref, w_ref, b_ref, o_ref):
    cs = _silu(c_ref[...])
    o_ref[...] = jnp.dot(cs, w_ref[...], preferred_element_type=F32, precision=HIGHEST) + b_ref[...]


def _mods(c8, w_ada, b_ada):
    n = w_ada.shape[1]
    tn = 1024
    return pl.pallas_call(
        _mods_kernel,
        out_shape=jax.ShapeDtypeStruct((8, n), F32),
        grid=(n // tn,),
        in_specs=[pl.BlockSpec((8, D_MODEL), lambda j: (0, 0)),
                  pl.BlockSpec((D_MODEL, tn), lambda j: (0, j)),
                  pl.BlockSpec((1, tn), lambda j: (0, j))],
        out_specs=pl.BlockSpec((8, tn), lambda j: (0, j)),
        compiler_params=_cp(("parallel",)),
        name="mods",
    )(c8, w_ada, b_ada.reshape(1, n))


def _postab_kernel(om_ref, o_ref):
    nf = om_ref.shape[1]
    p = lax.broadcasted_iota(jnp.int32, (GRID_W, nf), 0).astype(F32)
    a = p * om_ref[...]
    o_ref[:, 0:nf] = jnp.sin(a)
    o_ref[:, nf:2 * nf] = jnp.cos(a)


def _pos_table():
    nf = D_MODEL // 4
    omega = 1.0 / (10000.0 ** (jnp.arange(nf, dtype=F32) / nf))
    return pl.pallas_call(
        _postab_kernel,
        out_shape=jax.ShapeDtypeStruct((GRID_W, 2 * nf), F32),
        name="postab",
    )(omega.reshape(1, nf))


FF_CHUNKS = ((0, 6 * BD), (6 * BD, D_FF))


def _ffn_kernel(*refs, sub, has_pos, final):
    it = iter(refs)
    x_ref = next(it)
    pos_ref = next(it) if has_pos else None
    mod_ref = next(it)
    g_ref = next(it)
    up_ref = next(it)
    dn_ref = next(it)
    fg_ref = next(it) if final else None
    o_ref = next(it)

    x = x_ref[0]
    if has_pos:
        tm = x.shape[0]
        per = tm // GRID_W
        half = pos_ref.shape[1]
        i = pl.program_id(1)
        ecol = pos_ref[...]
        rows = []
        for q in range(per):
            erow = jnp.broadcast_to(pos_ref[pl.ds(i * per + q, 1), :], (GRID_W, half))
            rows.append(jnp.concatenate([erow, ecol], axis=1))
        x = x + jnp.concatenate(rows, axis=0)
    shift = mod_ref[0, 3 * sub:3 * sub + 1, :]
    scale = mod_ref[0, 3 * sub + 1:3 * sub + 2, :]
    gate = mod_ref[0, 3 * sub + 2:3 * sub + 3, :]
    h = _rms(x, g_ref[sub:sub + 1, :]) * (1.0 + scale) + shift
    hb = h.astype(BF16)
    acc = jnp.zeros(x.shape, F32)
    for lo, hi in FF_CHUNKS:
        gg = jnp.dot(hb, up_ref[:, lo:hi], preferred_element_type=F32)
        uu = jnp.dot(hb, up_ref[:, D_FF + lo:D_FF + hi], preferred_element_type=F32)
        a = (_silu(gg) * uu).astype(BF16)
        acc = acc + jnp.dot(a, dn_ref[lo:hi, :], preferred_element_type=F32)
    y = x + 0.5 * gate * acc
    if final:
        y = _rms(y, fg_ref[...])
    o_ref[0] = y


def _ffn(x, mods, mod_row, norm_g, up_b, dn_b, *, sub, tm, pos=None, final_g=None):
    B, L, D = x.shape
    args = [x]
    specs = [pl.BlockSpec((1, tm, D), lambda b, i: (b, i, 0))]
    if pos is not None:
        args.append(pos)
        specs.append(_const_spec(pos.shape))
    args += [mods, norm_g, up_b, dn_b]
    specs += [pl.BlockSpec((1, N_MOD, D), lambda b, i: (mod_row(b), 0, 0)),
              _const_spec((3, D)), _const_spec((D, 2 * D_FF)), _const_spec((D_FF, D))]
    if final_g is not None:
        args.append(final_g.reshape(1, D))
        specs.append(_const_spec((1, D)))
    return pl.pallas_call(
        functools.partial(_ffn_kernel, sub=sub, has_pos=pos is not None, final=final_g is not None),
        out_shape=jax.ShapeDtypeStruct((B, L, D), F32),
        grid=(B, L // tm),
        in_specs=specs,
        out_specs=pl.BlockSpec((1, tm, D), lambda b, i: (b, i, 0)),
        compiler_params=_cp(("parallel", "parallel")),
        name=f"ffn{sub}",
    )(*args)


def _proj_kernel(x_ref, mod_ref, g_ref, w_ref, *o_refs, widths):
    x = x_ref[0]
    shift = mod_ref[0, 3:4, :]
    scale = mod_ref[0, 4:5, :]
    hb = (_rms(x, g_ref[1:2, :]) * (1.0 + scale) + shift).astype(BF16)
    lo = 0
    for o_ref, w in zip(o_refs, widths):
        o_ref[0] = jnp.dot(hb, w_ref[:, lo:lo + w], preferred_element_type=F32).astype(o_ref.dtype)
        lo += w


def _proj(x, mods, mod_row, norm_g, w_b, widths, *, tm):
    B, L, D = x.shape
    n = sum(widths)
    return pl.pallas_call(
        functools.partial(_proj_kernel, widths=widths),
        out_shape=[jax.ShapeDtypeStruct((B, L, w), BF16) for w in widths],
        grid=(B, L // tm),
        in_specs=[pl.BlockSpec((1, tm, D), lambda b, i: (b, i, 0)),
                  pl.BlockSpec((1, N_MOD, D), lambda b, i: (mod_row(b), 0, 0)),
                  _const_spec((3, D)), _const_spec((D, n))],
        out_specs=[pl.BlockSpec((1, tm, w), lambda b, i: (b, i, 0)) for w in widths],
        compiler_params=_cp(("parallel", "parallel")),
        name="proj",
    )(x, mods, norm_g, w_b)


HALO = 16


def _shift_matrix():
    return jnp.concatenate([jnp.eye(BD, k=-1, dtype=BF16), jnp.eye(BD, k=1, dtype=BF16)], axis=0)


def _conv3(xb, prev_row, next_row, w_ref, b_ref, s_ref):
    T, C = xb.shape
    w0, w1, w2, b = w_ref[0:1, :], w_ref[1:2, :], w_ref[2:3, :], b_ref[...]
    x = xb.astype(F32)
    r8 = lax.broadcasted_iota(jnp.int32, (SUBLANES, C), 0)
    out = []
    for r0 in range(0, T, BD):
        r1 = r0 + BD
        sh = jnp.dot(s_ref[...], xb[r0:r1], preferred_element_type=F32)
        y = sh[0:BD] * w0 + x[r0:r1] * w1 + sh[BD:2 * BD] * w2 + b
        before = prev_row if r0 == 0 else x[r0 - 1:r0]
        after = next_row if r1 == T else x[r1:r1 + 1]
        first = before * w0 + x[r0:r0 + 1] * w1 + x[r0 + 1:r0 + 2] * w2 + b
        last = x[r1 - 2:r1 - 1] * w0 + x[r1 - 1:r1] * w1 + after * w2 + b
        top = jnp.where(r8 == 0, first, y[0:SUBLANES])
        bot = jnp.where(r8 == SUBLANES - 1, last, y[BD - SUBLANES:BD])
        out += [top, y[SUBLANES:BD - SUBLANES], bot]
    return jnp.concatenate(out, axis=0)


def _halo_rows(prev_ref, next_ref):
    i = pl.program_id(1)
    n = pl.num_programs(1)
    p = prev_ref[0, HALO - 1:HALO, :].astype(F32)
    q = next_ref[0, 0:1, :].astype(F32)
    p = jnp.where(i == 0, 0.0, p)
    q = jnp.where(i == n - 1, 0.0, q)
    return p, q


def _halo_specs(tm, L, C):
    r = tm // HALO
    nb = L // HALO
    return [pl.BlockSpec((1, HALO, C), lambda b, i: (b, jnp.maximum(i * r - 1, 0), 0)),
            pl.BlockSpec((1, tm, C), lambda b, i: (b, i, 0)),
            pl.BlockSpec((1, HALO, C), lambda b, i: (b, jnp.minimum((i + 1) * r, nb - 1), 0))]


def _qkvw_kernel(wq_ref, wk_ref, wv_ref, wg_ref, bq_ref, bkt_ref, bv_ref, gqk_ref, gv_ref):
    r = lax.broadcasted_iota(jnp.int32, (BD, BD), 0)
    c = lax.broadcasted_iota(jnp.int32, (BD, BD), 1)
    shift = QKV_BLOCK.bit_length() - 1
    same_block = (r >> shift) == (c >> shift)
    col_in_block = c & (QKV_BLOCK - 1)

    def tile(w_ref):
        w2 = w_ref[0]
        a = jnp.zeros((BD, BD), F32)
        for j in range(QKV_BLOCK):
            a = a + jnp.where(col_in_block == j, w2[:, j:j + 1], 0.0)
        return jnp.where(same_block, a, 0.0)

    tq = tile(wq_ref)
    tk = tile(wk_ref)
    tv = tile(wv_ref)
    bq_ref[0] = tq.astype(BF16)
    bkt_ref[0] = tk.T.astype(BF16)
    bv_ref[0] = tv.astype(BF16)
    gqk_ref[...] = (jnp.dot(tq, wg_ref[0], preferred_element_type=F32, precision=HIGHEST)
                    + jnp.dot(tk, wg_ref[1], preferred_element_type=F32, precision=HIGHEST)).astype(BF16)
    gv_ref[...] = jnp.dot(tv, wg_ref[2], preferred_element_type=F32, precision=HIGHEST).astype(BF16)


def _qkvw(wq, wk, wv, w_gate):
    nb = D_A // BD
    w2 = lambda w: w.reshape(nb, BD, QKV_BLOCK)
    wspec = pl.BlockSpec((1, BD, QKV_BLOCK), lambda j: (j, 0, 0))
    tspec = pl.BlockSpec((1, BD, BD), lambda j: (j, 0, 0))
    gspec = pl.BlockSpec((BD, 16), lambda j: (j, 0))
    tile = jax.ShapeDtypeStruct((nb, BD, BD), BF16)
    gw = jax.ShapeDtypeStruct((D_A, 16), BF16)
    return pl.pallas_call(
        _qkvw_kernel,
        out_shape=[tile, tile, tile, gw, gw],
        grid=(nb,),
        in_specs=[wspec, wspec, wspec, pl.BlockSpec((3, BD, 16), lambda j: (0, j, 0))],
        out_specs=[tspec, tspec, tspec, gspec, gspec],
        compiler_params=_cp(("parallel",)),
        name="qkvw",
    )(w2(wq), w2(wk), w2(wv), w_gate.reshape(3, D_A, 16))


def _feat_kernel(prev_ref, xm_ref, next_ref, cw_ref, cb_ref, sh_ref, bq_ref, bkt_ref, bv_ref, gqk_ref, gv_ref,
                 gb_ref, *o_refs, with_q):
    if with_q:
        q_ref, xc_ref, kt_ref, v_ref, g_ref, gl_ref = o_refs
    else:
        kt_ref, v_ref, g_ref, gl_ref = o_refs
    xmb = xm_ref[0]
    p, n = _halo_rows(prev_ref, next_ref)
    xc = _silu(_conv3(xmb, p, n, cw_ref, cb_ref, sh_ref))
    xcb = xc.astype(BF16)
    if with_q:
        xc_ref[0] = xcb
    g = (jnp.dot(xcb, gqk_ref[...], preferred_element_type=F32)
         + jnp.dot(xmb, gv_ref[...], preferred_element_type=F32) + gb_ref[...])
    g_ref[0] = g
    gl_ref[0] = g.T
    for j in range(D_A // BD):
        sl = slice(j * BD, (j + 1) * BD)
        if with_q:
            q = jnp.dot(xcb[:, sl], bq_ref[j], preferred_element_type=F32)
            q_ref[0, :, sl] = (q * (DH ** -0.5)).astype(BF16)
        kt_ref[0, sl, :] = lax.dot_general(bkt_ref[j], xcb[:, sl], (((1,), (1,)), ((), ())),
                                          preferred_element_type=F32).astype(BF16)
        v_ref[0, :, sl] = jnp.dot(xmb[:, sl], bv_ref[j], preferred_element_type=F32).astype(BF16)


def _feat(xm, cw, cb, bq, bkt, bv, gqk, gv, gb, *, tm, with_q=True):
    B, L, _ = xm.shape
    nb = D_A // BD
    tok = jax.ShapeDtypeStruct((B, L, D_A), BF16)
    tok_spec = pl.BlockSpec((1, tm, D_A), lambda b, i: (b, i, 0))
    shapes = [jax.ShapeDtypeStruct((B, D_A, L), BF16), tok, jax.ShapeDtypeStruct((B, L, 16), F32),
              jax.ShapeDtypeStruct((B, 16, L), F32)]
    specs = [pl.BlockSpec((1, D_A, tm), lambda b, i: (b, 0, i)), tok_spec,
             pl.BlockSpec((1, tm, 16), lambda b, i: (b, i, 0)), pl.BlockSpec((1, 16, tm), lambda b, i: (b, 0, i))]
    if with_q:
        shapes = [tok, tok] + shapes
        specs = [tok_spec, tok_spec] + specs
    return pl.pallas_call(
        functools.partial(_feat_kernel, with_q=with_q),
        out_shape=shapes,
        grid=(B, L // tm),
        in_specs=_halo_specs(tm, L, D_A) + [
            _const_spec((3, D_A)), _const_spec((1, D_A)), _const_spec((2 * BD, BD)),
            _const_spec((nb, BD, BD)), _const_spec((nb, BD, BD)), _const_spec((nb, BD, BD)),
            _const_spec((D_A, 16)), _const_spec((D_A, 16)), _const_spec((1, 16))],
        out_specs=specs,
        compiler_params=_cp(("parallel", "parallel")),
        name="feat" if with_q else "feat_ctx",
    )(xm, xm, xm, cw, cb.reshape(1, D_A), _shift_matrix(), bq, bkt, bv, gqk, gv, gb.reshape(1, 16))


def _gate_vectors(gt, gl, idx, sign):
    T = gt.shape[0]
    sub = lax.broadcasted_iota(jnp.int32, gl.shape, 0)
    lane = lax.broadcasted_iota(jnp.int32, gt.shape, 1)
    ig_row = jnp.sum(jnp.where(sub == idx, gl, 0.0), axis=0, keepdims=True)
    fg_row = jnp.sum(jnp.where(sub == idx + A_HEADS, gl, 0.0), axis=0, keepdims=True)
    fg_col = jnp.sum(jnp.where(lane == idx + A_HEADS, gt, 0.0), axis=1, keepdims=True)
    lf_row = jax.nn.log_sigmoid(fg_row)
    lf_col = jax.nn.log_sigmoid(fg_col)
    r = lax.broadcasted_iota(jnp.int32, (T, T), 0)
    c = lax.broadcasted_iota(jnp.int32, (T, T), 1)
    mask = sign * (r - c) >= 0
    mask_t = sign * (c - r) >= 0
    b_col = jnp.sum(jnp.where(mask, lf_row, 0.0), axis=1, keepdims=True)
    b_row = jnp.sum(jnp.where(mask_t, lf_col, 0.0), axis=0, keepdims=True)
    b_last = jnp.sum(lf_row, axis=1, keepdims=True)
    return ig_row, b_col, b_row, b_last, mask


def _state_step(kt, v, ig_row, b_row, b_last, m_prev):
    log_w = b_last - b_row + ig_row
    m_new = jnp.maximum(b_last + m_prev, jnp.max(log_w, axis=1, keepdims=True))
    decay = jnp.exp(b_last + m_prev - m_new)
    w = jnp.exp(log_w - m_new)
    kw = kt.astype(F32) * w
    dC = jnp.dot(kw.astype(BF16), v, preferred_element_type=F32)
    dn = jnp.sum(kw, axis=1, keepdims=True)
    return decay, m_new, dC, dn


def _ctxstate_kernel(kt_ref, v_ref, gt_ref, gl_ref, c_ref, n_ref, m_ref):
    hp = pl.program_id(1)
    m0 = jnp.zeros((1, 1), F32)
    for d in range(2):
        for hh in range(HEADS_PER_STEP):
            hsl = slice(hh * DH, (hh + 1) * DH)
            head = hp * HEADS_PER_STEP + hh
            ig_row, _, b_row, b_last, _ = _gate_vectors(gt_ref[0], gl_ref[0], 8 * d + head, 1 - 2 * d)
            _, m_new, dC, dn = _state_step(kt_ref[0, hsl, :], v_ref[0, :, hsl], ig_row, b_row, b_last, m0)
            c_ref[d, 0, hh] = dC
            n_ref[d, 0, hh] = jnp.broadcast_to(dn, (DH, 128))
            m_ref[d, 0, hh] = jnp.broadcast_to(m_new, (8, 128))


def _ctxstate(kt, v, gt, gl):
    B, _, Lc = kt.shape
    H = A_HEADS
    hs = HEADS_PER_STEP
    return pl.pallas_call(
        _ctxstate_kernel,
        out_shape=[jax.ShapeDtypeStruct((2, B, H, DH, DH), F32),
                   jax.ShapeDtypeStruct((2, B, H, DH, 128), F32),
                   jax.ShapeDtypeStruct((2, B, H, 8, 128), F32)],
        grid=(B, H // hs),
        in_specs=[pl.BlockSpec((1, hs * DH, Lc), lambda b, h: (b, h, 0)),
                  pl.BlockSpec((1, Lc, hs * DH), lambda b, h: (b, 0, h)),
                  pl.BlockSpec((1, Lc, 16), lambda b, h: (b, 0, 0)),
                  pl.BlockSpec((1, 16, Lc), lambda b, h: (b, 0, 0))],
        out_specs=[pl.BlockSpec((2, 1, hs, DH, DH), lambda b, h: (0, b, h, 0, 0)),
                   pl.BlockSpec((2, 1, hs, DH, 128), lambda b, h: (0, b, h, 0, 0)),
                   pl.BlockSpec((2, 1, hs, 8, 128), lambda b, h: (0, b, h, 0, 0))],
        compiler_params=_cp(("parallel", "parallel")),
        name="ctxstate",
    )(kt, v, gt, gl)


def _scan_kernel(qf_ref, ktf_ref, vf_ref, gtf_ref, glf_ref, qb_ref, ktb_ref, vb_ref, gtb_ref, glb_ref,
                 c0_ref, n0_ref, m0_ref, hf_ref, hb_ref, *scratch):
    hp = pl.program_id(1)
    j = pl.program_id(2)
    io = ((qf_ref, ktf_ref, vf_ref, gtf_ref, glf_ref, hf_ref), (qb_ref, ktb_ref, vb_ref, gtb_ref, glb_ref, hb_ref))
    chains = [(d, hh) + io[d] + tuple(scratch[5 * (d * HEADS_PER_STEP + hh):5 * (d * HEADS_PER_STEP + hh) + 5])
              for d in range(2) for hh in range(HEADS_PER_STEP)]

    @pl.when(j == 0)
    def _():
        for d, hh, _, _, _, _, _, _, c_s, cq_s, n_s, nq_s, m_s in chains:
            c0 = c0_ref[d, 0, hh]
            c_s[...] = c0
            cq_s[...] = c0.astype(BF16)
            n0 = n0_ref[d, 0, hh]
            n_s[...] = n0
            nq_s[...] = n0.astype(BF16)
            m_s[...] = m0_ref[d, 0, hh]

    for sub, (d, hh, q_ref, kt_ref, v_ref, gt_ref, gl_ref, h_ref, c_s, cq_s, n_s, nq_s, m_s) in (
            (sub, chain) for sub in range(SCAN_BLOCK) for chain in chains):
        sign = 1 - 2 * d
        hsl = slice(hh * DH, (hh + 1) * DH)
        r0 = (sub if d == 0 else SCAN_BLOCK - 1 - sub) * CHUNK
        rows = slice(r0, r0 + CHUNK)
        q = q_ref[0, rows, hsl]
        kt = kt_ref[0, hsl, rows]
        v = v_ref[0, rows, hsl]
        head = hp * HEADS_PER_STEP + hh
        ig_row, b_col, b_row, b_last, mask = _gate_vectors(gt_ref[0, rows, :], gl_ref[0, :, rows], 8 * d + head, sign)
        m_prev = m_s[0:1, 0:1]

        log_d = jnp.where(mask, b_col - b_row + ig_row, NEG)
        m_inter = b_col + m_prev
        m_t = jnp.maximum(m_inter, jnp.max(log_d, axis=1, keepdims=True))
        dmat = jnp.exp(log_d - m_t)
        inter = jnp.exp(m_inter - m_t)
        s = jnp.dot(q, kt, preferred_element_type=F32) * dmat
        num = (jnp.dot(s.astype(BF16), v, preferred_element_type=F32)
               + inter * jnp.dot(q, cq_s[...], preferred_element_type=F32))
        qn = jnp.dot(q, nq_s[...], preferred_element_type=F32)[:, 0:1]
        den = jnp.sum(s, axis=1, keepdims=True) + inter * qn
        h_ref[0, rows, hsl] = (num / jnp.maximum(jnp.abs(den), jnp.exp(-m_t))).astype(h_ref.dtype)

        decay, m_new, dC, dn = _state_step(kt, v, ig_row, b_row, b_last, m_prev)
        c_new = decay * c_s[...] + dC
        c_s[...] = c_new
        cq_s[...] = c_new.astype(BF16)
        n_new = decay * n_s[...] + dn
        n_s[...] = n_new
        nq_s[...] = n_new.astype(BF16)
        m_s[...] = jnp.broadcast_to(m_new, (8, 128))


def _scan(q, kt, v, gt, gl, c0, n0, m0):
    B, L, _ = q.shape
    hs = HEADS_PER_STEP
    T = SCAN_BLOCK * CHUNK
    nc = L // T
    fwd = lambda j: j
    bwd = lambda j: nc - 1 - j

    def specs(cj):
        return [pl.BlockSpec((1, T, hs * DH), lambda b, h, j: (b, cj(j), h)),
                pl.BlockSpec((1, hs * DH, T), lambda b, h, j: (b, h, cj(j))),
                pl.BlockSpec((1, T, hs * DH), lambda b, h, j: (b, cj(j), h)),
                pl.BlockSpec((1, T, 16), lambda b, h, j: (b, cj(j), 0)),
                pl.BlockSpec((1, 16, T), lambda b, h, j: (b, 0, cj(j)))]

    state = [pltpu.VMEM((DH, DH), F32), pltpu.VMEM((DH, DH), BF16),
             pltpu.VMEM((DH, 128), F32), pltpu.VMEM((DH, 128), BF16), pltpu.VMEM((8, 128), F32)]
    out = jax.ShapeDtypeStruct((B, L, D_A), BF16)
    return pl.pallas_call(
        _scan_kernel,
        out_shape=[out, out],
        grid=(B, A_HEADS // hs, nc),
        in_specs=specs(fwd) + specs(bwd) + [
            pl.BlockSpec((2, 1, hs, DH, DH), lambda b, h, j: (0, b, h, 0, 0)),
            pl.BlockSpec((2, 1, hs, DH, 128), lambda b, h, j: (0, b, h, 0, 0)),
            pl.BlockSpec((2, 1, hs, 8, 128), lambda b, h, j: (0, b, h, 0, 0))],
        out_specs=[pl.BlockSpec((1, T, hs * DH), lambda b, h, j: (b, fwd(j), h)),
                   pl.BlockSpec((1, T, hs * DH), lambda b, h, j: (b, bwd(j), h))],
        scratch_shapes=state * (2 * hs),
        compiler_params=_cp(("parallel", "parallel", "arbitrary")),
        name="scan",
    )(q, kt, v, gt, gl, q, kt, v, gt, gl, c0, n0, m0)


def _store_permuted(o_ref, lead, x, s_ref):
    T, C = x.shape
    for c in range(C // LANES):
        s_ref[c] = x[:, c * LANES:(c + 1) * LANES]
    for c in range(C // LANES):
        for n2 in range(N2):
            o_ref[lead + (n2, slice(None), slice(c * LANES, (c + 1) * LANES))] = (
                s_ref[c, pl.ds(n2, T // N2, stride=N2), :].astype(o_ref.dtype))


def _hyconv_kernel(prev_ref, hy_ref, next_ref, cw_ref, cb_ref, sh_ref, up_ref, x0_ref, s_ref):
    p, n = _halo_rows(prev_ref, next_ref)
    y = _conv3(hy_ref[0], p, n, cw_ref, cb_ref, sh_ref)
    x0_ref[0] = y[:, 0:D_B].astype(BF16)
    _store_permuted(up_ref, (0,), y[:, D_B:2 * D_B] * y[:, 2 * D_B:3 * D_B], s_ref)


def _hyconv(hy, cw, cb, *, tm):
    B, L, C = hy.shape
    return pl.pallas_call(
        _hyconv_kernel,
        out_shape=[jax.ShapeDtypeStruct((B, N2, L // N2, D_B), BF16), jax.ShapeDtypeStruct((B, L, D_B), BF16)],
        grid=(B, L // tm),
        in_specs=_halo_specs(tm, L, C) + [_const_spec((3, C)), _const_spec((1, C)), _const_spec((2 * BD, BD))],
        out_specs=[pl.BlockSpec((1, N2, tm // N2, D_B), lambda b, i: (b, 0, i, 0)),
                   pl.BlockSpec((1, tm, D_B), lambda b, i: (b, i, 0))],
        scratch_shapes=[pltpu.VMEM((D_B // LANES, tm, LANES), F32)],
        compiler_params=_cp(("parallel", "parallel")),
        name="hyconv",
    )(hy, hy, hy, cw, cb.reshape(1, C), _shift_matrix())


def _filt_kernel(w1_ref, b1_ref, w2_ref, b2_ref, w3_ref, b3_ref, w4_ref, fr_ref, o_ref, s_ref, *, L):
    i = pl.program_id(0)
    T = s_ref.shape[1]
    pos = (lax.broadcasted_iota(jnp.int32, (T, 128), 0) + i * T).astype(F32)
    lane = lax.broadcasted_iota(jnp.int32, (T, 128), 1)
    t = pos / (L - 1.0)
    w = (2.0 * math.pi) * pos / L
    band = jnp.where(lane <= FILTER_BANDS, lane - 1, lane - 1 - FILTER_BANDS).astype(F32)
    f = 1e-4 + band * ((FILTER_BANDS - 1 - 1e-4) / (FILTER_BANDS - 1))
    z = jnp.where(lane == 0, t,
                  jnp.where(lane <= FILTER_BANDS, jnp.cos(f * w),
                            jnp.where(lane <= 2 * FILTER_BANDS, -jnp.sin(f * w), 0.0)))
    fr = fr_ref[...]
    a = jnp.sin(fr * (jnp.dot(z, w1_ref[...], preferred_element_type=F32, precision=HIGHEST) + b1_ref[...]))
    a = jnp.sin(fr * (jnp.dot(a, w2_ref[...], preferred_element_type=F32, precision=HIGHEST) + b2_ref[...]))
    a = jnp.sin(fr * (jnp.dot(a, w3_ref[...], preferred_element_type=F32, precision=HIGHEST) + b3_ref[...]))
    w4 = w4_ref[...]
    a_hi, w_hi = a.astype(BF16), w4.astype(BF16)
    a_lo, w_lo = (a - a_hi.astype(F32)).astype(BF16), (w4 - w_hi.astype(F32)).astype(BF16)
    hf = (jnp.dot(a_hi, w_hi, preferred_element_type=F32) + jnp.dot(a_lo, w_hi, preferred_element_type=F32)
          + jnp.dot(a_hi, w_lo, preferred_element_type=F32))
    ch = lax.broadcasted_iota(jnp.int32, (1, D_B), 1).astype(F32)
    deltas = jnp.abs(MIN_DECAY + ch * ((MAX_DECAY - MIN_DECAY) / (D_B - 1)))
    window = jnp.exp(-t[:, 0:1] * deltas) + FILTER_SHIFT
    _store_permuted(o_ref, (0,), hf[:, 0:D_B] * window, s_ref)
    _store_permuted(o_ref, (1,), jnp.where(pos[:, 0:1] == 0.0, 0.0, hf[:, D_B:2 * D_B] * window), s_ref)


def _filters(L, w1, b1, w2, b2, w3, b3, w4, freq, *, tm):
    emb = w1.shape[0]
    w1p = jnp.zeros((128, FILTER_HIDDEN), F32).at[0:emb].set(w1)
    row = lambda a: a.reshape(1, -1)
    return pl.pallas_call(
        functools.partial(_filt_kernel, L=L),
        out_shape=jax.ShapeDtypeStruct((2, N2, L // N2, D_B), BF16),
        grid=(L // tm,),
        in_specs=[_const_spec((128, FILTER_HIDDEN)), _const_spec((1, FILTER_HIDDEN)),
                  _const_spec((FILTER_HIDDEN, FILTER_HIDDEN)), _const_spec((1, FILTER_HIDDEN)),
                  _const_spec((FILTER_HIDDEN, FILTER_HIDDEN)), _const_spec((1, FILTER_HIDDEN)),
                  _const_spec((FILTER_HIDDEN, 2 * D_B)), _const_spec((1, FILTER_HIDDEN))],
        out_specs=pl.BlockSpec((2, N2, tm // N2, D_B), lambda i: (0, 0, i, 0)),
        scratch_shapes=[pltpu.VMEM((D_B // LANES, tm, LANES), F32)],
        compiler_params=_cp(("parallel",)),
        name="filt",
    )(w1p, row(b1), w2, row(b2), w3, row(b3), w4, row(freq))


def _dft_tables():
    half = N1 // 2
    g, ri, kl = np.meshgrid(np.arange(NG), np.arange(2), np.arange(KPG), indexing="ij")
    k1 = (KPG * g + kl).reshape(-1).astype(np.float64)
    is_im = ri.reshape(-1).astype(bool)
    valid = k1 <= half
    n1 = np.arange(half, dtype=np.float64)
    th = 2.0 * np.pi * np.outer(k1, n1) / N1
    f1 = np.where(is_im[:, None], -np.sin(th), np.cos(th)) * valid[:, None]
    ck = np.where((k1 == 0) | (k1 == half), 1.0, 2.0) * valid
    g3 = (np.where(is_im[:, None], -np.sin(th), np.cos(th)) * ck[:, None] / NFFT).T
    mf = np.zeros((NG, BD, BD))
    mb = np.zeros((NG, BD, BD))
    a = np.arange(N2, dtype=np.float64)
    for gi in range(NG):
        for k in range(KPG):
            kk = KPG * gi + k
            if kk > half:
                continue
            phi = 2.0 * np.pi * (np.outer(a, a) / N2 + np.outer(np.ones(N2), a) * kk / NFFT)
            cr, ci = np.cos(phi), -np.sin(phi)
            r0, r1 = k * N2, BD // 2 + k * N2
            mf[gi, r0:r0 + N2, r0:r0 + N2] = cr
            mf[gi, r0:r0 + N2, r1:r1 + N2] = -ci
            mf[gi, r1:r1 + N2, r0:r0 + N2] = ci
            mf[gi, r1:r1 + N2, r1:r1 + N2] = cr
            br, bi = np.cos(phi).T, np.sin(phi).T
            mb[gi, r0:r0 + N2, r0:r0 + N2] = br
            mb[gi, r0:r0 + N2, r1:r1 + N2] = -bi
            mb[gi, r1:r1 + N2, r0:r0 + N2] = bi
            mb[gi, r1:r1 + N2, r1:r1 + N2] = br
    f = lambda t: jnp.asarray(t, dtype=F32)
    return f(f1), f(g3[:, :NYQ]), f(g3[:, NYQ:NYQ + 1]), f(mf), f(mb)


def _stage1(sig, f1_ref, z_s):
    for j in range(N2):
        zj = jnp.dot(f1_ref[...], sig(j), preferred_element_type=F32)
        for c in range(z_s.shape[0]):
            z_s[c, pl.ds(j, F1R, stride=N2), :] = zj[:, c * LANES:(c + 1) * LANES]


def _group_rows(z_s, r0):
    return jnp.concatenate([z_s[c, pl.ds(r0, BD), :] for c in range(z_s.shape[0])], axis=1)


def _fspec_kernel(hp_ref, f1_ref, mf_ref, kf_ref, z_s):
    half = BD // 2
    for s in range(2):
        _stage1(lambda j: hp_ref[s, j], f1_ref, z_s)

        def body(g, carry):
            r0 = pl.multiple_of(g * BD, BD)
            x = jnp.dot(mf_ref[g], _group_rows(z_s, r0).astype(BF16), preferred_element_type=F32)
            if s == 0:
                kf_ref[pl.ds(r0, BD), :] = x
            else:
                kf_ref[pl.ds(r0, half), :] += x[0:half]
                kf_ref[pl.ds(r0 + half, half), :] -= x[half:BD]
            return carry

        lax.fori_loop(0, NG, body, 0, unroll=GROUP_UNROLL)


def _fspec(hp, f1b, mfb):
    C = hp.shape[-1]
    half = N1 // 2
    return pl.pallas_call(
        _fspec_kernel,
        out_shape=jax.ShapeDtypeStruct((ZR, C), F32),
        grid=(C // CT,),
        in_specs=[pl.BlockSpec((2, N2, half, CT), lambda t: (0, 0, 0, t)),
                  _const_spec((F1R, half)), _const_spec((NG, BD, BD))],
        out_specs=pl.BlockSpec((ZR, CT), lambda t: (0, t)),
        scratch_shapes=[pltpu.VMEM((CT // LANES, ZR, LANES), F32)],
        compiler_params=_cp(("parallel",)),
        name="fspec",
    )(hp, f1b, mfb)


def _lconv_kernel(up_ref, x0_ref, kf_ref, f1_ref, mf_ref, mb_ref, g3_ref, g3n_ref, ds_ref, o_ref, z_s, y_s):
    half = BD // 2
    nsl = z_s.shape[0]
    _stage1(lambda j: up_ref[0, j], f1_ref, z_s)

    def body(g, carry):
        r0 = pl.multiple_of(g * BD, BD)
        x = jnp.dot(mf_ref[g], _group_rows(z_s, r0).astype(BF16), preferred_element_type=F32)
        kf = kf_ref[pl.ds(r0, BD), :]
        xr, xi = x[0:half], x[half:BD]
        kr, ki = kf[0:half], kf[half:BD]
        y = jnp.concatenate([xr * kr - xi * ki, xr * ki + xi * kr], axis=0)
        v = jnp.dot(mb_ref[g], y.astype(BF16), preferred_element_type=F32)
        for c in range(nsl):
            z_s[c, pl.ds(r0, BD), :] = v[:, c * LANES:(c + 1) * LANES]
        return carry

    lax.fori_loop(0, NG, body, 0, unroll=GROUP_UNROLL)

    for j in range(N2):
        vj = jnp.concatenate([z_s[c, pl.ds(j, F1R, stride=N2), :] for c in range(nsl)], axis=1)
        yj = (jnp.dot(g3_ref[...], vj[0:NYQ].astype(BF16), preferred_element_type=F32)
              + g3n_ref[...] * vj[NYQ:NYQ + 1] + up_ref[0, j].astype(F32) * ds_ref[...])
        for c in range(nsl):
            y_s[c, pl.ds(j, N1 // 2, stride=N2), :] = yj[:, c * LANES:(c + 1) * LANES]
    for c in range(nsl):
        sl = slice(c * LANES, (c + 1) * LANES)
        o_ref[0, :, sl] = (x0_ref[0, :, sl].astype(F32) * y_s[c]).astype(o_ref.dtype)


def _lconv(up, x0, kf, f1b, mfb, mbb, g3b, g3n, dskip):
    B, L, C = x0.shape
    half = N1 // 2
    return pl.pallas_call(
        _lconv_kernel,
        out_shape=jax.ShapeDtypeStruct((B, L, C), BF16),
        grid=(C // CT, B),
        in_specs=[pl.BlockSpec((1, N2, half, CT), lambda t, b: (b, 0, 0, t)),
                  pl.BlockSpec((1, L, CT), lambda t, b: (b, 0, t)),
                  pl.BlockSpec((ZR, CT), lambda t, b: (0, t)),
                  _const_spec((F1R, half)), _const_spec((NG, BD, BD)), _const_spec((NG, BD, BD)),
                  _const_spec((half, NYQ)), _const_spec((half, 1)),
                  pl.BlockSpec((1, CT), lambda t, b: (0, t))],
        out_specs=pl.BlockSpec((1, L, CT), lambda t, b: (b, 0, t)),
        scratch_shapes=[pltpu.VMEM((CT // LANES, ZR, LANES), F32), pltpu.VMEM((CT // LANES, L, LANES), F32)],
        compiler_params=_cp(("parallel", "parallel")),
        name="lconv",
    )(up, x0, kf, f1b, mfb, mbb, g3b, g3n, dskip.reshape(1, C))


def _merge_kernel(hf_ref, hb_ref, xc_ref, z_ref, yb_ref, gab_ref, x_ref, mod_ref, ng_ref, sk_ref,
                  wpa_ref, wpb_ref, wo_ref, o_ref):
    h = hf_ref[0].astype(F32) + hb_ref[0].astype(F32)
    xc = xc_ref[0].astype(F32)
    parts = []
    for k in range(A_HEADS):
        sl = slice(k * DH, (k + 1) * DH)
        hh = h[:, sl]
        hn = hh * lax.rsqrt(jnp.mean(hh * hh, axis=-1, keepdims=True) + EPS)
        parts.append(hn * ng_ref[:, sl] + sk_ref[:, sl] * xc[:, sl])
    ya = jax.nn.sigmoid(z_ref[0].astype(F32)) * jnp.concatenate(parts, axis=1)
    gab = gab_ref[0].astype(F32)
    mix = (jax.nn.sigmoid(gab[:, 0:D_MODEL]) * jnp.dot(ya.astype(BF16), wpa_ref[...], preferred_element_type=F32)
           + jax.nn.sigmoid(gab[:, D_MODEL:]) * jnp.dot(yb_ref[0], wpb_ref[...], preferred_element_type=F32))
    out = jnp.dot(mix.astype(BF16), wo_ref[...], preferred_element_type=F32)
    o_ref[0] = x_ref[0] + mod_ref[0, 5:6, :] * out


def _merge(hf, hb, xc, z, yb, gab, x, mods, a_norm_g, a_skip, wpa, wpb, wo, *, tm):
    B, L, D = x.shape
    tok = lambda w: pl.BlockSpec((1, tm, w), lambda b, i: (b, i, 0))
    return pl.pallas_call(
        _merge_kernel,
        out_shape=jax.ShapeDtypeStruct((B, L, D), F32),
        grid=(B, L // tm),
        in_specs=[tok(D_A), tok(D_A),
                  tok(D_A), tok(D_A), tok(D_B), tok(2 * D_MODEL), tok(D),
                  pl.BlockSpec((1, N_MOD, D), lambda b, i: (b, 0, 0)),
                  _const_spec((1, D_A)), _const_spec((1, D_A)),
                  _const_spec((D_A, D)), _const_spec((D_B, D)), _const_spec((D, D))],
        out_specs=tok(D),
        compiler_params=_cp(("parallel", "parallel")),
        name="merge",
    )(hf, hb, xc, z, yb, gab, x, mods, a_norm_g.reshape(1, D_A), a_skip.reshape(1, D_A), wpa, wpb, wo)


def kernel(x, c, ctx, c_ctx, w_ada, b_ada, norm_g, ffn1_up, ffn1_down, ffn2_up, ffn2_down, w_in, a_conv_w, a_conv_b, a_wq, a_wk, a_wv, a_w_gate, a_b_gate, a_norm_g, a_skip, b_conv_w, b_conv_b, b_filt_w1, b_filt_b1, b_filt_w2, b_filt_b2, b_filt_w3, b_filt_b3, b_filt_w4, b_filt_freq, b_skip, w_pa, w_pb, w_out, final_g):
    B, L, D = x.shape
    Lc = ctx.shape[1]
    assert w_ada.shape[0] == 1, "single-layer stack"
    assert 2 * L == NFFT and D == D_MODEL and L // GRID_W == GRID_W
    TM = 512
    FFN_TM = 1024

    c8 = jnp.zeros((8, D), F32).at[0:B].set(c).at[B].set(c_ctx)
    mods = _mods(c8, w_ada[0], b_ada[0]).reshape(8, N_MOD, D)
    ng = norm_g[0]
    row_b = lambda b: b
    row_ctx = lambda b: B

    up1 = ffn1_up[0].astype(BF16)
    dn1 = ffn1_down[0].astype(BF16)
    up2 = ffn2_up[0].astype(BF16)
    dn2 = ffn2_down[0].astype(BF16)
    w_in_b = w_in[0].astype(BF16)

    x1 = _ffn(x, mods, row_b, ng, up1, dn1, sub=0, tm=FFN_TM, pos=_pos_table())
    ctx1 = _ffn(ctx.reshape(1, B * Lc, D), mods, row_ctx, ng, up1, dn1, sub=0, tm=TM)

    widths = (D_A, D_A, 3 * D_B, 2 * D_MODEL)
    xm, z, hy, gab = _proj(x1, mods, row_b, ng, w_in_b, widths, tm=TM)
    (xm_c,) = _proj(ctx1, mods, row_ctx, ng, w_in_b, (D_A,), tm=TM)

    bq, bkt, bv, gqk, gv = _qkvw(a_wq[0], a_wk[0], a_wv[0], a_w_gate[0])
    fw = (a_conv_w[0], a_conv_b[0], bq, bkt, bv, gqk, gv, a_b_gate[0])
    q, xc, kt, v, g, gl = _feat(xm, *fw, tm=TM)
    kt_c, v_c, g_c, gl_c = _feat(xm_c.reshape(B, Lc, D_A), *fw, tm=Lc, with_q=False)

    c0, n0, m0 = _ctxstate(kt_c, v_c, g_c, gl_c)
    hf, hb = _scan(q, kt, v, g, gl, c0, n0, m0)

    up, x0c = _hyconv(hy, b_conv_w[0], b_conv_b[0], tm=TM)
    f1, g3, g3n, mf, mb = _dft_tables()
    f1b, g3b, mfb, mbb = (a.astype(BF16) for a in (f1, g3, mf, mb))
    hp = _filters(L, b_filt_w1[0], b_filt_b1[0], b_filt_w2[0], b_filt_b2[0], b_filt_w3[0], b_filt_b3[0],
                  b_filt_w4[0], b_filt_freq[0], tm=TM)
    kf = _fspec(hp, f1b, mfb)
    yb = _lconv(up, x0c, kf, f1b, mfb, mbb, g3b, g3n, b_skip[0])

    x2 = _merge(hf, hb, xc, z, yb, gab, x1, mods, a_norm_g[0], a_skip[0],
                w_pa[0].astype(BF16), w_pb[0].astype(BF16), w_out[0].astype(BF16), tm=TM)

    return _ffn(x2, mods, row_b, ng, up2, dn2, sub=2, tm=FFN_TM, final_g=final_g)
```

```python
import functools
import math

import numpy as np
import jax
import jax.numpy as jnp
from jax import lax
from jax.experimental import pallas as pl
from jax.experimental.pallas import tpu as pltpu

F32 = jnp.float32
BF16 = jnp.bfloat16

D_MODEL = 1024
D_A = 2048
A_HEADS = 4
DH = D_A // A_HEADS
QKV_BLOCK = 4
D_B = 1024
D_FF = 2816
EPS = 1e-6
N_MOD = 9
GRID_W = 64
CHUNK = 256
HEADS_PER_STEP = 2
SCAN_BLOCK = 2
FILTER_BANDS = 16
FILTER_HIDDEN = 64
DECAY_TARGET = 1e-2
MAX_DECAY = math.log(DECAY_TARGET) / 0.3
MIN_DECAY = math.log(DECAY_TARGET) / 1.5
FILTER_SHIFT = 0.05

LANES = 128
SUBLANES = 8
BD = 256

N2 = SUBLANES
NFFT = 8192
N1 = NFFT // N2
KPG = BD // (2 * N2)
NG = (N1 // 2) // KPG + 1
F1R = NG * 2 * KPG
ZR = F1R * N2
NYQ = (NG - 1) * 2 * KPG
CT = 256
GROUP_UNROLL = 11

VMEM_LIMIT = 56 * 1024 * 1024
HIGHEST = lax.Precision.HIGHEST
NEG = -1e30


def _cp(sem):
    return pltpu.CompilerParams(dimension_semantics=sem, vmem_limit_bytes=VMEM_LIMIT)


def _const_spec(shape):
    nd = len(shape)
    return pl.BlockSpec(shape, lambda *_: (0,) * nd, pipeline_mode=pl.Buffered(1))


def _silu(x):
    return x * jax.nn.sigmoid(x)


def _rms(x, g):
    return x * lax.rsqrt(jnp.mean(x * x, axis=-1, keepdims=True) + EPS) * g


def _mods_kernel(c_ref, w_ref, b_ref, o_ref):
    cs = _silu(c_ref[...])
    o_ref[...] = jnp.dot(cs, w_ref[...], preferred_element_type=F32, precision=HIGHEST) + b_ref[...]


def _mods(c8, w_ada, b_ada):
    n = w_ada.shape[1]
    tn = 1024
    return pl.pallas_call(
        _mods_kernel,
        out_shape=jax.ShapeDtypeStruct((8, n), F32),
        grid=(n // tn,),
        in_specs=[pl.BlockSpec((8, D_MODEL), lambda j: (0, 0)),
                  pl.BlockSpec((D_MODEL, tn), lambda j: (0, j)),
                  pl.BlockSpec((1, tn), lambda j: (0, j))],
        out_specs=pl.BlockSpec((8, tn), lambda j: (0, j)),
        compiler_params=_cp(("parallel",)),
        name="mods",
    )(c8, w_ada, b_ada.reshape(1, n))


def _postab_kernel(om_ref, o_ref):
    nf = om_ref.shape[1]
    p = lax.broadcasted_iota(jnp.int32, (GRID_W, nf), 0).astype(F32)
    a = p * om_ref[...]
    o_ref[:, 0:nf] = jnp.sin(a)
    o_ref[:, nf:2 * nf] = jnp.cos(a)


def _pos_table():
    nf = D_MODEL // 4
    omega = 1.0 / (10000.0 ** (jnp.arange(nf, dtype=F32) / nf))
    return pl.pallas_call(
        _postab_kernel,
        out_shape=jax.ShapeDtypeStruct((GRID_W, 2 * nf), F32),
        name="postab",
    )(omega.reshape(1, nf))


FF_CHUNKS = ((0, 6 * BD), (6 * BD, D_FF))


def _ffn_kernel(*refs, sub, has_pos, final):
    it = iter(refs)
    x_ref = next(it)
    pos_ref = next(it) if has_pos else None
    mod_ref = next(it)
    g_ref = next(it)
    up_ref = next(it)
    dn_ref = next(it)
    fg_ref = next(it) if final else None
    o_ref = next(it)

    x = x_ref[0]
    if has_pos:
        tm = x.shape[0]
        per = tm // GRID_W
        half = pos_ref.shape[1]
        i = pl.program_id(1)
        ecol = pos_ref[...]
        rows = []
        for q in range(per):
            erow = jnp.broadcast_to(pos_ref[pl.ds(i * per + q, 1), :], (GRID_W, half))
            rows.append(jnp.concatenate([erow, ecol], axis=1))
        x = x + jnp.concatenate(rows, axis=0)
    shift = mod_ref[0, 3 * sub:3 * sub + 1, :]
    scale = mod_ref[0, 3 * sub + 1:3 * sub + 2, :]
    gate = mod_ref[0, 3 * sub + 2:3 * sub + 3, :]
    h = _rms(x, g_ref[sub:sub + 1, :]) * (1.0 + scale) + shift
    hb = h.astype(BF16)
    acc = jnp.zeros(x.shape, F32)
    for lo, hi in FF_CHUNKS:
        gg = jnp.dot(hb, up_ref[:, lo:hi], preferred_element_type=F32)
        uu = jnp.dot(hb, up_ref[:, D_FF + lo:D_FF + hi], preferred_element_type=F32)
        a = (_silu(gg) * uu).astype(BF16)
        acc = acc + jnp.dot(a, dn_ref[lo:hi, :], preferred_element_type=F32)
    y = x + 0.5 * gate * acc
    if final:
        y = _rms(y, fg_ref[...])
    o_ref[0] = y


def _ffn(x, mods, mod_row, norm_g, up_b, dn_b, *, sub, tm, pos=None, final_g=None):
    B, L, D = x.shape
    args = [x]
    specs = [pl.BlockSpec((1, tm, D), lambda b, i: (b, i, 0))]
    if pos is not None:
        args.append(pos)
        specs.append(_const_spec(pos.shape))
    args += [mods, norm_g, up_b, dn_b]
    specs += [pl.BlockSpec((1, N_MOD, D), lambda b, i: (mod_row(b), 0, 0)),
              _const_spec((3, D)), _const_spec((D, 2 * D_FF)), _const_spec((D_FF, D))]
    if final_g is not None:
        args.append(final_g.reshape(1, D))
        specs.append(_const_spec((1, D)))
    return pl.pallas_call(
        functools.partial(_ffn_kernel, sub=sub, has_pos=pos is not None, final=final_g is not None),
        out_shape=jax.ShapeDtypeStruct((B, L, D), F32),
        grid=(B, L // tm),
        in_specs=specs,
        out_specs=pl.BlockSpec((1, tm, D), lambda b, i: (b, i, 0)),
        compiler_params=_cp(("parallel", "parallel")),
        name=f"ffn{sub}",
    )(*args)


def _proj_kernel(x_ref, mod_ref, g_ref, w_ref, *o_refs, widths):
    x = x_ref[0]
    shift = mod_ref[0, 3:4, :]
    scale = mod_ref[0, 4:5, :]
    hb = (_rms(x, g_ref[1:2, :]) * (1.0 + scale) + shift).astype(BF16)
    lo = 0
    for o_ref, w in zip(o_refs, widths):
        o_ref[0] = jnp.dot(hb, w_ref[:, lo:lo + w], preferred_element_type=F32).astype(o_ref.dtype)
        lo += w


def _proj(x, mods, mod_row, norm_g, w_b, widths, *, tm):
    B, L, D = x.shape
    n = sum(widths)
    return pl.pallas_call(
        functools.partial(_proj_kernel, widths=widths),
        out_shape=[jax.ShapeDtypeStruct((B, L, w), BF16) for w in widths],
        grid=(B, L // tm),
        in_specs=[pl.BlockSpec((1, tm, D), lambda b, i: (b, i, 0)),
                  pl.BlockSpec((1, N_MOD, D), lambda b, i: (mod_row(b), 0, 0)),
                  _const_spec((3, D)), _const_spec((D, n))],
        out_specs=[pl.BlockSpec((1, tm, w), lambda b, i: (b, i, 0)) for w in widths],
        compiler_params=_cp(("parallel", "parallel")),
        name="proj",
    )(x, mods, norm_g, w_b)


HALO = 16


def _shift_matrix():
    return jnp.concatenate([jnp.eye(BD, k=-1, dtype=BF16), jnp.eye(BD, k=1, dtype=BF16)], axis=0)


def _conv3(xb, prev_row, next_row, w_ref, b_ref, s_ref):
    T, C = xb.shape
    w0, w1, w2, b = w_ref[0:1, :], w_ref[1:2, :], w_ref[2:3, :], b_ref[...]
    x = xb.astype(F32)
    r8 = lax.broadcasted_iota(jnp.int32, (SUBLANES, C), 0)
    out = []
    for r0 in range(0, T, BD):
        r1 = r0 + BD
        sh = jnp.dot(s_ref[...], xb[r0:r1], preferred_element_type=F32)
        y = sh[0:BD] * w0 + x[r0:r1] * w1 + sh[BD:2 * BD] * w2 + b
        before = prev_row if r0 == 0 else x[r0 - 1:r0]
        after = next_row if r1 == T else x[r1:r1 + 1]
        first = before * w0 + x[r0:r0 + 1] * w1 + x[r0 + 1:r0 + 2] * w2 + b
        last = x[r1 - 2:r1 - 1] * w0 + x[r1 - 1:r1] * w1 + after * w2 + b
        top = jnp.where(r8 == 0, first, y[0:SUBLANES])
        bot = jnp.where(r8 == SUBLANES - 1, last, y[BD - SUBLANES:BD])
        out += [top, y[SUBLANES:BD - SUBLANES], bot]
    return jnp.concatenate(out, axis=0)


def _halo_rows(prev_ref, next_ref):
    i = pl.program_id(1)
    n = pl.num_programs(1)
    p = prev_ref[0, HALO - 1:HALO, :].astype(F32)
    q = next_ref[0, 0:1, :].astype(F32)
    p = jnp.where(i == 0, 0.0, p)
    q = jnp.where(i == n - 1, 0.0, q)
    return p, q


def _halo_specs(tm, L, C):
    r = tm // HALO
    nb = L // HALO
    return [pl.BlockSpec((1, HALO, C), lambda b, i: (b, jnp.maximum(i * r - 1, 0), 0)),
            pl.BlockSpec((1, tm, C), lambda b, i: (b, i, 0)),
            pl.BlockSpec((1, HALO, C), lambda b, i: (b, jnp.minimum((i + 1) * r, nb - 1), 0))]


def _qkvw_kernel(wq_ref, wk_ref, wv_ref, wg_ref, bq_ref, bkt_ref, bv_ref, gqk_ref, gv_ref):
    r = lax.broadcasted_iota(jnp.int32, (BD, BD), 0)
    c = lax.broadcasted_iota(jnp.int32, (BD, BD), 1)
    shift = QKV_BLOCK.bit_length() - 1
    same_block = (r >> shift) == (c >> shift)
    col_in_block = c & (QKV_BLOCK - 1)

    def tile(w_ref):
        w2 = w_ref[0]
        a = jnp.zeros((BD, BD), F32)
        for j in range(QKV_BLOCK):
            a = a + jnp.where(col_in_block == j, w2[:, j:j + 1], 0.0)
        return jnp.where(same_block, a, 0.0)

    tq = tile(wq_ref)
    tk = tile(wk_ref)
    tv = tile(wv_ref)
    bq_ref[0] = tq.astype(BF16)
    bkt_ref[0] = tk.T.astype(BF16)
    bv_ref[0] = tv.astype(BF16)
    gqk_ref[...] = (jnp.dot(tq, wg_ref[0], preferred_element_type=F32, precision=HIGHEST)
                    + jnp.dot(tk, wg_ref[1], preferred_element_type=F32, precision=HIGHEST)).astype(BF16)
    gv_ref[...] = jnp.dot(tv, wg_ref[2], preferred_element_type=F32, precision=HIGHEST).astype(BF16)


def _qkvw(wq, wk, wv, w_gate):
    nb = D_A // BD
    w2 = lambda w: w.reshape(nb, BD, QKV_BLOCK)
    wspec = pl.BlockSpec((1, BD, QKV_BLOCK), lambda j: (j, 0, 0))
    tspec = pl.BlockSpec((1, BD, BD), lambda j: (j, 0, 0))
    gspec = pl.BlockSpec((BD, 16), lambda j: (j, 0))
    tile = jax.ShapeDtypeStruct((nb, BD, BD), BF16)
    gw = jax.ShapeDtypeStruct((D_A, 16), BF16)
    return pl.pallas_call(
        _qkvw_kernel,
        out_shape=[tile, tile, tile, gw, gw],
        grid=(nb,),
        in_specs=[wspec, wspec, wspec, pl.BlockSpec((3, BD, 16), lambda j: (0, j, 0))],
        out_specs=[tspec, tspec, tspec, gspec, gspec],
        compiler_params=_cp(("parallel",)),
        name="qkvw",
    )(w2(wq), w2(wk), w2(wv), w_gate.reshape(3, D_A, 16))


def _feat_kernel(prev_ref, xm_ref, next_ref, cw_ref, cb_ref, sh_ref, bq_ref, bkt_ref, bv_ref, gqk_ref, gv_ref,
                 gb_ref, *o_refs, with_q):
    if with_q:
        q_ref, xc_ref, kt_ref, v_ref, g_ref, gl_ref = o_refs
    else:
        kt_ref, v_ref, g_ref, gl_ref = o_refs
    xmb = xm_ref[0]
    p, n = _halo_rows(prev_ref, next_ref)
    xc = _silu(_conv3(xmb, p, n, cw_ref, cb_ref, sh_ref))
    xcb = xc.astype(BF16)
    if with_q:
        xc_ref[0] = xcb
    g = (jnp.dot(xcb, gqk_ref[...], preferred_element_type=F32)
         + jnp.dot(xmb, gv_ref[...], preferred_element_type=F32) + gb_ref[...])
    g_ref[0] = g
    gl_ref[0] = g.T
    for j in range(D_A // BD):
        sl = slice(j * BD, (j + 1) * BD)
        if with_q:
            q = jnp.dot(xcb[:, sl], bq_ref[j], preferred_element_type=F32)
            q_ref[0, :, sl] = (q * (DH ** -0.5)).astype(BF16)
        kt_ref[0, sl, :] = lax.dot_general(bkt_ref[j], xcb[:, sl], (((1,), (1,)), ((), ())),
                                          preferred_element_type=F32).astype(BF16)
        v_ref[0, :, sl] = jnp.dot(xmb[:, sl], bv_ref[j], preferred_element_type=F32).astype(BF16)


def _feat(xm, cw, cb, bq, bkt, bv, gqk, gv, gb, *, tm, with_q=True):
    B, L, _ = xm.shape
    nb = D_A // BD
    tok = jax.ShapeDtypeStruct((B, L, D_A), BF16)
    tok_spec = pl.BlockSpec((1, tm, D_A), lambda b, i: (b, i, 0))
    shapes = [jax.ShapeDtypeStruct((B, D_A, L), BF16), tok, jax.ShapeDtypeStruct((B, L, 16), F32),
              jax.ShapeDtypeStruct((B, 16, L), F32)]
    specs = [pl.BlockSpec((1, D_A, tm), lambda b, i: (b, 0, i)), tok_spec,
             pl.BlockSpec((1, tm, 16), lambda b, i: (b, i, 0)), pl.BlockSpec((1, 16, tm), lambda b, i: (b, 0, i))]
    if with_q:
        shapes = [tok, tok] + shapes
        specs = [tok_spec, tok_spec] + specs
    return pl.pallas_call(
        functools.partial(_feat_kernel, with_q=with_q),
        out_shape=shapes,
        grid=(B, L // tm),
        in_specs=_halo_specs(tm, L, D_A) + [
            _const_spec((3, D_A)), _const_spec((1, D_A)), _const_spec((2 * BD, BD)),
            _const_spec((nb, BD, BD)), _const_spec((nb, BD, BD)), _const_spec((nb, BD, BD)),
            _const_spec((D_A, 16)), _const_spec((D_A, 16)), _const_spec((1, 16))],
        out_specs=specs,
        compiler_params=_cp(("parallel", "parallel")),
        name="feat" if with_q else "feat_ctx",
    )(xm, xm, xm, cw, cb.reshape(1, D_A), _shift_matrix(), bq, bkt, bv, gqk, gv, gb.reshape(1, 16))


def _gate_vectors(gt, gl, idx, sign):
    T = gt.shape[0]
    sub = lax.broadcasted_iota(jnp.int32, gl.shape, 0)
    lane = lax.broadcasted_iota(jnp.int32, gt.shape, 1)
    ig_row = jnp.sum(jnp.where(sub == idx, gl, 0.0), axis=0, keepdims=True)
    fg_row = jnp.sum(jnp.where(sub == idx + A_HEADS, gl, 0.0), axis=0, keepdims=True)
    fg_col = jnp.sum(jnp.where(lane == idx + A_HEADS, gt, 0.0), axis=1, keepdims=True)
    lf_row = jax.nn.log_sigmoid(fg_row)
    lf_col = jax.nn.log_sigmoid(fg_col)
    r = lax.broadcasted_iota(jnp.int32, (T, T), 0)
    c = lax.broadcasted_iota(jnp.int32, (T, T), 1)
    mask = sign * (r - c) >= 0
    mask_t = sign * (c - r) >= 0
    b_col = jnp.sum(jnp.where(mask, lf_row, 0.0), axis=1, keepdims=True)
    b_row = jnp.sum(jnp.where(mask_t, lf_col, 0.0), axis=0, keepdims=True)
    b_last = jnp.sum(lf_row, axis=1, keepdims=True)
    return ig_row, b_col, b_row, b_last, mask


def _state_step(kt, v, ig_row, b_row, b_last, m_prev):
    log_w = b_last - b_row + ig_row
    m_new = jnp.maximum(b_last + m_prev, jnp.max(log_w, axis=1, keepdims=True))
    decay = jnp.exp(b_last + m_prev - m_new)
    w = jnp.exp(log_w - m_new)
    kw = kt.astype(F32) * w
    dC = jnp.dot(kw.astype(BF16), v, preferred_element_type=F32)
    dn = jnp.sum(kw, axis=1, keepdims=True)
    return decay, m_new, dC, dn


def _ctxstate_kernel(kt_ref, v_ref, gt_ref, gl_ref, c_ref, n_ref, m_ref):
    hp = pl.program_id(1)
    m0 = jnp.zeros((1, 1), F32)
    for d in range(2):
        for hh in range(HEADS_PER_STEP):
            hsl = slice(hh * DH, (hh + 1) * DH)
            head = hp * HEADS_PER_STEP + hh
            ig_row, _, b_row, b_last, _ = _gate_vectors(gt_ref[0], gl_ref[0], 8 * d + head, 1 - 2 * d)
            _, m_new, dC, dn = _state_step(kt_ref[0, hsl, :], v_ref[0, :, hsl], ig_row, b_row, b_last, m0)
            c_ref[d, 0, hh] = dC
            n_ref[d, 0, hh] = jnp.broadcast_to(dn, (DH, 128))
            m_ref[d, 0, hh] = jnp.broadcast_to(m_new, (8, 128))


def _ctxstate(kt, v, gt, gl):
    B, _, Lc = kt.shape
    H = A_HEADS
    hs = HEADS_PER_STEP
    return pl.pallas_call(
        _ctxstate_kernel,
        out_shape=[jax.ShapeDtypeStruct((2, B, H, DH, DH), F32),
                   jax.ShapeDtypeStruct((2, B, H, DH, 128), F32),
                   jax.ShapeDtypeStruct((2, B, H, 8, 128), F32)],
        grid=(B, H // hs),
        in_specs=[pl.BlockSpec((1, hs * DH, Lc), lambda b, h: (b, h, 0)),
                  pl.BlockSpec((1, Lc, hs * DH), lambda b, h: (b, 0, h)),
                  pl.BlockSpec((1, Lc, 16), lambda b, h: (b, 0, 0)),
                  pl.BlockSpec((1, 16, Lc), lambda b, h: (b, 0, 0))],
        out_specs=[pl.BlockSpec((2, 1, hs, DH, DH), lambda b, h: (0, b, h, 0, 0)),
                   pl.BlockSpec((2, 1, hs, DH, 128), lambda b, h: (0, b, h, 0, 0)),
                   pl.BlockSpec((2, 1, hs, 8, 128), lambda b, h: (0, b, h, 0, 0))],
        compiler_params=_cp(("parallel", "parallel")),
        name="ctxstate",
    )(kt, v, gt, gl)


def _scan_kernel(qf_ref, ktf_ref, vf_ref, gtf_ref, glf_ref, qb_ref, ktb_ref, vb_ref, gtb_ref, glb_ref,
                 c0_ref, n0_ref, m0_ref, hf_ref, hb_ref, *scratch):
    hp = pl.program_id(1)
    j = pl.program_id(2)
    io = ((qf_ref, ktf_ref, vf_ref, gtf_ref, glf_ref, hf_ref), (qb_ref, ktb_ref, vb_ref, gtb_ref, glb_ref, hb_ref))
    chains = [(d, hh) + io[d] + tuple(scratch[5 * (d * HEADS_PER_STEP + hh):5 * (d * HEADS_PER_STEP + hh) + 5])
              for d in range(2) for hh in range(HEADS_PER_STEP)]

    @pl.when(j == 0)
    def _():
        for d, hh, _, _, _, _, _, _, c_s, cq_s, n_s, nq_s, m_s in chains:
            c0 = c0_ref[d, 0, hh]
            c_s[...] = c0
            cq_s[...] = c0.astype(BF16)
            n0 = n0_ref[d, 0, hh]
            n_s[...] = n0
            nq_s[...] = n0.astype(BF16)
            m_s[...] = m0_ref[d, 0, hh]

    for sub, (d, hh, q_ref, kt_ref, v_ref, gt_ref, gl_ref, h_ref, c_s, cq_s, n_s, nq_s, m_s) in (
            (sub, chain) for sub in range(SCAN_BLOCK) for chain in chains):
        sign = 1 - 2 * d
        hsl = slice(hh * DH, (hh + 1) * DH)
        r0 = (sub if d == 0 else SCAN_BLOCK - 1 - sub) * CHUNK
        rows = slice(r0, r0 + CHUNK)
        q = q_ref[0, rows, hsl]
        kt = kt_ref[0, hsl, rows]
        v = v_ref[0, rows, hsl]
        head = hp * HEADS_PER_STEP + hh
        ig_row, b_col, b_row, b_last, mask = _gate_vectors(gt_ref[0, rows, :], gl_ref[0, :, rows], 8 * d + head, sign)
        m_prev = m_s[0:1, 0:1]

        log_d = jnp.where(mask, b_col - b_row + ig_row, NEG)
        m_inter = b_col + m_prev
        m_t = jnp.maximum(m_inter, jnp.max(log_d, axis=1, keepdims=True))
        dmat = jnp.exp(log_d - m_t)
        inter = jnp.exp(m_inter - m_t)
        s = jnp.dot(q, kt, preferred_element_type=F32) * dmat
        num = (jnp.dot(s.astype(BF16), v, preferred_element_type=F32)
               + inter * jnp.dot(q, cq_s[...], preferred_element_type=F32))
        qn = jnp.dot(q, nq_s[...], preferred_element_type=F32)[:, 0:1]
        den = jnp.sum(s, axis=1, keepdims=True) + inter * qn
        h_ref[0, rows, hsl] = (num / jnp.maximum(jnp.abs(den), jnp.exp(-m_t))).astype(h_ref.dtype)

        decay, m_new, dC, dn = _state_step(kt, v, ig_row, b_row, b_last, m_prev)
        c_new = decay * c_s[...] + dC
        c_s[...] = c_new
        cq_s[...] = c_new.astype(BF16)
        n_new = decay * n_s[...] + dn
        n_s[...] = n_new
        nq_s[...] = n_new.astype(BF16)
        m_s[...] = jnp.broadcast_to(m_new, (8, 128))


def _scan(q, kt, v, gt, gl, c0, n0, m0):
    B, L, _ = q.shape
    hs = HEADS_PER_STEP
    T = SCAN_BLOCK * CHUNK
    nc = L // T
    fwd = lambda j: j
    bwd = lambda j: nc - 1 - j

    def specs(cj):
        return [pl.BlockSpec((1, T, hs * DH), lambda b, h, j: (b, cj(j), h)),
                pl.BlockSpec((1, hs * DH, T), lambda b, h, j: (b, h, cj(j))),
                pl.BlockSpec((1, T, hs * DH), lambda b, h, j: (b, cj(j), h)),
                pl.BlockSpec((1, T, 16), lambda b, h, j: (b, cj(j), 0)),
                pl.BlockSpec((1, 16, T), lambda b, h, j: (b, 0, cj(j)))]

    state = [pltpu.VMEM((DH, DH), F32), pltpu.VMEM((DH, DH), BF16),
             pltpu.VMEM((DH, 128), F32), pltpu.VMEM((DH, 128), BF16), pltpu.VMEM((8, 128), F32)]
    out = jax.ShapeDtypeStruct((B, L, D_A), BF16)
    return pl.pallas_call(
        _scan_kernel,
        out_shape=[out, out],
        grid=(B, A_HEADS // hs, nc),
        in_specs=specs(fwd) + specs(bwd) + [
            pl.BlockSpec((2, 1, hs, DH, DH), lambda b, h, j: (0, b, h, 0, 0)),
            pl.BlockSpec((2, 1, hs, DH, 128), lambda b, h, j: (0, b, h, 0, 0)),
            pl.BlockSpec((2, 1, hs, 8, 128), lambda b, h, j: (0, b, h, 0, 0))],
        out_specs=[pl.BlockSpec((1, T, hs * DH), lambda b, h, j: (b, fwd(j), h)),
                   pl.BlockSpec((1, T, hs * DH), lambda b, h, j: (b, bwd(j), h))],
        scratch_shapes=state * (2 * hs),
        compiler_params=_cp(("parallel", "parallel", "arbitrary")),
        name="scan",
    )(q, kt, v, gt, gl, q, kt, v, gt, gl, c0, n0, m0)


def _store_permuted(o_ref, lead, x, s_ref):
    T, C = x.shape
    for c in range(C // LANES):
        s_ref[c] = x[:, c * LANES:(c + 1) * LANES]
    for c in range(C // LANES):
        for n2 in range(N2):
            o_ref[lead + (n2, slice(None), slice(c * LANES, (c + 1) * LANES))] = (
                s_ref[c, pl.ds(n2, T // N2, stride=N2), :].astype(o_ref.dtype))


def _hyconv_kernel(prev_ref, hy_ref, next_ref, cw_ref, cb_ref, sh_ref, up_ref, x0_ref, s_ref):
    p, n = _halo_rows(prev_ref, next_ref)
    y = _conv3(hy_ref[0], p, n, cw_ref, cb_ref, sh_ref)
    x0_ref[0] = y[:, 0:D_B].astype(BF16)
    _store_permuted(up_ref, (0,), y[:, D_B:2 * D_B] * y[:, 2 * D_B:3 * D_B], s_ref)


def _hyconv(hy, cw, cb, *, tm):
    B, L, C = hy.shape
    return pl.pallas_call(
        _hyconv_kernel,
        out_shape=[jax.ShapeDtypeStruct((B, N2, L // N2, D_B), BF16), jax.ShapeDtypeStruct((B, L, D_B), BF16)],
        grid=(B, L // tm),
        in_specs=_halo_specs(tm, L, C) + [_const_spec((3, C)), _const_spec((1, C)), _const_spec((2 * BD, BD))],
        out_specs=[pl.BlockSpec((1, N2, tm // N2, D_B), lambda b, i: (b, 0, i, 0)),
                   pl.BlockSpec((1, tm, D_B), lambda b, i: (b, i, 0))],
        scratch_shapes=[pltpu.VMEM((D_B // LANES, tm, LANES), F32)],
        compiler_params=_cp(("parallel", "parallel")),
        name="hyconv",
    )(hy, hy, hy, cw, cb.reshape(1, C), _shift_matrix())


def _filt_kernel(w1_ref, b1_ref, w2_ref, b2_ref, w3_ref, b3_ref, w4_ref, fr_ref, o_ref, s_ref, *, L):
    i = pl.program_id(0)
    T = s_ref.shape[1]
    pos = (lax.broadcasted_iota(jnp.int32, (T, 128), 0) + i * T).astype(F32)
    lane = lax.broadcasted_iota(jnp.int32, (T, 128), 1)
    t = pos / (L - 1.0)
    w = (2.0 * math.pi) * pos / L
    band = jnp.where(lane <= FILTER_BANDS, lane - 1, lane - 1 - FILTER_BANDS).astype(F32)
    f = 1e-4 + band * ((FILTER_BANDS - 1 - 1e-4) / (FILTER_BANDS - 1))
    z = jnp.where(lane == 0, t,
                  jnp.where(lane <= FILTER_BANDS, jnp.cos(f * w),
                            jnp.where(lane <= 2 * FILTER_BANDS, -jnp.sin(f * w), 0.0)))
    fr = fr_ref[...]
    a = jnp.sin(fr * (jnp.dot(z, w1_ref[...], preferred_element_type=F32, precision=HIGHEST) + b1_ref[...]))
    a = jnp.sin(fr * (jnp.dot(a, w2_ref[...], preferred_element_type=F32, precision=HIGHEST) + b2_ref[...]))
    a = jnp.sin(fr * (jnp.dot(a, w3_ref[...], preferred_element_type=F32, precision=HIGHEST) + b3_ref[...]))
    hf = jnp.dot(a, w4_ref[...], preferred_element_type=F32, precision=HIGHEST)
    ch = lax.broadcasted_iota(jnp.int32, (1, D_B), 1).astype(F32)
    deltas = jnp.abs(MIN_DECAY + ch * ((MAX_DECAY - MIN_DECAY) / (D_B - 1)))
    window = jnp.exp(-t[:, 0:1] * deltas) + FILTER_SHIFT
    _store_permuted(o_ref, (0,), hf[:, 0:D_B] * window, s_ref)
    _store_permuted(o_ref, (1,), jnp.where(pos[:, 0:1] == 0.0, 0.0, hf[:, D_B:2 * D_B] * window), s_ref)


def _filters(L, w1, b1, w2, b2, w3, b3, w4, freq, *, tm):
    emb = w1.shape[0]
    w1p = jnp.zeros((128, FILTER_HIDDEN), F32).at[0:emb].set(w1)
    row = lambda a: a.reshape(1, -1)
    return pl.pallas_call(
        functools.partial(_filt_kernel, L=L),
        out_shape=jax.ShapeDtypeStruct((2, N2, L // N2, D_B), BF16),
        grid=(L // tm,),
        in_specs=[_const_spec((128, FILTER_HIDDEN)), _const_spec((1, FILTER_HIDDEN)),
                  _const_spec((FILTER_HIDDEN, FILTER_HIDDEN)), _const_spec((1, FILTER_HIDDEN)),
                  _const_spec((FILTER_HIDDEN, FILTER_HIDDEN)), _const_spec((1, FILTER_HIDDEN)),
                  _const_spec((FILTER_HIDDEN, 2 * D_B)), _const_spec((1, FILTER_HIDDEN))],
        out_specs=pl.BlockSpec((2, N2, tm // N2, D_B), lambda i: (0, 0, i, 0)),
        scratch_shapes=[pltpu.VMEM((D_B // LANES, tm, LANES), F32)],
        compiler_params=_cp(("parallel",)),
        name="filt",
    )(w1p, row(b1), w2, row(b2), w3, row(b3), w4, row(freq))


def _dft_tables():
    half = N1 // 2
    g, ri, kl = np.meshgrid(np.arange(NG), np.arange(2), np.arange(KPG), indexing="ij")
    k1 = (KPG * g + kl).reshape(-1).astype(np.float64)
    is_im = ri.reshape(-1).astype(bool)
    valid = k1 <= half
    n1 = np.arange(half, dtype=np.float64)
    th = 2.0 * np.pi * np.outer(k1, n1) / N1
    f1 = np.where(is_im[:, None], -np.sin(th), np.cos(th)) * valid[:, None]
    ck = np.where((k1 == 0) | (k1 == half), 1.0, 2.0) * valid
    g3 = (np.where(is_im[:, None], -np.sin(th), np.cos(th)) * ck[:, None] / NFFT).T
    mf = np.zeros((NG, BD, BD))
    mb = np.zeros((NG, BD, BD))
    a = np.arange(N2, dtype=np.float64)
    for gi in range(NG):
        for k in range(KPG):
            kk = KPG * gi + k
            if kk > half:
                continue
            phi = 2.0 * np.pi * (np.outer(a, a) / N2 + np.outer(np.ones(N2), a) * kk / NFFT)
            cr, ci = np.cos(phi), -np.sin(phi)
            r0, r1 = k * N2, BD // 2 + k * N2
            mf[gi, r0:r0 + N2, r0:r0 + N2] = cr
            mf[gi, r0:r0 + N2, r1:r1 + N2] = -ci
            mf[gi, r1:r1 + N2, r0:r0 + N2] = ci
            mf[gi, r1:r1 + N2, r1:r1 + N2] = cr
            br, bi = np.cos(phi).T, np.sin(phi).T
            mb[gi, r0:r0 + N2, r0:r0 + N2] = br
            mb[gi, r0:r0 + N2, r1:r1 + N2] = -bi
            mb[gi, r1:r1 + N2, r0:r0 + N2] = bi
            mb[gi, r1:r1 + N2, r1:r1 + N2] = br
    f = lambda t: jnp.asarray(t, dtype=F32)
    return f(f1), f(g3[:, :NYQ]), f(g3[:, NYQ:NYQ + 1]), f(mf), f(mb)


def _stage1(sig, f1_ref, z_s):
    for j in range(N2):
        zj = jnp.dot(f1_ref[...], sig(j), preferred_element_type=F32)
        for c in range(z_s.shape[0]):
            z_s[c, pl.ds(j, F1R, stride=N2), :] = zj[:, c * LANES:(c + 1) * LANES]


def _group_rows(z_s, r0):
    return jnp.concatenate([z_s[c, pl.ds(r0, BD), :] for c in range(z_s.shape[0])], axis=1)


def _fspec_kernel(hp_ref, f1_ref, mf_ref, kf_ref, z_s):
    half = BD // 2
    for s in range(2):
        _stage1(lambda j: hp_ref[s, j], f1_ref, z_s)

        def body(g, carry):
            r0 = pl.multiple_of(g * BD, BD)
            x = jnp.dot(mf_ref[g], _group_rows(z_s, r0).astype(BF16), preferred_element_type=F32)
            if s == 0:
                kf_ref[pl.ds(r0, BD), :] = x
            else:
                kf_ref[pl.ds(r0, half), :] += x[0:half]
                kf_ref[pl.ds(r0 + half, half), :] -= x[half:BD]
            return carry

        lax.fori_loop(0, NG, body, 0, unroll=GROUP_UNROLL)


def _fspec(hp, f1b, mfb):
    C = hp.shape[-1]
    half = N1 // 2
    return pl.pallas_call(
        _fspec_kernel,
        out_shape=jax.ShapeDtypeStruct((ZR, C), F32),
        grid=(C // CT,),
        in_specs=[pl.BlockSpec((2, N2, half, CT), lambda t: (0, 0, 0, t)),
                  _const_spec((F1R, half)), _const_spec((NG, BD, BD))],
        out_specs=pl.BlockSpec((ZR, CT), lambda t: (0, t)),
        scratch_shapes=[pltpu.VMEM((CT // LANES, ZR, LANES), F32)],
        compiler_params=_cp(("parallel",)),
        name="fspec",
    )(hp, f1b, mfb)


def _lconv_kernel(up_ref, x0_ref, kf_ref, f1_ref, mf_ref, mb_ref, g3_ref, g3n_ref, ds_ref, o_ref, z_s, y_s):
    half = BD // 2
    nsl = z_s.shape[0]
    _stage1(lambda j: up_ref[0, j], f1_ref, z_s)

    def body(g, carry):
        r0 = pl.multiple_of(g * BD, BD)
        x = jnp.dot(mf_ref[g], _group_rows(z_s, r0).astype(BF16), preferred_element_type=F32)
        kf = kf_ref[pl.ds(r0, BD), :]
        xr, xi = x[0:half], x[half:BD]
        kr, ki = kf[0:half], kf[half:BD]
        y = jnp.concatenate([xr * kr - xi * ki, xr * ki + xi * kr], axis=0)
        v = jnp.dot(mb_ref[g], y.astype(BF16), preferred_element_type=F32)
        for c in range(nsl):
            z_s[c, pl.ds(r0, BD), :] = v[:, c * LANES:(c + 1) * LANES]
        return carry

    lax.fori_loop(0, NG, body, 0, unroll=GROUP_UNROLL)

    for j in range(N2):
        vj = jnp.concatenate([z_s[c, pl.ds(j, F1R, stride=N2), :] for c in range(nsl)], axis=1)
        yj = (jnp.dot(g3_ref[...], vj[0:NYQ].astype(BF16), preferred_element_type=F32)
              + g3n_ref[...] * vj[NYQ:NYQ + 1] + up_ref[0, j].astype(F32) * ds_ref[...])
        for c in range(nsl):
            y_s[c, pl.ds(j, N1 // 2, stride=N2), :] = yj[:, c * LANES:(c + 1) * LANES]
    for c in range(nsl):
        sl = slice(c * LANES, (c + 1) * LANES)
        o_ref[0, :, sl] = (x0_ref[0, :, sl].astype(F32) * y_s[c]).astype(o_ref.dtype)


def _lconv(up, x0, kf, f1b, mfb, mbb, g3b, g3n, dskip):
    B, L, C = x0.shape
    half = N1 // 2
    return pl.pallas_call(
        _lconv_kernel,
        out_shape=jax.ShapeDtypeStruct((B, L, C), BF16),
        grid=(C // CT, B),
        in_specs=[pl.BlockSpec((1, N2, half, CT), lambda t, b: (b, 0, 0, t)),
                  pl.BlockSpec((1, L, CT), lambda t, b: (b, 0, t)),
                  pl.BlockSpec((ZR, CT), lambda t, b: (0, t)),
                  _const_spec((F1R, half)), _const_spec((NG, BD, BD)), _const_spec((NG, BD, BD)),
                  _const_spec((half, NYQ)), _const_spec((half, 1)),
                  pl.BlockSpec((1, CT), lambda t, b: (0, t))],
        out_specs=pl.BlockSpec((1, L, CT), lambda t, b: (b, 0, t)),
        scratch_shapes=[pltpu.VMEM((CT // LANES, ZR, LANES), F32), pltpu.VMEM((CT // LANES, L, LANES), F32)],
        compiler_params=_cp(("parallel", "parallel")),
        name="lconv",
    )(up, x0, kf, f1b, mfb, mbb, g3b, g3n, dskip.reshape(1, C))


def _merge_kernel(hf_ref, hb_ref, xc_ref, z_ref, yb_ref, gab_ref, x_ref, mod_ref, ng_ref, sk_ref,
                  wpa_ref, wpb_ref, wo_ref, o_ref):
    h = (hf_ref[0] + hb_ref[0]).astype(F32)
    xc = xc_ref[0].astype(F32)
    parts = []
    for k in range(A_HEADS):
        sl = slice(k * DH, (k + 1) * DH)
        hh = h[:, sl]
        hn = hh * lax.rsqrt(jnp.mean(hh * hh, axis=-1, keepdims=True) + EPS)
        parts.append(hn * ng_ref[:, sl] + sk_ref[:, sl] * xc[:, sl])
    ya = jax.nn.sigmoid(z_ref[0].astype(F32)) * jnp.concatenate(parts, axis=1)
    gab = gab_ref[0].astype(F32)
    mix = (jax.nn.sigmoid(gab[:, 0:D_MODEL]) * jnp.dot(ya.astype(BF16), wpa_ref[...], preferred_element_type=F32)
           + jax.nn.sigmoid(gab[:, D_MODEL:]) * jnp.dot(yb_ref[0], wpb_ref[...], preferred_element_type=F32))
    out = jnp.dot(mix.astype(BF16), wo_ref[...], preferred_element_type=F32)
    o_ref[0] = x_ref[0] + mod_ref[0, 5:6, :] * out


def _merge(hf, hb, xc, z, yb, gab, x, mods, a_norm_g, a_skip, wpa, wpb, wo, *, tm):
    B, L, D = x.shape
    tok = lambda w: pl.BlockSpec((1, tm, w), lambda b, i: (b, i, 0))
    return pl.pallas_call(
        _merge_kernel,
        out_shape=jax.ShapeDtypeStruct((B, L, D), F32),
        grid=(B, L // tm),
        in_specs=[tok(D_A), tok(D_A),
                  tok(D_A), tok(D_A), tok(D_B), tok(2 * D_MODEL), tok(D),
                  pl.BlockSpec((1, N_MOD, D), lambda b, i: (b, 0, 0)),
                  _const_spec((1, D_A)), _const_spec((1, D_A)),
                  _const_spec((D_A, D)), _const_spec((D_B, D)), _const_spec((D, D))],
        out_specs=tok(D),
        compiler_params=_cp(("parallel", "parallel")),
        name="merge",
    )(hf, hb, xc, z, yb, gab, x, mods, a_norm_g.reshape(1, D_A), a_skip.reshape(1, D_A), wpa, wpb, wo)


def kernel(x, c, ctx, c_ctx, w_ada, b_ada, norm_g, ffn1_up, ffn1_down, ffn2_up, ffn2_down, w_in, a_conv_w, a_conv_b, a_wq, a_wk, a_wv, a_w_gate, a_b_gate, a_norm_g, a_skip, b_conv_w, b_conv_b, b_filt_w1, b_filt_b1, b_filt_w2, b_filt_b2, b_filt_w3, b_filt_b3, b_filt_w4, b_filt_freq, b_skip, w_pa, w_pb, w_out, final_g):
    B, L, D = x.shape
    Lc = ctx.shape[1]
    assert w_ada.shape[0] == 1, "single-layer stack"
    assert 2 * L == NFFT and D == D_MODEL and L // GRID_W == GRID_W
    TM = 512
    FFN_TM = 1024

    c8 = jnp.zeros((8, D), F32).at[0:B].set(c).at[B].set(c_ctx)
    mods = _mods(c8, w_ada[0], b_ada[0]).reshape(8, N_MOD, D)
    ng = norm_g[0]
    row_b = lambda b: b
    row_ctx = lambda b: B

    up1 = ffn1_up[0].astype(BF16)
    dn1 = ffn1_down[0].astype(BF16)
    up2 = ffn2_up[0].astype(BF16)
    dn2 = ffn2_down[0].astype(BF16)
    w_in_b = w_in[0].astype(BF16)

    x1 = _ffn(x, mods, row_b, ng, up1, dn1, sub=0, tm=FFN_TM, pos=_pos_table())
    ctx1 = _ffn(ctx.reshape(1, B * Lc, D), mods, row_ctx, ng, up1, dn1, sub=0, tm=TM)

    widths = (D_A, D_A, 3 * D_B, 2 * D_MODEL)
    xm, z, hy, gab = _proj(x1, mods, row_b, ng, w_in_b, widths, tm=TM)
    (xm_c,) = _proj(ctx1, mods, row_ctx, ng, w_in_b, (D_A,), tm=TM)

    bq, bkt, bv, gqk, gv = _qkvw(a_wq[0], a_wk[0], a_wv[0], a_w_gate[0])
    fw = (a_conv_w[0], a_conv_b[0], bq, bkt, bv, gqk, gv, a_b_gate[0])
    q, xc, kt, v, g, gl = _feat(xm, *fw, tm=TM)
    kt_c, v_c, g_c, gl_c = _feat(xm_c.reshape(B, Lc, D_A), *fw, tm=Lc, with_q=False)

    c0, n0, m0 = _ctxstate(kt_c, v_c, g_c, gl_c)
    hf, hb = _scan(q, kt, v, g, gl, c0, n0, m0)

    up, x0c = _hyconv(hy, b_conv_w[0], b_conv_b[0], tm=TM)
    f1, g3, g3n, mf, mb = _dft_tables()
    f1b, g3b, mfb, mbb = (a.astype(BF16) for a in (f1, g3, mf, mb))
    hp = _filters(L, b_filt_w1[0], b_filt_b1[0], b_filt_w2[0], b_filt_b2[0], b_filt_w3[0], b_filt_b3[0],
                  b_filt_w4[0], b_filt_freq[0], tm=TM)
    kf = _fspec(hp, f1b, mfb)
    yb = _lconv(up, x0c, kf, f1b, mfb, mbb, g3b, g3n, b_skip[0])

    x2 = _merge(hf, hb, xc, z, yb, gab, x1, mods, a_norm_g[0], a_skip[0],
                w_pa[0].astype(BF16), w_pb[0].astype(BF16), w_out[0].astype(BF16), tm=TM)

    return _ffn(x2, mods, row_b, ng, up2, dn2, sub=2, tm=FFN_TM, final_g=final_g)
```
